```python
import math
import jax, jax.numpy as jnp
from jax import lax
import numpy as np

D_MODEL = 1024
BATCH = 4
SEQ = 8192
DEPTH = 2

N_MIXERS = 2
N_HEADS = 16
HEAD_DIM = D_MODEL // N_HEADS
MOBA_BLOCK = 256
MOBA_TOPK = 3
Q_CHUNK = 64
REL_BUCKETS = 32
REL_MAX_DIST = 128
CONV_WIDTH = 31
PEER_HEADS = 8
PEER_NKEYS = 128
PEER_NEXPERTS = PEER_NKEYS * PEER_NKEYS
PEER_KEY_DIM = 256
PEER_HALF = PEER_KEY_DIM // 2
PEER_TOPK = 16
PEER_CHUNK = 128
EPS = 1e-6
NEG = -1e30

kernel_name = "moba_conformer_peer_hybrid"


def rmsnorm(x, g):
    xf = x.astype(jnp.float32)
    y = xf * lax.rsqrt(jnp.mean(xf * xf, axis=-1, keepdims=True) + EPS)
    return (y * g.astype(jnp.float32)).astype(x.dtype)


def layernorm(x, g, b):
    xf = x.astype(jnp.float32)
    mu = jnp.mean(xf, axis=-1, keepdims=True)
    var = jnp.mean(jnp.square(xf - mu), axis=-1, keepdims=True)
    y = (xf - mu) * lax.rsqrt(var + EPS)
    return (y * g.astype(jnp.float32) + b.astype(jnp.float32)).astype(x.dtype)


def t5_bucket(dist):
    max_exact = REL_BUCKETS // 2
    d = jnp.maximum(dist, 0)
    df = jnp.maximum(d, 1).astype(jnp.float32)
    large = max_exact + (jnp.log(df / max_exact) / math.log(REL_MAX_DIST / max_exact)
                         * (REL_BUCKETS - max_exact)).astype(jnp.int32)
    large = jnp.minimum(large, REL_BUCKETS - 1)
    return jnp.where(d < max_exact, d, large)


def moba_attention(h, w_qkv, w_o, rel_bias):
    B, S, D = h.shape
    qkv = h @ w_qkv
    q, k, v = jnp.split(qkv, 3, axis=-1)
    to_heads = lambda t: t.reshape(B, S, N_HEADS, HEAD_DIM).transpose(0, 2, 1, 3)
    q, k, v = to_heads(q), to_heads(k), to_heads(v)
    nb = -(-S // MOBA_BLOCK)
    pad = nb * MOBA_BLOCK - S
    k = jnp.pad(k, ((0, 0), (0, 0), (0, pad), (0, 0)))
    v = jnp.pad(v, ((0, 0), (0, 0), (0, pad), (0, 0)))
    k_blocks = k.reshape(B, N_HEADS, nb, MOBA_BLOCK, HEAD_DIM)
    v_blocks = v.reshape(B, N_HEADS, nb, MOBA_BLOCK, HEAD_DIM)
    k_mean = jnp.mean(k_blocks.astype(jnp.float32), axis=3).astype(k.dtype)
    n_sel = min(MOBA_TOPK, nb)
    scale = HEAD_DIM ** -0.5
    b_ix = jnp.arange(B)[:, None, None, None]
    h_ix = jnp.arange(N_HEADS)[:, None, None, None]
    blk_pos = jnp.arange(MOBA_BLOCK, dtype=jnp.int32)
    n_chunks = S // Q_CHUNK

    def chunk(c):
        q0 = c * Q_CHUNK
        own = q0 // MOBA_BLOCK
        qc = lax.dynamic_slice_in_dim(q, q0, Q_CHUNK, axis=2)
        q_pos = q0 + jnp.arange(Q_CHUNK, dtype=jnp.int32)
        gate = jnp.einsum('bhqd,bhnd->bhqn', qc, k_mean).astype(jnp.float32)
        past = jnp.arange(nb) < own
        gate = jnp.where(past, gate, NEG)
        _, sel = lax.top_k(gate, n_sel)
        sel_valid = sel < own
        k_sel = k_blocks[b_ix, jnp.arange(N_HEADS)[None, :, None, None], sel]
        v_sel = v_blocks[b_ix, jnp.arange(N_HEADS)[None, :, None, None], sel]
        s_sel = jnp.einsum('bhqd,bhqnkd->bhqnk', qc, k_sel).astype(jnp.float32) * scale
        k_pos_sel = sel[..., None] * MOBA_BLOCK + blk_pos
        bucket_sel = t5_bucket(q_pos[None, None, :, None, None] - k_pos_sel)
        bias_sel = rel_bias[h_ix[None], bucket_sel].astype(jnp.float32)
        s_sel = jnp.where(sel_valid[..., None], s_sel + bias_sel, NEG)
        k_own = lax.dynamic_slice_in_dim(k, own * MOBA_BLOCK, MOBA_BLOCK, axis=2)
        v_own = lax.dynamic_slice_in_dim(v, own * MOBA_BLOCK, MOBA_BLOCK, axis=2)
        s_own = jnp.einsum('bhqd,bhkd->bhqk', qc, k_own).astype(jnp.float32) * scale
        dist_own = q_pos[:, None] - (own * MOBA_BLOCK + blk_pos)[None, :]
        bias_own = rel_bias[:, t5_bucket(dist_own)].astype(jnp.float32)
        s_own = jnp.where(dist_own >= 0, s_own + bias_own[None], NEG)
        logits = jnp.concatenate([s_own, s_sel.reshape(B, N_HEADS, Q_CHUNK, n_sel * MOBA_BLOCK)], axis=-1)
        p = jax.nn.softmax(logits, axis=-1)
        p_own = p[..., :MOBA_BLOCK].astype(v.dtype)
        p_sel = p[..., MOBA_BLOCK:].reshape(B, N_HEADS, Q_CHUNK, n_sel, MOBA_BLOCK).astype(v.dtype)
        return (jnp.einsum('bhqk,bhkd->bhqd', p_own, v_own)
                + jnp.einsum('bhqnk,bhqnkd->bhqd', p_sel, v_sel))

    outs = lax.map(chunk, jnp.arange(n_chunks, dtype=jnp.int32))
    o = outs.transpose(1, 0, 3, 2, 4).reshape(B, S, D)
    return o @ w_o


def conformer_conv(h, w_pw1, b_pw1, w_dw, b_dw, ln_g, ln_b, w_pw2, b_pw2):
    a = h @ w_pw1 + b_pw1
    val, gate = jnp.split(a, 2, axis=-1)
    u = val * jax.nn.sigmoid(gate)
    u = lax.conv_general_dilated(u, w_dw[:, None, :], window_strides=(1,),
                                 padding=[(CONV_WIDTH - 1, 0)],
                                 dimension_numbers=('NWC', 'WIO', 'NWC'),
                                 feature_group_count=D_MODEL) + b_dw
    u = jax.nn.silu(layernorm(u, ln_g, ln_b))
    return u @ w_pw2 + b_pw2


def peer(h, w_pq, sub_keys, expert_u, expert_v):
    B, S, D = h.shape
    T = B * S
    xt = h.reshape(T, D)
    q = (xt @ w_pq).reshape(T, PEER_HEADS, 2, PEER_HALF)
    s = jnp.einsum('thcd,hcnd->thcn', q, sub_keys).astype(jnp.float32)
    top_s, top_i = lax.top_k(s, PEER_TOPK)
    cand_s = top_s[:, :, 0, :, None] + top_s[:, :, 1, None, :]
    cand_i = top_i[:, :, 0, :, None] * PEER_NKEYS + top_i[:, :, 1, None, :]
    best_s, best_pos = lax.top_k(cand_s.reshape(T, PEER_HEADS, PEER_TOPK * PEER_TOPK), PEER_TOPK)
    experts = jnp.take_along_axis(cand_i.reshape(T, PEER_HEADS, PEER_TOPK * PEER_TOPK), best_pos, axis=-1)
    gates = jax.nn.softmax(best_s, axis=-1)
    nc = T // PEER_CHUNK

    def chunk(args):
        xc, ec, gc = args
        u = expert_u[ec]
        act = jax.nn.gelu(jnp.einsum('cd,chkd->chk', xc, u))
        w = gc.astype(xc.dtype) * act
        return jnp.einsum('chk,chkd->cd', w, expert_v[ec])

    out = lax.map(chunk, (xt.reshape(nc, PEER_CHUNK, D),
                          experts.reshape(nc, PEER_CHUNK, PEER_HEADS, PEER_TOPK),
                          gates.reshape(nc, PEER_CHUNK, PEER_HEADS, PEER_TOPK)))
    return out.reshape(B, S, D)


def setup_inputs(seed: int = 0) -> dict:
    key = jax.random.key(seed)
    ks = jax.random.split(key, 24)
    n_attn = (DEPTH + 1) // 2
    n_conv = DEPTH // 2
    D = D_MODEL
    nrm = lambda k, shape, s: jax.random.normal(k, shape, jnp.float32) * s
    return {
        "x": nrm(ks[0], (BATCH, SEQ, D), 1.0),
        "rel_bias": nrm(ks[1], (N_HEADS, REL_BUCKETS), 0.2),
        "norm_mix": 1.0 + nrm(ks[2], (DEPTH, D), 0.05),
        "norm_ffn": 1.0 + nrm(ks[3], (DEPTH, D), 0.05),
        "attn_w_qkv": nrm(ks[4], (n_attn, D, 3 * D), D ** -0.5),
        "attn_w_o": nrm(ks[5], (n_attn, D, D), D ** -0.5),
        "conv_w_pw1": nrm(ks[6], (n_conv, D, 2 * D), D ** -0.5),
        "conv_b_pw1": nrm(ks[7], (n_conv, 2 * D), 0.01),
        "conv_w_dw": nrm(ks[8], (n_conv, CONV_WIDTH, D), CONV_WIDTH ** -0.5),
        "conv_b_dw": nrm(ks[9], (n_conv, D), 0.01),
        "conv_ln_g": 1.0 + nrm(ks[10], (n_conv, D), 0.05),
        "conv_ln_b": nrm(ks[11], (n_conv, D), 0.01),
        "conv_w_pw2": nrm(ks[12], (n_conv, D, D), D ** -0.5),
        "conv_b_pw2": nrm(ks[13], (n_conv, D), 0.01),
        "peer_w_q": nrm(ks[14], (DEPTH, D, PEER_HEADS * PEER_KEY_DIM), D ** -0.5),
        "peer_sub_keys": nrm(ks[15], (DEPTH, PEER_HEADS, 2, PEER_NKEYS, PEER_HALF), PEER_HALF ** -0.5),
        "peer_u": nrm(ks[16], (DEPTH, PEER_NEXPERTS, D), D ** -0.5),
        "peer_v": nrm(ks[17], (DEPTH, PEER_NEXPERTS, D), (PEER_HEADS * PEER_TOPK) ** -0.5),
        "norm_final": 1.0 + nrm(ks[18], (D,), 0.05),
    }


def reference(x, rel_bias, norm_mix, norm_ffn, attn_w_qkv, attn_w_o,
              conv_w_pw1, conv_b_pw1, conv_w_dw, conv_b_dw, conv_ln_g, conv_ln_b,
              conv_w_pw2, conv_b_pw2, peer_w_q, peer_sub_keys, peer_u, peer_v, norm_final):
    for i in range(DEPTH):
        h = rmsnorm(x, norm_mix[i])
        j = i // N_MIXERS
        if i % N_MIXERS == 0:
            mix = moba_attention(h, attn_w_qkv[j], attn_w_o[j], rel_bias)
        else:
            mix = conformer_conv(h, conv_w_pw1[j], conv_b_pw1[j], conv_w_dw[j], conv_b_dw[j],
                                 conv_ln_g[j], conv_ln_b[j], conv_w_pw2[j], conv_b_pw2[j])
        x = x + mix
        h = rmsnorm(x, norm_ffn[i])
        x = x + peer(h, peer_w_q[i], peer_sub_keys[i], peer_u[i], peer_v[i])
    return rmsnorm(x, norm_final)
```

```python
import functools
import math

import numpy as np
import jax
import jax.numpy as jnp
from jax import lax
from jax.experimental import pallas as pl
from jax.experimental.pallas import tpu as pltpu

F32 = jnp.float32
BF16 = jnp.bfloat16
EPS = 1e-6
NEG = -1e30

N_HEADS = 16
HEAD_DIM = 64
MOBA_BLOCK = 256
MOBA_TOPK = 3
REL_BUCKETS = 32
REL_MAX_DIST = 128
CONV_WIDTH = 31
PEER_HEADS = 8
PEER_NKEYS = 128
PEER_HALF = 128
PEER_TOPK = 16

LANES = 128
SUBLANES = 8
VMEM_LIMIT = 56 * 1024 * 1024

TM_PROJ = 512
TT_ROUTER = 1024
TT_PEER = 512
CE_PEER = 512
TM_CONV = 512
CONV_HALO = 32
CONV_ROWS = 16

NT_DIMS = (((1,), (1,)), ((), ()))


def _cparams(sem):
    return pltpu.CompilerParams(dimension_semantics=sem, vmem_limit_bytes=VMEM_LIMIT)


def _rms(x, g):
    return x * lax.rsqrt(jnp.mean(x * x, axis=-1, keepdims=True) + EPS) * g


def _qkv_kernel(x_ref, g_ref, w_ref, qkv_ref, km_ref, *, d):
    h = _rms(x_ref[...], g_ref[...]).astype(BF16)
    y = jnp.dot(h, w_ref[...], preferred_element_type=F32)
    qkv_ref[...] = y.astype(BF16)
    k = y[:, d:2 * d]
    nblk = k.shape[0] // MOBA_BLOCK
    km_ref[0] = jnp.mean(k.reshape(nblk, MOBA_BLOCK, d), axis=1)


def _qkv_proj(x2d, g, w_bf16):
    t, d = x2d.shape
    tm = TM_PROJ
    nblk = tm // MOBA_BLOCK
    return pl.pallas_call(
        functools.partial(_qkv_kernel, d=d),
        grid=(t // tm,),
        in_specs=[
            pl.BlockSpec((tm, d), lambda i: (i, 0)),
            pl.BlockSpec((1, d), lambda i: (0, 0)),
            pl.BlockSpec((d, 3 * d), lambda i: (0, 0)),
        ],
        out_specs=[
            pl.BlockSpec((tm, 3 * d), lambda i: (i, 0)),
            pl.BlockSpec((1, nblk, d), lambda i: (i, 0, 0)),
        ],
        out_shape=[
            jax.ShapeDtypeStruct((t, 3 * d), BF16),
            jax.ShapeDtypeStruct((t // tm, nblk, d), F32),
        ],
        compiler_params=_cparams(("parallel",)),
        name="qkv_proj",
    )(x2d, g.reshape(1, d), w_bf16)


def _moba_kernel(q_ref, k_ref, v_ref, km_ref, tbl_ref, o_ref, *, nb):
    blk = MOBA_BLOCK
    i = pl.program_id(2)
    q = q_ref[...]
    lane = lax.broadcasted_iota(jnp.int32, (blk, LANES), 1)
    km = km_ref[...]
    km_pad = jnp.concatenate([km, jnp.zeros((LANES - nb, LANES), F32)], axis=0)
    valid = lane < i
    scale = HEAD_DIM ** -0.5
    jp = jnp.maximum(i - 1, 0)

    def block_scores(lhs, j):
        kj = k_ref[pl.ds(pl.multiple_of(j * blk, blk), blk), :]
        onehot = jnp.where(lane == j, 1.0, 0.0).astype(BF16)
        rhs = jnp.concatenate([kj, onehot], axis=1)
        return lax.dot_general(lhs, rhs, NT_DIMS, preferred_element_type=F32)

    def pv(p, j):
        vj = v_ref[pl.ds(pl.multiple_of(j * blk, blk), blk), :]
        return jnp.dot(p.astype(BF16), vj, preferred_element_type=F32)

    outs = []
    for hh in range(2):
        qh = jnp.where(lane // HEAD_DIM == hh, q, jnp.zeros_like(q))
        gate = lax.dot_general(qh.astype(F32), km_pad, NT_DIMS,
                               precision=lax.Precision.HIGHEST,
                               preferred_element_type=F32)
        gate = jnp.where(valid, gate, NEG)
        g = gate
        for _ in range(MOBA_TOPK - 1):
            m = jnp.max(g, axis=-1, keepdims=True)
            g = jnp.where(g >= m, NEG, g)
        tau = jnp.max(g, axis=-1, keepdims=True)
        selbias = jnp.where(valid, jnp.where(gate >= tau, 0.0, NEG), NEG).astype(BF16)
        qs = (qh.astype(F32) * scale).astype(BF16)
        lhs = jnp.concatenate([qs, selbias], axis=1)

        k_own = k_ref[pl.ds(pl.multiple_of(i * blk, blk), blk), :]
        s = lax.dot_general(qs, k_own, NT_DIMS, preferred_element_type=F32) + tbl_ref[hh, 0]
        m0 = jnp.max(s, axis=-1, keepdims=True)
        p = jnp.exp(s - m0)
        l0 = jnp.sum(p, axis=-1, keepdims=True)
        acc0 = pv(p, i)

        def update(carry, s, j):
            m_prev, l_prev, acc_prev = carry
            m_new = jnp.maximum(m_prev, jnp.max(s, axis=-1, keepdims=True))
            alpha = jnp.exp(m_prev - m_new)
            p = jnp.exp(s - m_new)
            l_new = alpha * l_prev + jnp.sum(p, axis=-1, keepdims=True)
            acc_new = alpha * acc_prev + pv(p, j)
            return m_new, l_new, acc_new

        s = block_scores(lhs, jp) + tbl_ref[hh, 1]
        carry = update((m0, l0, acc0), s, jp)

        def body(j, carry):
            return update(carry, block_scores(lhs, j), j)

        m_f, l_f, acc_f = lax.fori_loop(0, jp, body, carry)
        outs.append(acc_f / l_f)

    o_ref[...] = jnp.where(lane < HEAD_DIM, outs[0], outs[1]).astype(o_ref.dtype)


def _t5_bucket_np(dist):
    max_exact = REL_BUCKETS // 2
    d = np.maximum(dist, 0)
    df = np.maximum(d, 1).astype(np.float32)
    large = max_exact + (np.log(df / max_exact) / math.log(REL_MAX_DIST / max_exact)
                         * (REL_BUCKETS - max_exact)).astype(np.int32)
    large = np.minimum(large, REL_BUCKETS - 1)
    return np.where(d < max_exact, d, large)


def _moba_bias_tables(rel_bias):
    blk = MOBA_BLOCK
    qi = np.arange(blk)[:, None]
    ki = np.arange(blk)[None, :]
    assert int(_t5_bucket_np(np.array([blk + 1]))[0]) == REL_BUCKETS - 1
    b_own = _t5_bucket_np(qi - ki)
    b_prev = _t5_bucket_np(qi - ki + blk)
    far = rel_bias[:, REL_BUCKETS - 1][:, None, None]
    own = jnp.where((qi >= ki)[None], rel_bias[:, b_own] - far, NEG)
    prev = rel_bias[:, b_prev] - far
    return jnp.stack([own, prev], axis=1).astype(F32)


def _moba_attention(qkv, kmean, tbl, b, s, d):
    blk = MOBA_BLOCK
    nb = s // blk
    nhp = d // LANES
    qkv3 = qkv.reshape(b, s, 3 * d)
    return pl.pallas_call(
        functools.partial(_moba_kernel, nb=nb),
        grid=(b, nhp, nb),
        in_specs=[
            pl.BlockSpec((None, blk, LANES), lambda bi, hp, i: (bi, i, hp)),
            pl.BlockSpec((None, s, LANES), lambda bi, hp, i: (bi, 0, nhp + hp)),
            pl.BlockSpec((None, s, LANES), lambda bi, hp, i: (bi, 0, 2 * nhp + hp)),
            pl.BlockSpec((None, nb, LANES), lambda bi, hp, i: (bi, 0, hp)),
            pl.BlockSpec((2, 2, blk, blk), lambda bi, hp, i: (hp, 0, 0, 0)),
        ],
        out_specs=pl.BlockSpec((None, blk, LANES), lambda bi, hp, i: (bi, i, hp)),
        out_shape=jax.ShapeDtypeStruct((b, s, d), BF16),
        compiler_params=_cparams(("parallel", "parallel", "arbitrary")),
        name="moba_attention",
    )(qkv3, qkv3, qkv3, kmean.reshape(b, nb, d), tbl)


def _peerq_epilogue(x_new, gf_ref, wpq_ref, x_out_ref, h_ref, pq_ref):
    x_out_ref[...] = x_new
    h = _rms(x_new, gf_ref[...]).astype(BF16)
    h_ref[...] = h
    pq_ref[...] = jnp.dot(h, wpq_ref[...], preferred_element_type=F32)


def _attn_out_kernel(o_ref, wo_ref, x_ref, gf_ref, wpq_ref, x_out_ref, h_ref, pq_ref):
    x_new = x_ref[...] + jnp.dot(o_ref[...], wo_ref[...], preferred_element_type=F32)
    _peerq_epilogue(x_new, gf_ref, wpq_ref, x_out_ref, h_ref, pq_ref)


def _attn_out_proj(o2d, wo_bf16, x2d, gf, wpq_bf16):
    t, d = x2d.shape
    nq = wpq_bf16.shape[1]
    tm = TM_PROJ
    return pl.pallas_call(
        _attn_out_kernel,
        grid=(t // tm,),
        in_specs=[
            pl.BlockSpec((tm, d), lambda i: (i, 0)),
            pl.BlockSpec((d, d), lambda i: (0, 0)),
            pl.BlockSpec((tm, d), lambda i: (i, 0)),
            pl.BlockSpec((1, d), lambda i: (0, 0)),
            pl.BlockSpec((d, nq), lambda i: (0, 0)),
        ],
        out_specs=[
            pl.BlockSpec((tm, d), lambda i: (i, 0)),
            pl.BlockSpec((tm, d), lambda i: (i, 0)),
            pl.BlockSpec((tm, nq), lambda i: (i, 0)),
        ],
        out_shape=[
            jax.ShapeDtypeStruct((t, d), F32),
            jax.ShapeDtypeStruct((t, d), BF16),
            jax.ShapeDtypeStruct((t, nq), F32),
        ],
        compiler_params=_cparams(("parallel",)),
        name="attn_out_proj",
    )(o2d, wo_bf16, x2d, gf.reshape(1, d), wpq_bf16)


def _cmpx(vals, a, b):
    hi = jnp.maximum(vals[a], vals[b])
    lo = jnp.minimum(vals[a], vals[b])
    vals[a], vals[b] = hi, lo


def _bitonic_merge_desc(vals):
    n = len(vals)
    dist = n // 2
    while dist >= 1:
        for s in range(0, n, 2 * dist):
            for t in range(s, s + dist):
                _cmpx(vals, t, t + dist)
        dist //= 2
    return vals


def _sort_desc(vals):
    n = len(vals)
    if n == 1:
        return vals
    top = _sort_desc(vals[: n // 2])
    bot = _sort_desc(vals[n // 2:])
    return _bitonic_merge_desc(top + bot[::-1])


def _top_merge(a, b):
    n = len(a)
    return _bitonic_merge_desc([jnp.maximum(a[r], b[n - 1 - r]) for r in range(n)])


def _sorted_top16(vals):
    groups = [_sort_desc(vals[g:g + PEER_TOPK]) for g in range(0, len(vals), PEER_TOPK)]
    while len(groups) > 1:
        groups = [_top_merge(groups[g], groups[g + 1]) for g in range(0, len(groups), 2)]
    return groups[0]


def _router_kernel(pq_ref, keys_ref, st_ref, sc_ref, scr_ref):
    nch = TT_ROUTER // LANES
    tops = []
    for c in range(2):
        qc = pq_ref[:, c * PEER_HALF:(c + 1) * PEER_HALF]
        s_t = lax.dot_general(keys_ref[0, c], qc, NT_DIMS,
                              precision=lax.Precision.HIGHEST,
                              preferred_element_type=F32)
        st_ref[0, c] = s_t
        for ch in range(nch):
            scr_ref[c, pl.ds(ch, PEER_NKEYS, stride=nch), :] = s_t[:, ch * LANES:(ch + 1) * LANES]
        vals = [scr_ref[c, k * nch:(k + 1) * nch, :] for k in range(PEER_NKEYS)]
        tops.append(_sorted_top16(vals))
    a, b = tops
    cands = [a[r] + b[c] for r in range(PEER_TOPK) for c in range(PEER_TOPK)
             if (r + 1) * (c + 1) <= PEER_TOPK]
    npad = 1 << (len(cands) - 1).bit_length()
    cands = cands + [jnp.full_like(cands[0], NEG)] * (npad - len(cands))
    best = _sort_desc(cands)[:PEER_TOPK]
    z = jnp.ones_like(best[0])
    for r in range(1, PEER_TOPK):
        z = z + jnp.exp(best[r] - best[0])
    sc_ref[0, 0] = best[PEER_TOPK - 1]
    sc_ref[1, 0] = a[0]
    sc_ref[2, 0] = b[0]
    sc_ref[3, 0] = 1.0 / z


def _peer_router(pq, sub_keys):
    t = pq.shape[0]
    tt = TT_ROUTER
    nch = tt // LANES
    hp = PEER_HEADS
    return pl.pallas_call(
        _router_kernel,
        grid=(t // tt, hp),
        in_specs=[
            pl.BlockSpec((tt, 2 * PEER_HALF), lambda i, h: (i, h)),
            pl.BlockSpec((1, 2, PEER_NKEYS, PEER_HALF), lambda i, h: (h, 0, 0, 0)),
        ],
        out_specs=[
            pl.BlockSpec((1, 2, PEER_NKEYS, tt), lambda i, h: (h, 0, 0, i)),
            pl.BlockSpec((4, 1, nch, LANES), lambda i, h: (0, h, i, 0)),
        ],
        out_shape=[
            jax.ShapeDtypeStruct((hp, 2, PEER_NKEYS, t), F32),
            jax.ShapeDtypeStruct((4, hp, t // LANES, LANES), F32),
        ],
        scratch_shapes=[pltpu.VMEM((2, PEER_NKEYS * nch, LANES), F32)],
        compiler_params=_cparams(("parallel", "parallel")),
        name="peer_router",
    )(pq, sub_keys)


def _peer_kernel(h_ref, st_ref, sc_ref, u_ref, vt_ref, x_ref, gfin_ref, o_ref,
                 a_scr, b_scr, acc_scr, *, final_norm):
    e = pl.program_id(1)
    ipc = CE_PEER // PEER_NKEYS

    @pl.when(e == 0)
    def _():
        acc_scr[...] = jnp.zeros_like(acc_scr)
        for hd in range(PEER_HEADS):
            a0 = sc_ref[1, hd:hd + 1, :]
            b0 = sc_ref[2, hd:hd + 1, :]
            zinv = sc_ref[3, hd:hd + 1, :]
            a_scr[hd] = jnp.exp(st_ref[hd, 0] - a0)
            b_scr[hd] = jnp.exp(st_ref[hd, 1] - b0) * zinv

    act = lax.dot_general(u_ref[...], h_ref[...], NT_DIMS, preferred_element_type=F32)
    act = jax.nn.gelu(act)
    wa = []
    for il in range(ipc):
        row = e * ipc + il
        w = None
        for hd in range(PEER_HEADS):
            s1_row = st_ref[hd, 0, pl.ds(row, 1), :]
            a_row = a_scr[hd, pl.ds(row, 1), :]
            chosen = s1_row + st_ref[hd, 1] >= sc_ref[0, hd:hd + 1, :]
            term = jnp.where(chosen, a_row * b_scr[hd], 0.0)
            w = term if w is None else w + term
        wa.append((w * act[il * PEER_NKEYS:(il + 1) * PEER_NKEYS, :]).astype(BF16))
    wa = jnp.concatenate(wa, axis=0)
    acc_scr[...] += jnp.dot(vt_ref[...], wa, preferred_element_type=F32)

    @pl.when(e == pl.num_programs(1) - 1)
    def _():
        y = x_ref[...] + acc_scr[...].T
        if final_norm:
            y = _rms(y, gfin_ref[...])
        o_ref[...] = y


def _peer_experts(h_bf16, st, sc, u_bf16, vt_bf16, x2d, gfin, final_norm):
    t, d = x2d.shape
    ne = u_bf16.shape[0]
    tt, ce = TT_PEER, CE_PEER
    hp = PEER_HEADS
    return pl.pallas_call(
        functools.partial(_peer_kernel, final_norm=final_norm),
        grid=(t // tt, ne // ce),
        in_specs=[
            pl.BlockSpec((tt, d), lambda i, e: (i, 0)),
            pl.BlockSpec((hp, 2, PEER_NKEYS, tt), lambda i, e: (0, 0, 0, i)),
            pl.BlockSpec((4, hp, tt), lambda i, e: (0, 0, i)),
            pl.BlockSpec((ce, d), lambda i, e: (e, 0)),
            pl.BlockSpec((d, ce), lambda i, e: (0, e)),
            pl.BlockSpec((tt, d), lambda i, e: (i, 0)),
            pl.BlockSpec((1, d), lambda i, e: (0, 0)),
        ],
        out_specs=pl.BlockSpec((tt, d), lambda i, e: (i, 0)),
        out_shape=jax.ShapeDtypeStruct((t, d), F32),
        scratch_shapes=[
            pltpu.VMEM((hp, PEER_NKEYS, tt), F32),
            pltpu.VMEM((hp, PEER_NKEYS, tt), F32),
            pltpu.VMEM((d, tt), F32),
        ],
        compiler_params=_cparams(("parallel", "arbitrary")),
        name="peer_experts",
    )(h_bf16, st, sc.reshape(4, hp, t), u_bf16, vt_bf16, x2d, gfin.reshape(1, d))


def _peer(h_bf16, pq, sub_keys, u_bf16, vt_bf16, x2d, gfin, final_norm):
    st, sc = _peer_router(pq, sub_keys)
    return _peer_experts(h_bf16, st, sc, u_bf16, vt_bf16, x2d, gfin, final_norm)


def _glu_kernel(x_ref, g_ref, w_ref, b_ref, u_ref, *, d):
    h = _rms(x_ref[...], g_ref[...]).astype(BF16)
    a = jnp.dot(h, w_ref[...], preferred_element_type=F32) + b_ref[...]
    u_ref[...] = a[:, :d] * jax.nn.sigmoid(a[:, d:])


def _glu_proj(x2d, g, w_bf16, bias):
    t, d = x2d.shape
    tm = TM_PROJ
    return pl.pallas_call(
        functools.partial(_glu_kernel, d=d),
        grid=(t // tm,),
        in_specs=[
            pl.BlockSpec((tm, d), lambda i: (i, 0)),
            pl.BlockSpec((1, d), lambda i: (0, 0)),
            pl.BlockSpec((d, 2 * d), lambda i: (0, 0)),
            pl.BlockSpec((1, 2 * d), lambda i: (0, 0)),
        ],
        out_specs=pl.BlockSpec((tm, d), lambda i: (i, 0)),
        out_shape=jax.ShapeDtypeStruct((t, d), F32),
        compiler_params=_cparams(("parallel",)),
        name="glu_proj",
    )(x2d, g.reshape(1, d), w_bf16, bias.reshape(1, 2 * d))


def _conv_kernel(ucur_ref, uprev_ref, wdw_ref, bdw_ref, lng_ref, lnb_ref, w2_ref, b2_ref,
                 x_ref, gf_ref, wpq_ref, x_out_ref, h_ref, pq_ref, ext_scr, y_scr):
    tm, d = ucur_ref.shape
    first = pl.program_id(1) == 0
    ext_scr[0:CONV_HALO, :] = jnp.where(first, 0.0, uprev_ref[...])
    ext_scr[CONV_HALO:, :] = ucur_ref[...]
    lead = CONV_HALO - (CONV_WIDTH - 1)

    def chunk(ci, carry):
        r0 = pl.multiple_of(ci * CONV_ROWS, CONV_ROWS)
        window = ext_scr[pl.ds(r0, CONV_ROWS + CONV_HALO), :]
        acc = jnp.broadcast_to(bdw_ref[...], (CONV_ROWS, d))
        for w in range(CONV_WIDTH):
            acc = acc + wdw_ref[w:w + 1, :] * window[lead + w:lead + w + CONV_ROWS, :]
        y_scr[pl.ds(r0, CONV_ROWS), :] = acc
        return carry

    lax.fori_loop(0, tm // CONV_ROWS, chunk, 0)
    y = y_scr[...]
    mu = jnp.mean(y, axis=-1, keepdims=True)
    var = jnp.mean(jnp.square(y - mu), axis=-1, keepdims=True)
    y = (y - mu) * lax.rsqrt(var + EPS) * lng_ref[...] + lnb_ref[...]
    y = y * jax.nn.sigmoid(y)
    x_new = x_ref[...] + jnp.dot(y.astype(BF16), w2_ref[...], preferred_element_type=F32) + b2_ref[...]
    _peerq_epilogue(x_new, gf_ref, wpq_ref, x_out_ref, h_ref, pq_ref)


def _conv_tail(u3, wdw, bdw, lng, lnb, w2_bf16, b2, x3, gf, wpq_bf16):
    b, s, d = x3.shape
    nq = wpq_bf16.shape[1]
    tm = TM_CONV
    hpb = tm // CONV_HALO
    row = lambda v: v.reshape(1, d)
    tok = pl.BlockSpec((None, tm, d), lambda bi, i: (bi, i, 0))
    vec = pl.BlockSpec((1, d), lambda bi, i: (0, 0))
    return pl.pallas_call(
        _conv_kernel,
        grid=(b, s // tm),
        in_specs=[
            tok,
            pl.BlockSpec((None, CONV_HALO, d), lambda bi, i: (bi, jnp.maximum(i * hpb - 1, 0), 0)),
            pl.BlockSpec((CONV_WIDTH, d), lambda bi, i: (0, 0)),
            vec, vec, vec,
            pl.BlockSpec((d, d), lambda bi, i: (0, 0)),
            vec,
            tok,
            vec,
            pl.BlockSpec((d, nq), lambda bi, i: (0, 0)),
        ],
        out_specs=[
            tok,
            tok,
            pl.BlockSpec((None, tm, nq), lambda bi, i: (bi, i, 0)),
        ],
        out_shape=[
            jax.ShapeDtypeStruct((b, s, d), F32),
            jax.ShapeDtypeStruct((b, s, d), BF16),
            jax.ShapeDtypeStruct((b, s, nq), F32),
        ],
        scratch_shapes=[
            pltpu.VMEM((tm + CONV_HALO, d), F32),
            pltpu.VMEM((tm, d), F32),
        ],
        compiler_params=_cparams(("parallel", "arbitrary")),
        name="conv_tail",
    )(u3, u3, wdw, row(bdw), row(lng), row(lnb), w2_bf16, row(b2), x3, row(gf), wpq_bf16)


def kernel(x, rel_bias, norm_mix, norm_ffn, attn_w_qkv, attn_w_o, conv_w_pw1, conv_b_pw1,
           conv_w_dw, conv_b_dw, conv_ln_g, conv_ln_b, conv_w_pw2, conv_b_pw2, peer_w_q,
           peer_sub_keys, peer_u, peer_v, norm_final):
    b, s, d = x.shape
    t = b * s
    depth = norm_mix.shape[0]
    assert d == N_HEADS * HEAD_DIM and s % MOBA_BLOCK == 0 and s // MOBA_BLOCK <= LANES
    assert t % TT_ROUTER == 0 and s % TM_CONV == 0 and t % TM_PROJ == 0
    x2d = x.reshape(t, d)
    for i in range(depth):
        j = i // 2
        wpq = peer_w_q[i].astype(BF16)
        if i % 2 == 0:
            qkv, kmean = _qkv_proj(x2d, norm_mix[i], attn_w_qkv[j].astype(BF16))
            o = _moba_attention(qkv, kmean, _moba_bias_tables(rel_bias), b, s, d)
            x2d, h, pq = _attn_out_proj(o.reshape(t, d), attn_w_o[j].astype(BF16), x2d,
                                        norm_ffn[i], wpq)
        else:
            u = _glu_proj(x2d, norm_mix[i], conv_w_pw1[j].astype(BF16), conv_b_pw1[j])
            x3, h3, pq3 = _conv_tail(u.reshape(b, s, d), conv_w_dw[j], conv_b_dw[j], conv_ln_g[j],
                                     conv_ln_b[j], conv_w_pw2[j].astype(BF16), conv_b_pw2[j],
                                     x2d.reshape(b, s, d), norm_ffn[i], wpq)
            x2d, h, pq = x3.reshape(t, d), h3.reshape(t, d), pq3.reshape(t, -1)
        last = i == depth - 1
        x2d = _peer(h, pq, peer_sub_keys[i], peer_u[i].astype(BF16),
                    peer_v[i].T.astype(BF16), x2d, norm_final, final_norm=last)
    return x2d.reshape(b, s, d)
```

```python
import functools
import math

import numpy as np
import jax
import jax.numpy as jnp
from jax import lax
from jax.experimental import pallas as pl
from jax.experimental.pallas import tpu as pltpu

F32 = jnp.float32
BF16 = jnp.bfloat16
EPS = 1e-6
NEG = -1e30

N_HEADS = 16
HEAD_DIM = 64
MOBA_BLOCK = 256
MOBA_TOPK = 3
REL_BUCKETS = 32
REL_MAX_DIST = 128
CONV_WIDTH = 31
PEER_HEADS = 8
PEER_NKEYS = 128
PEER_HALF = 128
PEER_TOPK = 16

LANES = 128
SUBLANES = 8
VMEM_LIMIT = 56 * 1024 * 1024

MOBA_QLANES = 128
TM_PROJ = 512
TT_ROUTER = 1024
TT_PEER = 512
CE_PEER = 512
TM_CONV = 512
CONV_HALO = 32
CONV_ROWS = 16

NT_DIMS = (((1,), (1,)), ((), ()))


def _cparams(sem):
    return pltpu.CompilerParams(dimension_semantics=sem, vmem_limit_bytes=VMEM_LIMIT)


def _rms(x, g):
    return x * lax.rsqrt(jnp.mean(x * x, axis=-1, keepdims=True) + EPS) * g


def _qkv_kernel(x_ref, g_ref, wqt_ref, wk_ref, wvt_ref, qt_ref, k_ref, vt_ref, km_ref):
    h = _rms(x_ref[...], g_ref[...]).astype(BF16)
    scale = HEAD_DIM ** -0.5
    qt = lax.dot_general(wqt_ref[...], h, NT_DIMS, preferred_element_type=F32)
    qt_ref[...] = (qt * scale).astype(BF16)
    k = jnp.dot(h, wk_ref[...], preferred_element_type=F32)
    k_ref[...] = k.astype(BF16)
    vt = lax.dot_general(wvt_ref[...], h, NT_DIMS, preferred_element_type=F32)
    vt_ref[...] = vt.astype(BF16)
    nblk = k.shape[0] // MOBA_BLOCK
    km_ref[...] = jnp.mean(k.reshape(nblk, MOBA_BLOCK, k.shape[1]), axis=1)


def _qkv_proj(x3, g, wqt_bf16, wk_bf16, wvt_bf16):
    b, s, d = x3.shape
    tm = TM_PROJ
    nblk = tm // MOBA_BLOCK
    nb = s // MOBA_BLOCK
    wspec = pl.BlockSpec((d, d), lambda bi, i: (0, 0))
    tspec = pl.BlockSpec((None, d, tm), lambda bi, i: (bi, 0, i))
    return pl.pallas_call(
        _qkv_kernel,
        grid=(b, s // tm),
        in_specs=[
            pl.BlockSpec((None, tm, d), lambda bi, i: (bi, i, 0)),
            pl.BlockSpec((1, d), lambda bi, i: (0, 0)),
            wspec, wspec, wspec,
        ],
        out_specs=[
            tspec,
            pl.BlockSpec((None, tm, d), lambda bi, i: (bi, i, 0)),
            tspec,
            pl.BlockSpec((None, None, nblk, d), lambda bi, i: (bi, i, 0, 0)),
        ],
        out_shape=[
            jax.ShapeDtypeStruct((b, d, s), BF16),
            jax.ShapeDtypeStruct((b, s, d), BF16),
            jax.ShapeDtypeStruct((b, d, s), BF16),
            jax.ShapeDtypeStruct((b, s // tm, nblk, d), F32),
        ],
        compiler_params=_cparams(("parallel", "parallel")),
        name="qkv_proj",
    )(x3, g.reshape(1, d), wqt_bf16, wk_bf16, wvt_bf16)


def _moba_kernel(qt_ref, k_ref, vt_ref, km_ref, tbl_ref, o_ref, sel_scr, *, nb):
    blk = MOBA_BLOCK
    i = pl.program_id(2)
    jp = jnp.maximum(i - 1, 0)
    qt = qt_ref[...]
    frow = lax.broadcasted_iota(jnp.int32, qt.shape, 0)
    blk_id = lax.broadcasted_iota(jnp.int32, (nb, blk), 0)
    valid = blk_id < i
    km = km_ref[...]

    qh = []
    for hh in range(2):
        qh_t = jnp.where(frow // HEAD_DIM == hh, qt, jnp.zeros_like(qt))
        qh.append(qh_t)
        gate = jnp.dot(km, qh_t.astype(F32), precision=lax.Precision.HIGHEST,
                       preferred_element_type=F32)
        gate = jnp.where(valid, gate, NEG)
        g = gate
        for _ in range(MOBA_TOPK - 1):
            g = jnp.where(g >= jnp.max(g, axis=0, keepdims=True), NEG, g)
        tau = jnp.max(g, axis=0, keepdims=True)
        sel = jnp.where(valid, jnp.where(gate >= tau, 0.0, NEG), NEG)
        for qs in range(blk // MOBA_QLANES):
            sel_scr[hh, qs] = sel[:, qs * MOBA_QLANES:(qs + 1) * MOBA_QLANES]

    def k_block(j):
        return k_ref[pl.ds(pl.multiple_of(j * blk, blk), blk), :]

    chains = [(hh, qs) for hh in range(2) for qs in range(blk // MOBA_QLANES)]

    def qcols(qs):
        return slice(qs * MOBA_QLANES, (qs + 1) * MOBA_QLANES)

    def scores(kj, hh, qs):
        return jnp.dot(kj, qh[hh][:, qcols(qs)], preferred_element_type=F32)

    def sel_row(hh, qs, j):
        return sel_scr[hh, qs, pl.ds(j, 1), :]

    def pv(hh, p, j):
        vj = vt_ref[hh * HEAD_DIM:(hh + 1) * HEAD_DIM, pl.ds(pl.multiple_of(j * blk, blk), blk)]
        return jnp.dot(vj, p.astype(BF16), preferred_element_type=F32)

    def softmax_step(m_prev, l_prev, s):
        m_new = jnp.maximum(m_prev, jnp.max(s, axis=0, keepdims=True))
        alpha = jnp.exp(m_prev - m_new)
        p = jnp.exp(s - m_new)
        l_new = alpha * l_prev + jnp.sum(p, axis=0, keepdims=True)
        return m_new, l_new, alpha, p.astype(BF16)

    def block_scores(j):
        kj = k_block(j)
        return tuple(scores(kj, hh, qs) + sel_row(hh, qs, j) for hh, qs in chains)

    k_own, k_prev = k_block(i), k_block(jp)
    ss = [scores(k_own, hh, qs) + tbl_ref[hh, 0, :, qcols(qs)] for hh, qs in chains]
    ss_prev = [scores(k_prev, hh, qs) + tbl_ref[hh, 1, :, qcols(qs)] + sel_row(hh, qs, jp)
               for hh, qs in chains]
    ms = [jnp.max(s, axis=0, keepdims=True) for s in ss]
    ps = [jnp.exp(s - m) for s, m in zip(ss, ms)]
    ls = [jnp.sum(p, axis=0, keepdims=True) for p in ps]
    accs = [pv(hh, p, i) for (hh, _), p in zip(chains, ps)]
    steps = [softmax_step(m, l, s) for m, l, s in zip(ms, ls, ss_prev)]
    accs = [alpha * acc + pv(hh, p, jp) for (hh, _), acc, (_, _, alpha, p) in zip(chains, accs, steps)]
    ms = [st[0] for st in steps]
    ls = [st[1] for st in steps]

    def body(j, state):
        ss_cur, ms, ls, accs, ps_prev = state
        ss_next = block_scores(j + 1)
        pvs = [pv(hh, p, jnp.maximum(j - 1, 0)) for (hh, _), p in zip(chains, ps_prev)]
        steps = [softmax_step(m, l, s) for m, l, s in zip(ms, ls, ss_cur)]
        accs = [alpha * (acc + r) for acc, r, (_, _, alpha, _) in zip(accs, pvs, steps)]
        return (ss_next, tuple(st[0] for st in steps), tuple(st[1] for st in steps), tuple(accs),
                tuple(st[3] for st in steps))

    zero_p = tuple(jnp.zeros((blk, MOBA_QLANES), BF16) for _ in chains)
    state = (block_scores(0), tuple(ms), tuple(ls), tuple(accs), zero_p)
    _, ms, ls, accs, ps_last = lax.fori_loop(0, jp, body, state)
    accs = [acc + pv(hh, p, jnp.maximum(jp - 1, 0)) for (hh, _), acc, p in zip(chains, accs, ps_last)]
    outs = [acc / l for l, acc in zip(ls, accs)]
    nqs = blk // MOBA_QLANES
    o_t = jnp.concatenate([jnp.concatenate(outs[hh * nqs:(hh + 1) * nqs], axis=1) for hh in range(2)],
                          axis=0)
    o_ref[...] = o_t.T.astype(o_ref.dtype)


def _t5_bucket_np(dist):
    max_exact = REL_BUCKETS // 2
    d = np.maximum(dist, 0)
    df = np.maximum(d, 1).astype(np.float32)
    large = max_exact + (np.log(df / max_exact) / math.log(REL_MAX_DIST / max_exact)
                         * (REL_BUCKETS - max_exact)).astype(np.int32)
    large = np.minimum(large, REL_BUCKETS - 1)
    return np.where(d < max_exact, d, large)


def _moba_bias_tables(rel_bias):
    blk = MOBA_BLOCK
    qi = np.arange(blk)[None, :]
    ki = np.arange(blk)[:, None]
    assert int(_t5_bucket_np(np.array([blk + 1]))[0]) == REL_BUCKETS - 1
    b_own = _t5_bucket_np(qi - ki)
    b_prev = _t5_bucket_np(qi - ki + blk)
    far = rel_bias[:, REL_BUCKETS - 1][:, None, None]
    own = jnp.where((qi >= ki)[None], rel_bias[:, b_own] - far, NEG)
    prev = rel_bias[:, b_prev] - far
    return jnp.stack([own, prev], axis=1).astype(F32)


def _moba_attention(qt, k, vt, kmean, tbl):
    b, s, d = k.shape
    blk = MOBA_BLOCK
    nb = s // blk
    nhp = d // LANES
    return pl.pallas_call(
        functools.partial(_moba_kernel, nb=nb),
        grid=(b, nhp, nb),
        in_specs=[
            pl.BlockSpec((None, LANES, blk), lambda bi, hp, i: (bi, hp, i)),
            pl.BlockSpec((None, s, LANES), lambda bi, hp, i: (bi, 0, hp)),
            pl.BlockSpec((None, LANES, s), lambda bi, hp, i: (bi, hp, 0)),
            pl.BlockSpec((None, nb, LANES), lambda bi, hp, i: (bi, 0, hp)),
            pl.BlockSpec((2, 2, blk, blk), lambda bi, hp, i: (hp, 0, 0, 0)),
        ],
        out_specs=pl.BlockSpec((None, blk, LANES), lambda bi, hp, i: (bi, i, hp)),
        out_shape=jax.ShapeDtypeStruct((b, s, d), BF16),
        scratch_shapes=[pltpu.VMEM((2, blk // MOBA_QLANES, nb, MOBA_QLANES), F32)],
        compiler_params=_cparams(("parallel", "parallel", "arbitrary")),
        name="moba_attention",
    )(qt, k, vt, kmean.reshape(b, nb, d), tbl)


def _peerq_epilogue(x_new, gf_ref, wpq_ref, x_out_ref, h_ref, pq_ref):
    x_out_ref[...] = x_new
    h = _rms(x_new, gf_ref[...]).astype(BF16)
    h_ref[...] = h
    pq_ref[...] = jnp.dot(h, wpq_ref[...], preferred_element_type=F32)


def _attn_out_kernel(o_ref, wo_ref, x_ref, gf_ref, wpq_ref, x_out_ref, h_ref, pq_ref):
    x_new = x_ref[...] + jnp.dot(o_ref[...], wo_ref[...], preferred_element_type=F32)
    _peerq_epilogue(x_new, gf_ref, wpq_ref, x_out_ref, h_ref, pq_ref)


def _attn_out_proj(o2d, wo_bf16, x2d, gf, wpq_bf16):
    t, d = x2d.shape
    nq = wpq_bf16.shape[1]
    tm = TM_PROJ
    return pl.pallas_call(
        _attn_out_kernel,
        grid=(t // tm,),
        in_specs=[
            pl.BlockSpec((tm, d), lambda i: (i, 0)),
            pl.BlockSpec((d, d), lambda i: (0, 0)),
            pl.BlockSpec((tm, d), lambda i: (i, 0)),
            pl.BlockSpec((1, d), lambda i: (0, 0)),
            pl.BlockSpec((d, nq), lambda i: (0, 0)),
        ],
        out_specs=[
            pl.BlockSpec((tm, d), lambda i: (i, 0)),
            pl.BlockSpec((tm, d), lambda i: (i, 0)),
            pl.BlockSpec((tm, nq), lambda i: (i, 0)),
        ],
        out_shape=[
            jax.ShapeDtypeStruct((t, d), F32),
            jax.ShapeDtypeStruct((t, d), BF16),
            jax.ShapeDtypeStruct((t, nq), F32),
        ],
        compiler_params=_cparams(("parallel",)),
        name="attn_out_proj",
    )(o2d, wo_bf16, x2d, gf.reshape(1, d), wpq_bf16)


def _cmpx(vals, a, b):
    hi = jnp.maximum(vals[a], vals[b])
    lo = jnp.minimum(vals[a], vals[b])
    vals[a], vals[b] = hi, lo


def _bitonic_merge_desc(vals):
    n = len(vals)
    dist = n // 2
    while dist >= 1:
        for s in range(0, n, 2 * dist):
            for t in range(s, s + dist):
                _cmpx(vals, t, t + dist)
        dist //= 2
    return vals


def _sort_desc(vals):
    n = len(vals)
    if n == 1:
        return vals
    top = _sort_desc(vals[: n // 2])
    bot = _sort_desc(vals[n // 2:])
    return _bitonic_merge_desc(top + bot[::-1])


def _top_merge(a, b):
    n = len(a)
    return _bitonic_merge_desc([jnp.maximum(a[r], b[n - 1 - r]) for r in range(n)])


def _sorted_top16(vals):
    groups = [_sort_desc(vals[g:g + PEER_TOPK]) for g in range(0, len(vals), PEER_TOPK)]
    while len(groups) > 1:
        groups = [_top_merge(groups[g], groups[g + 1]) for g in range(0, len(groups), 2)]
    return groups[0]


def _router_kernel(pq_ref, keys_ref, st_ref, sc_ref, scr_ref):
    nch = TT_ROUTER // LANES
    tops = []
    for c in range(2):
        qc = pq_ref[:, c * PEER_HALF:(c + 1) * PEER_HALF]
        s_t = lax.dot_general(keys_ref[0, c], qc, NT_DIMS,
                              precision=lax.Precision.HIGHEST,
                              preferred_element_type=F32)
        st_ref[0, c] = s_t
        for ch in range(nch):
            scr_ref[c, pl.ds(ch, PEER_NKEYS, stride=nch), :] = s_t[:, ch * LANES:(ch + 1) * LANES]
        vals = [scr_ref[c, k * nch:(k + 1) * nch, :] for k in range(PEER_NKEYS)]
        tops.append(_sorted_top16(vals))
    a, b = tops
    cands = [a[r] + b[c] for r in range(PEER_TOPK) for c in range(PEER_TOPK)
             if (r + 1) * (c + 1) <= PEER_TOPK]
    npad = 1 << (len(cands) - 1).bit_length()
    cands = cands + [jnp.full_like(cands[0], NEG)] * (npad - len(cands))
    best = _sort_desc(cands)[:PEER_TOPK]
    z = jnp.ones_like(best[0])
    for r in range(1, PEER_TOPK):
        z = z + jnp.exp(best[r] - best[0])
    sc_ref[0, 0] = best[PEER_TOPK - 1]
    sc_ref[1, 0] = a[0]
    sc_ref[2, 0] = b[0]
    sc_ref[3, 0] = 1.0 / z


def _peer_router(pq, sub_keys):
    t = pq.shape[0]
    tt = TT_ROUTER
    nch = tt // LANES
    hp = PEER_HEADS
    return pl.pallas_call(
        _router_kernel,
        grid=(t // tt, hp),
        in_specs=[
            pl.BlockSpec((tt, 2 * PEER_HALF), lambda i, h: (i, h)),
            pl.BlockSpec((1, 2, PEER_NKEYS, PEER_HALF), lambda i, h: (h, 0, 0, 0)),
        ],
        out_specs=[
            pl.BlockSpec((1, 2, PEER_NKEYS, tt), lambda i, h: (h, 0, 0, i)),
            pl.BlockSpec((4, 1, nch, LANES), lambda i, h: (0, h, i, 0)),
        ],
        out_shape=[
            jax.ShapeDtypeStruct((hp, 2, PEER_NKEYS, t), F32),
            jax.ShapeDtypeStruct((4, hp, t // LANES, LANES), F32),
        ],
        scratch_shapes=[pltpu.VMEM((2, PEER_NKEYS * nch, LANES), F32)],
        compiler_params=_cparams(("parallel", "parallel")),
        name="peer_router",
    )(pq, sub_keys)


def _peer_kernel(h_ref, st_ref, sc_ref, u_ref, vt_ref, x_ref, gfin_ref, o_ref,
                 a_scr, b_scr, acc_scr, *, final_norm):
    e = pl.program_id(1)
    ipc = CE_PEER // PEER_NKEYS

    @pl.when(e == 0)
    def _():
        acc_scr[...] = jnp.zeros_like(acc_scr)
        for hd in range(PEER_HEADS):
            a0 = sc_ref[1, hd:hd + 1, :]
            b0 = sc_ref[2, hd:hd + 1, :]
            zinv = sc_ref[3, hd:hd + 1, :]
            a_scr[hd] = jnp.exp(st_ref[hd, 0] - a0)
            b_scr[hd] = jnp.exp(st_ref[hd, 1] - b0) * zinv

    act = lax.dot_general(u_ref[...], h_ref[...], NT_DIMS, preferred_element_type=F32)
    act = jax.nn.gelu(act)
    wa = []
    for il in range(ipc):
        row = e * ipc + il
        w = None
        for hd in range(PEER_HEADS):
            s1_row = st_ref[hd, 0, pl.ds(row, 1), :]
            a_row = a_scr[hd, pl.ds(row, 1), :]
            chosen = s1_row + st_ref[hd, 1] >= sc_ref[0, hd:hd + 1, :]
            term = jnp.where(chosen, a_row * b_scr[hd], 0.0)
            w = term if w is None else w + term
        wa.append((w * act[il * PEER_NKEYS:(il + 1) * PEER_NKEYS, :]).astype(BF16))
    wa = jnp.concatenate(wa, axis=0)
    acc_scr[...] += jnp.dot(vt_ref[...], wa, preferred_element_type=F32)

    @pl.when(e == pl.num_programs(1) - 1)
    def _():
        y = x_ref[...] + acc_scr[...].T
        if final_norm:
            y = _rms(y, gfin_ref[...])
        o_ref[...] = y


def _peer_experts(h_bf16, st, sc, u_bf16, vt_bf16, x2d, gfin, final_norm):
    t, d = x2d.shape
    ne = u_bf16.shape[0]
    tt, ce = TT_PEER, CE_PEER
    hp = PEER_HEADS
    return pl.pallas_call(
        functools.partial(_peer_kernel, final_norm=final_norm),
        grid=(t // tt, ne // ce),
        in_specs=[
            pl.BlockSpec((tt, d), lambda i, e: (i, 0)),
            pl.BlockSpec((hp, 2, PEER_NKEYS, tt), lambda i, e: (0, 0, 0, i)),
            pl.BlockSpec((4, hp, tt), lambda i, e: (0, 0, i)),
            pl.BlockSpec((ce, d), lambda i, e: (e, 0)),
            pl.BlockSpec((d, ce), lambda i, e: (0, e)),
            pl.BlockSpec((tt, d), lambda i, e: (i, 0)),
            pl.BlockSpec((1, d), lambda i, e: (0, 0)),
        ],
        out_specs=pl.BlockSpec((tt, d), lambda i, e: (i, 0)),
        out_shape=jax.ShapeDtypeStruct((t, d), F32),
        scratch_shapes=[
            pltpu.VMEM((hp, PEER_NKEYS, tt), F32),
            pltpu.VMEM((hp, PEER_NKEYS, tt), F32),
            pltpu.VMEM((d, tt), F32),
        ],
        compiler_params=_cparams(("parallel", "arbitrary")),
        name="peer_experts",
    )(h_bf16, st, sc.reshape(4, hp, t), u_bf16, vt_bf16, x2d, gfin.reshape(1, d))


def _peer(h_bf16, pq, sub_keys, u_bf16, vt_bf16, x2d, gfin, final_norm):
    st, sc = _peer_router(pq, sub_keys)
    return _peer_experts(h_bf16, st, sc, u_bf16, vt_bf16, x2d, gfin, final_norm)


def _glu_kernel(x_ref, g_ref, w_ref, b_ref, u_ref, *, d):
    h = _rms(x_ref[...], g_ref[...]).astype(BF16)
    a = jnp.dot(h, w_ref[...], preferred_element_type=F32) + b_ref[...]
    u_ref[...] = a[:, :d] * jax.nn.sigmoid(a[:, d:])


def _glu_proj(x2d, g, w_bf16, bias):
    t, d = x2d.shape
    tm = TM_PROJ
    return pl.pallas_call(
        functools.partial(_glu_kernel, d=d),
        grid=(t // tm,),
        in_specs=[
            pl.BlockSpec((tm, d), lambda i: (i, 0)),
            pl.BlockSpec((1, d), lambda i: (0, 0)),
            pl.BlockSpec((d, 2 * d), lambda i: (0, 0)),
            pl.BlockSpec((1, 2 * d), lambda i: (0, 0)),
        ],
        out_specs=pl.BlockSpec((tm, d), lambda i: (i, 0)),
        out_shape=jax.ShapeDtypeStruct((t, d), F32),
        compiler_params=_cparams(("parallel",)),
        name="glu_proj",
    )(x2d, g.reshape(1, d), w_bf16, bias.reshape(1, 2 * d))


def _conv_kernel(ucur_ref, uprev_ref, wdw_ref, bdw_ref, lng_ref, lnb_ref, w2_ref, b2_ref,
                 x_ref, gf_ref, wpq_ref, x_out_ref, h_ref, pq_ref, ext_scr, y_scr):
    tm, d = ucur_ref.shape
    first = pl.program_id(1) == 0
    ext_scr[0:CONV_HALO, :] = jnp.where(first, 0.0, uprev_ref[...])
    ext_scr[CONV_HALO:, :] = ucur_ref[...]
    lead = CONV_HALO - (CONV_WIDTH - 1)

    def chunk(ci, carry):
        r0 = pl.multiple_of(ci * CONV_ROWS, CONV_ROWS)
        window = ext_scr[pl.ds(r0, CONV_ROWS + CONV_HALO), :]
        acc = jnp.broadcast_to(bdw_ref[...], (CONV_ROWS, d))
        for w in range(CONV_WIDTH):
            acc = acc + wdw_ref[w:w + 1, :] * window[lead + w:lead + w + CONV_ROWS, :]
        y_scr[pl.ds(r0, CONV_ROWS), :] = acc
        return carry

    lax.fori_loop(0, tm // CONV_ROWS, chunk, 0)
    y = y_scr[...]
    mu = jnp.mean(y, axis=-1, keepdims=True)
    var = jnp.mean(jnp.square(y - mu), axis=-1, keepdims=True)
    y = (y - mu) * lax.rsqrt(var + EPS) * lng_ref[...] + lnb_ref[...]
    y = y * jax.nn.sigmoid(y)
    x_new = x_ref[...] + jnp.dot(y.astype(BF16), w2_ref[...], preferred_element_type=F32) + b2_ref[...]
    _peerq_epilogue(x_new, gf_ref, wpq_ref, x_out_ref, h_ref, pq_ref)


def _conv_tail(u3, wdw, bdw, lng, lnb, w2_bf16, b2, x3, gf, wpq_bf16):
    b, s, d = x3.shape
    nq = wpq_bf16.shape[1]
    tm = TM_CONV
    hpb = tm // CONV_HALO
    row = lambda v: v.reshape(1, d)
    tok = pl.BlockSpec((None, tm, d), lambda bi, i: (bi, i, 0))
    vec = pl.BlockSpec((1, d), lambda bi, i: (0, 0))
    return pl.pallas_call(
        _conv_kernel,
        grid=(b, s // tm),
        in_specs=[
            tok,
            pl.BlockSpec((None, CONV_HALO, d), lambda bi, i: (bi, jnp.maximum(i * hpb - 1, 0), 0)),
            pl.BlockSpec((CONV_WIDTH, d), lambda bi, i: (0, 0)),
            vec, vec, vec,
            pl.BlockSpec((d, d), lambda bi, i: (0, 0)),
            vec,
            tok,
            vec,
            pl.BlockSpec((d, nq), lambda bi, i: (0, 0)),
        ],
        out_specs=[
            tok,
            tok,
            pl.BlockSpec((None, tm, nq), lambda bi, i: (bi, i, 0)),
        ],
        out_shape=[
            jax.ShapeDtypeStruct((b, s, d), F32),
            jax.ShapeDtypeStruct((b, s, d), BF16),
            jax.ShapeDtypeStruct((b, s, nq), F32),
        ],
        scratch_shapes=[
            pltpu.VMEM((tm + CONV_HALO, d), F32),
            pltpu.VMEM((tm, d), F32),
        ],
        compiler_params=_cparams(("parallel", "arbitrary")),
        name="conv_tail",
    )(u3, u3, wdw, row(bdw), row(lng), row(lnb), w2_bf16, row(b2), x3, row(gf), wpq_bf16)


def kernel(x, rel_bias, norm_mix, norm_ffn, attn_w_qkv, attn_w_o, conv_w_pw1, conv_b_pw1,
           conv_w_dw, conv_b_dw, conv_ln_g, conv_ln_b, conv_w_pw2, conv_b_pw2, peer_w_q,
           peer_sub_keys, peer_u, peer_v, norm_final):
    b, s, d = x.shape
    t = b * s
    depth = norm_mix.shape[0]
    assert d == N_HEADS * HEAD_DIM and s % MOBA_BLOCK == 0 and s // MOBA_BLOCK <= LANES
    assert t % TT_ROUTER == 0 and s % TM_CONV == 0 and t % TM_PROJ == 0
    x2d = x.reshape(t, d)
    for i in range(depth):
        j = i // 2
        wpq = peer_w_q[i].astype(BF16)
        if i % 2 == 0:
            wq, wk, wv = (attn_w_qkv[j][:, c * d:(c + 1) * d] for c in range(3))
            qt, k, vt, kmean = _qkv_proj(x2d.reshape(b, s, d), norm_mix[i], wq.T.astype(BF16),
                                         wk.astype(BF16), wv.T.astype(BF16))
            o = _moba_attention(qt, k, vt, kmean, _moba_bias_tables(rel_bias))
            x2d, h, pq = _attn_out_proj(o.reshape(t, d), attn_w_o[j].astype(BF16), x2d,
                                        norm_ffn[i], wpq)
        else:
            u = _glu_proj(x2d, norm_mix[i], conv_w_pw1[j].astype(BF16), conv_b_pw1[j])
            x3, h3, pq3 = _conv_tail(u.reshape(b, s, d), conv_w_dw[j], conv_b_dw[j], conv_ln_g[j],
                                     conv_ln_b[j], conv_w_pw2[j].astype(BF16), conv_b_pw2[j],
                                     x2d.reshape(b, s, d), norm_ffn[i], wpq)
            x2d, h, pq = x3.reshape(t, d), h3.reshape(t, d), pq3.reshape(t, -1)
        last = i == depth - 1
        x2d = _peer(h, pq, peer_sub_keys[i], peer_u[i].astype(BF16),
                    peer_v[i].T.astype(BF16), x2d, norm_final, final_norm=last)
    return x2d.reshape(b, s, d)
```

```python
import functools
import math

import numpy as np
import jax
import jax.numpy as jnp
from jax import lax
from jax.experimental import pallas as pl
from jax.experimental.pallas import tpu as pltpu

F32 = jnp.float32
BF16 = jnp.bfloat16
EPS = 1e-6
NEG = -1e30

N_HEADS = 16
HEAD_DIM = 64
MOBA_BLOCK = 256
MOBA_TOPK = 3
REL_BUCKETS = 32
REL_MAX_DIST = 128
CONV_WIDTH = 31
PEER_HEADS = 8
PEER_NKEYS = 128
PEER_HALF = 128
PEER_TOPK = 16
SC_TOP1, SC_TOP2, SC_TAU, SC_ZINV, SC_ROWS = 0, PEER_TOPK, 2 * PEER_TOPK, 2 * PEER_TOPK + 1, 2 * PEER_TOPK + 2

LANES = 128
SUBLANES = 8
VMEM_LIMIT = 56 * 1024 * 1024

MOBA_QLANES = 128
TM_PROJ = 512
TT_ROUTER = 1024
TT_PEER = 512
CE_STEP = 2048
CE_SUB = 512
TM_CONV = 512
CONV_HALO = 32
CONV_ROWS = 16

NT_DIMS = (((1,), (1,)), ((), ()))


def _cparams(sem):
    return pltpu.CompilerParams(dimension_semantics=sem, vmem_limit_bytes=VMEM_LIMIT)


def _rms(x, g):
    return x * lax.rsqrt(jnp.mean(x * x, axis=-1, keepdims=True) + EPS) * g


def _qkv_kernel(x_ref, g_ref, wqt_ref, wk_ref, wvt_ref, qt_ref, k_ref, vt_ref, km_ref):
    h = _rms(x_ref[...], g_ref[...]).astype(BF16)
    scale = HEAD_DIM ** -0.5
    qt = lax.dot_general(wqt_ref[...], h, NT_DIMS, preferred_element_type=F32)
    qt_ref[...] = (qt * scale).astype(BF16)
    k = jnp.dot(h, wk_ref[...], preferred_element_type=F32)
    k_ref[...] = k.astype(BF16)
    vt = lax.dot_general(wvt_ref[...], h, NT_DIMS, preferred_element_type=F32)
    vt_ref[...] = vt.astype(BF16)
    nblk = k.shape[0] // MOBA_BLOCK
    km_ref[...] = jnp.mean(k.reshape(nblk, MOBA_BLOCK, k.shape[1]), axis=1)


def _qkv_proj(x3, g, wqt_bf16, wk_bf16, wvt_bf16):
    b, s, d = x3.shape
    tm = TM_PROJ
    nblk = tm // MOBA_BLOCK
    nb = s // MOBA_BLOCK
    wspec = pl.BlockSpec((d, d), lambda bi, i: (0, 0))
    tspec = pl.BlockSpec((None, d, tm), lambda bi, i: (bi, 0, i))
    return pl.pallas_call(
        _qkv_kernel,
        grid=(b, s // tm),
        in_specs=[
            pl.BlockSpec((None, tm, d), lambda bi, i: (bi, i, 0)),
            pl.BlockSpec((1, d), lambda bi, i: (0, 0)),
            wspec, wspec, wspec,
        ],
        out_specs=[
            tspec,
            pl.BlockSpec((None, tm, d), lambda bi, i: (bi, i, 0)),
            tspec,
            pl.BlockSpec((None, None, nblk, d), lambda bi, i: (bi, i, 0, 0)),
        ],
        out_shape=[
            jax.ShapeDtypeStruct((b, d, s), BF16),
            jax.ShapeDtypeStruct((b, s, d), BF16),
            jax.ShapeDtypeStruct((b, d, s), BF16),
            jax.ShapeDtypeStruct((b, s // tm, nblk, d), F32),
        ],
        compiler_params=_cparams(("parallel", "parallel")),
        name="qkv_proj",
    )(x3, g.reshape(1, d), wqt_bf16, wk_bf16, wvt_bf16)


def _moba_kernel(qt_ref, k_ref, vt_ref, km_ref, tbl_ref, o_ref, sel_scr, *, nb):
    blk = MOBA_BLOCK
    i = pl.program_id(2)
    jp = jnp.maximum(i - 1, 0)
    qt = qt_ref[...]
    frow = lax.broadcasted_iota(jnp.int32, qt.shape, 0)
    blk_id = lax.broadcasted_iota(jnp.int32, (nb, blk), 0)
    valid = blk_id < i
    km = km_ref[...]

    qh = []
    for hh in range(2):
        qh_t = jnp.where(frow // HEAD_DIM == hh, qt, jnp.zeros_like(qt))
        qh.append(qh_t)
        gate = jnp.dot(km, qh_t.astype(F32), precision=lax.Precision.HIGHEST,
                       preferred_element_type=F32)
        gate = jnp.where(valid, gate, NEG)
        g = gate
        for _ in range(MOBA_TOPK - 1):
            g = jnp.where(g >= jnp.max(g, axis=0, keepdims=True), NEG, g)
        tau = jnp.max(g, axis=0, keepdims=True)
        sel = jnp.where(valid, jnp.where(gate >= tau, 0.0, NEG), NEG)
        for qs in range(blk // MOBA_QLANES):
            sel_scr[hh, qs] = sel[:, qs * MOBA_QLANES:(qs + 1) * MOBA_QLANES]

    def k_block(j):
        return k_ref[pl.ds(pl.multiple_of(j * blk, blk), blk), :]

    chains = [(hh, qs) for hh in range(2) for qs in range(blk // MOBA_QLANES)]

    def qcols(qs):
        return slice(qs * MOBA_QLANES, (qs + 1) * MOBA_QLANES)

    def scores(kj, hh, qs):
        return jnp.dot(kj, qh[hh][:, qcols(qs)], preferred_element_type=F32)

    def sel_row(hh, qs, j):
        return sel_scr[hh, qs, pl.ds(j, 1), :]

    def pv(hh, p, j):
        vj = vt_ref[hh * HEAD_DIM:(hh + 1) * HEAD_DIM, pl.ds(pl.multiple_of(j * blk, blk), blk)]
        return jnp.dot(vj, p.astype(BF16), preferred_element_type=F32)

    def softmax_step(m_prev, l_prev, s):
        m_new = jnp.maximum(m_prev, jnp.max(s, axis=0, keepdims=True))
        alpha = jnp.exp(m_prev - m_new)
        p = jnp.exp(s - m_new)
        l_new = alpha * l_prev + jnp.sum(p, axis=0, keepdims=True)
        return m_new, l_new, alpha, p.astype(BF16)

    def block_scores(j):
        kj = k_block(j)
        return tuple(scores(kj, hh, qs) + sel_row(hh, qs, j) for hh, qs in chains)

    k_own, k_prev = k_block(i), k_block(jp)
    ss = [scores(k_own, hh, qs) + tbl_ref[hh, 0, :, qcols(qs)] for hh, qs in chains]
    ss_prev = [scores(k_prev, hh, qs) + tbl_ref[hh, 1, :, qcols(qs)] + sel_row(hh, qs, jp)
               for hh, qs in chains]
    ms = [jnp.max(s, axis=0, keepdims=True) for s in ss]
    ps = [jnp.exp(s - m) for s, m in zip(ss, ms)]
    ls = [jnp.sum(p, axis=0, keepdims=True) for p in ps]
    accs = [pv(hh, p, i) for (hh, _), p in zip(chains, ps)]
    steps = [softmax_step(m, l, s) for m, l, s in zip(ms, ls, ss_prev)]
    accs = [alpha * acc + pv(hh, p, jp) for (hh, _), acc, (_, _, alpha, p) in zip(chains, accs, steps)]
    ms = [st[0] for st in steps]
    ls = [st[1] for st in steps]

    def body(j, state):
        ss_cur, ms, ls, accs, ps_prev = state
        ss_next = block_scores(j + 1)
        pvs = [pv(hh, p, jnp.maximum(j - 1, 0)) for (hh, _), p in zip(chains, ps_prev)]
        steps = [softmax_step(m, l, s) for m, l, s in zip(ms, ls, ss_cur)]
        accs = [alpha * (acc + r) for acc, r, (_, _, alpha, _) in zip(accs, pvs, steps)]
        return (ss_next, tuple(st[0] for st in steps), tuple(st[1] for st in steps), tuple(accs),
                tuple(st[3] for st in steps))

    zero_p = tuple(jnp.zeros((blk, MOBA_QLANES), BF16) for _ in chains)
    state = (block_scores(0), tuple(ms), tuple(ls), tuple(accs), zero_p)
    _, ms, ls, accs, ps_last = lax.fori_loop(0, jp, body, state)
    accs = [acc + pv(hh, p, jnp.maximum(jp - 1, 0)) for (hh, _), acc, p in zip(chains, accs, ps_last)]
    outs = [acc / l for l, acc in zip(ls, accs)]
    nqs = blk // MOBA_QLANES
    o_t = jnp.concatenate([jnp.concatenate(outs[hh * nqs:(hh + 1) * nqs], axis=1) for hh in range(2)],
                          axis=0)
    o_ref[...] = o_t.T.astype(o_ref.dtype)


def _t5_bucket_np(dist):
    max_exact = REL_BUCKETS // 2
    d = np.maximum(dist, 0)
    df = np.maximum(d, 1).astype(np.float32)
    large = max_exact + (np.log(df / max_exact) / math.log(REL_MAX_DIST / max_exact)
                         * (REL_BUCKETS - max_exact)).astype(np.int32)
    large = np.minimum(large, REL_BUCKETS - 1)
    return np.where(d < max_exact, d, large)


def _moba_bias_tables(rel_bias):
    blk = MOBA_BLOCK
    qi = np.arange(blk)[None, :]
    ki = np.arange(blk)[:, None]
    assert int(_t5_bucket_np(np.array([blk + 1]))[0]) == REL_BUCKETS - 1
    b_own = _t5_bucket_np(qi - ki)
    b_prev = _t5_bucket_np(qi - ki + blk)
    far = rel_bias[:, REL_BUCKETS - 1][:, None, None]
    own = jnp.where((qi >= ki)[None], rel_bias[:, b_own] - far, NEG)
    prev = rel_bias[:, b_prev] - far
    return jnp.stack([own, prev], axis=1).astype(F32)


def _moba_attention(qt, k, vt, kmean, tbl):
    b, s, d = k.shape
    blk = MOBA_BLOCK
    nb = s // blk
    nhp = d // LANES
    return pl.pallas_call(
        functools.partial(_moba_kernel, nb=nb),
        grid=(b, nhp, nb),
        in_specs=[
            pl.BlockSpec((None, LANES, blk), lambda bi, hp, i: (bi, hp, i)),
            pl.BlockSpec((None, s, LANES), lambda bi, hp, i: (bi, 0, hp)),
            pl.BlockSpec((None, LANES, s), lambda bi, hp, i: (bi, hp, 0)),
            pl.BlockSpec((None, nb, LANES), lambda bi, hp, i: (bi, 0, hp)),
            pl.BlockSpec((2, 2, blk, blk), lambda bi, hp, i: (hp, 0, 0, 0)),
        ],
        out_specs=pl.BlockSpec((None, blk, LANES), lambda bi, hp, i: (bi, i, hp)),
        out_shape=jax.ShapeDtypeStruct((b, s, d), BF16),
        scratch_shapes=[pltpu.VMEM((2, blk // MOBA_QLANES, nb, MOBA_QLANES), F32)],
        compiler_params=_cparams(("parallel", "parallel", "arbitrary")),
        name="moba_attention",
    )(qt, k, vt, kmean.reshape(b, nb, d), tbl)


def _peerq_epilogue(x_new, gf_ref, wpq_ref, x_out_ref, h_ref, pq_ref):
    x_out_ref[...] = x_new
    h = _rms(x_new, gf_ref[...]).astype(BF16)
    h_ref[...] = h
    pq_ref[...] = jnp.dot(h, wpq_ref[...], preferred_element_type=F32)


def _attn_out_kernel(o_ref, wo_ref, x_ref, gf_ref, wpq_ref, x_out_ref, h_ref, pq_ref):
    x_new = x_ref[...] + jnp.dot(o_ref[...], wo_ref[...], preferred_element_type=F32)
    _peerq_epilogue(x_new, gf_ref, wpq_ref, x_out_ref, h_ref, pq_ref)


def _attn_out_proj(o2d, wo_bf16, x2d, gf, wpq_bf16):
    t, d = x2d.shape
    nq = wpq_bf16.shape[1]
    tm = TM_PROJ
    return pl.pallas_call(
        _attn_out_kernel,
        grid=(t // tm,),
        in_specs=[
            pl.BlockSpec((tm, d), lambda i: (i, 0)),
            pl.BlockSpec((d, d), lambda i: (0, 0)),
            pl.BlockSpec((tm, d), lambda i: (i, 0)),
            pl.BlockSpec((1, d), lambda i: (0, 0)),
            pl.BlockSpec((d, nq), lambda i: (0, 0)),
        ],
        out_specs=[
            pl.BlockSpec((tm, d), lambda i: (i, 0)),
            pl.BlockSpec((tm, d), lambda i: (i, 0)),
            pl.BlockSpec((tm, nq), lambda i: (i, 0)),
        ],
        out_shape=[
            jax.ShapeDtypeStruct((t, d), F32),
            jax.ShapeDtypeStruct((t, d), BF16),
            jax.ShapeDtypeStruct((t, nq), F32),
        ],
        compiler_params=_cparams(("parallel",)),
        name="attn_out_proj",
    )(o2d, wo_bf16, x2d, gf.reshape(1, d), wpq_bf16)


def _cmpx(vals, a, b):
    hi = jnp.maximum(vals[a], vals[b])
    lo = jnp.minimum(vals[a], vals[b])
    vals[a], vals[b] = hi, lo


def _bitonic_merge_desc(vals):
    n = len(vals)
    dist = n // 2
    while dist >= 1:
        for s in range(0, n, 2 * dist):
            for t in range(s, s + dist):
                _cmpx(vals, t, t + dist)
        dist //= 2
    return vals


def _sort_desc(vals):
    n = len(vals)
    if n == 1:
        return vals
    top = _sort_desc(vals[: n // 2])
    bot = _sort_desc(vals[n // 2:])
    return _bitonic_merge_desc(top + bot[::-1])


def _top_merge(a, b):
    n = len(a)
    return _bitonic_merge_desc([jnp.maximum(a[r], b[n - 1 - r]) for r in range(n)])


def _sorted_top16(vals):
    groups = [_sort_desc(vals[g:g + PEER_TOPK]) for g in range(0, len(vals), PEER_TOPK)]
    while len(groups) > 1:
        groups = [_top_merge(groups[g], groups[g + 1]) for g in range(0, len(groups), 2)]
    return groups[0]


def _router_kernel(pq_ref, keys_ref, st_ref, sc_ref, scr_ref):
    nch = TT_ROUTER // LANES
    tops = []
    for c in range(2):
        qc = pq_ref[:, c * PEER_HALF:(c + 1) * PEER_HALF]
        s_t = lax.dot_general(keys_ref[0, c], qc, NT_DIMS,
                              precision=lax.Precision.HIGHEST,
                              preferred_element_type=F32)
        st_ref[0, c] = s_t
        for ch in range(nch):
            scr_ref[c, pl.ds(ch, PEER_NKEYS, stride=nch), :] = s_t[:, ch * LANES:(ch + 1) * LANES]
        vals = [scr_ref[c, k * nch:(k + 1) * nch, :] for k in range(PEER_NKEYS)]
        tops.append(_sorted_top16(vals))
    a, b = tops
    cands = [a[r] + b[c] for r in range(PEER_TOPK) for c in range(PEER_TOPK)
             if (r + 1) * (c + 1) <= PEER_TOPK]
    npad = 1 << (len(cands) - 1).bit_length()
    cands = cands + [jnp.full_like(cands[0], NEG)] * (npad - len(cands))
    best = _sort_desc(cands)[:PEER_TOPK]
    z = jnp.ones_like(best[0])
    for r in range(1, PEER_TOPK):
        z = z + jnp.exp(best[r] - best[0])
    for r in range(PEER_TOPK):
        sc_ref[0, SC_TOP1 + r] = a[r]
        sc_ref[0, SC_TOP2 + r] = b[r]
    sc_ref[0, SC_TAU] = best[PEER_TOPK - 1]
    sc_ref[0, SC_ZINV] = 1.0 / z


def _peer_router(pq, sub_keys):
    t = pq.shape[0]
    tt = TT_ROUTER
    nch = tt // LANES
    hp = PEER_HEADS
    return pl.pallas_call(
        _router_kernel,
        grid=(t // tt, hp),
        in_specs=[
            pl.BlockSpec((tt, 2 * PEER_HALF), lambda i, h: (i, h)),
            pl.BlockSpec((1, 2, PEER_NKEYS, PEER_HALF), lambda i, h: (h, 0, 0, 0)),
        ],
        out_specs=[
            pl.BlockSpec((1, 2, PEER_NKEYS, tt), lambda i, h: (h, 0, 0, i)),
            pl.BlockSpec((1, SC_ROWS, nch, LANES), lambda i, h: (h, 0, i, 0)),
        ],
        out_shape=[
            jax.ShapeDtypeStruct((hp, 2, PEER_NKEYS, t), F32),
            jax.ShapeDtypeStruct((hp, SC_ROWS, t // LANES, LANES), F32),
        ],
        scratch_shapes=[pltpu.VMEM((2, PEER_NKEYS * nch, LANES), F32)],
        compiler_params=_cparams(("parallel", "parallel")),
        name="peer_router",
    )(pq, sub_keys)


def _gelu2(x):
    c0 = math.sqrt(2.0 / math.pi)
    z = x * (c0 + (c0 * 0.044715) * (x * x))
    return x + x * jnp.tanh(z)


def _dup_bf16_bits(v):
    bits = pltpu.bitcast(v.astype(BF16).astype(F32), jnp.uint32)
    return bits | (bits >> 16)


def _peer_kernel(h_ref, st_ref, sc_ref, u_ref, vt_ref, x_ref, gfin_ref, o_ref,
                 rank_scr, bw_scr, cnt_scr, aw_scr, act_scr, wa_scr, acc_scr, *, final_norm):
    e = pl.program_id(1)
    tt = h_ref.shape[0]
    nsub = CE_STEP // CE_SUB
    ipc = CE_SUB // PEER_NKEYS
    pack = 2 * SUBLANES

    @pl.when(e == 0)
    def _():
        acc_scr[...] = jnp.zeros_like(acc_scr)
        for hd in range(PEER_HEADS):
            s1 = st_ref[hd, 0]
            s2 = st_ref[hd, 1]
            tau = sc_ref[hd, SC_TAU:SC_TAU + 1, :]
            rank2 = jnp.zeros_like(s2)
            cnt = jnp.zeros_like(s1)
            for c in range(PEER_TOPK):
                b_c = sc_ref[hd, SC_TOP2 + c:SC_TOP2 + c + 1, :]
                rank2 = jnp.where(b_c > s2, c + 1.0, rank2)
                cnt = jnp.where(s1 + b_c >= tau, c + 1.0, cnt)
            a0 = sc_ref[hd, SC_TOP1:SC_TOP1 + 1, :]
            b0 = sc_ref[hd, SC_TOP2:SC_TOP2 + 1, :]
            zinv = sc_ref[hd, SC_ZINV:SC_ZINV + 1, :]
            rank_scr[hd] = rank2.astype(BF16)
            bw_scr[hd] = (jnp.exp(s2 - b0) * (0.5 * zinv)).astype(BF16)
            cnt_w = _dup_bf16_bits(cnt)
            a_w = _dup_bf16_bits(jnp.exp(s1 - a0))
            for lg in range(tt // LANES):
                cnt_scr[hd, lg] = cnt_w[:, lg * LANES:(lg + 1) * LANES]
                aw_scr[hd, lg] = a_w[:, lg * LANES:(lg + 1) * LANES]

    def row_bf16(scr, hd, row, lg):
        words = jnp.broadcast_to(scr[hd, lg, pl.ds(row, 1), :], (SUBLANES, LANES))
        return pltpu.bitcast(words, BF16)

    def up_proj(c):
        act_scr[c % 2] = lax.dot_general(u_ref[c * CE_SUB:(c + 1) * CE_SUB, :], h_ref[...], NT_DIMS,
                                         preferred_element_type=F32)

    def down_proj(c):
        acc_scr[...] += jnp.dot(vt_ref[:, c * CE_SUB:(c + 1) * CE_SUB], wa_scr[c % 2],
                                preferred_element_type=F32)

    def gate(c):
        njg = PEER_NKEYS // pack
        zero = jnp.zeros((pack, LANES), BF16)
        for lg in range(tt // LANES):
            lanes = slice(lg * LANES, (lg + 1) * LANES)
            w = [[None] * njg for _ in range(ipc)]
            for hd in range(PEER_HEADS):
                rank = [rank_scr[hd, jg * pack:(jg + 1) * pack, lanes] for jg in range(njg)]
                bw = [bw_scr[hd, jg * pack:(jg + 1) * pack, lanes] for jg in range(njg)]
                for il in range(ipc):
                    row = (e * nsub + c) * ipc + il
                    cnt_row = row_bf16(cnt_scr, hd, row, lg)
                    a_row = row_bf16(aw_scr, hd, row, lg)
                    for jg in range(njg):
                        term = jnp.where(rank[jg] < cnt_row, a_row * bw[jg], zero)
                        w[il][jg] = term if w[il][jg] is None else w[il][jg] + term
            for il in range(ipc):
                for jg in range(njg):
                    rows = slice(il * PEER_NKEYS + jg * pack, il * PEER_NKEYS + (jg + 1) * pack)
                    wa_scr[c % 2, rows, lanes] = w[il][jg] * _gelu2(act_scr[c % 2, rows, lanes]).astype(BF16)

    up_proj(0)
    for c in range(nsub):
        if c + 1 < nsub:
            up_proj(c + 1)
        if c >= 1:
            down_proj(c - 1)
        gate(c)
    down_proj(nsub - 1)

    @pl.when(e == pl.num_programs(1) - 1)
    def _():
        y = x_ref[...] + acc_scr[...].T
        if final_norm:
            y = _rms(y, gfin_ref[...])
        o_ref[...] = y


def _peer_experts(h_bf16, st, sc, u_bf16, vt_bf16, x2d, gfin, final_norm):
    t, d = x2d.shape
    ne = u_bf16.shape[0]
    tt, ce = TT_PEER, CE_STEP
    hp = PEER_HEADS
    return pl.pallas_call(
        functools.partial(_peer_kernel, final_norm=final_norm),
        grid=(t // tt, ne // ce),
        in_specs=[
            pl.BlockSpec((tt, d), lambda i, e: (i, 0)),
            pl.BlockSpec((hp, 2, PEER_NKEYS, tt), lambda i, e: (0, 0, 0, i)),
            pl.BlockSpec((hp, SC_ROWS, tt), lambda i, e: (0, 0, i)),
            pl.BlockSpec((ce, d), lambda i, e: (e, 0)),
            pl.BlockSpec((d, ce), lambda i, e: (0, e)),
            pl.BlockSpec((tt, d), lambda i, e: (i, 0)),
            pl.BlockSpec((1, d), lambda i, e: (0, 0)),
        ],
        out_specs=pl.BlockSpec((tt, d), lambda i, e: (i, 0)),
        out_shape=jax.ShapeDtypeStruct((t, d), F32),
        scratch_shapes=[
            pltpu.VMEM((hp, PEER_NKEYS, tt), BF16),
            pltpu.VMEM((hp, PEER_NKEYS, tt), BF16),
            pltpu.VMEM((hp, tt // LANES, PEER_NKEYS, LANES), jnp.uint32),
            pltpu.VMEM((hp, tt // LANES, PEER_NKEYS, LANES), jnp.uint32),
            pltpu.VMEM((2, CE_SUB, tt), F32),
            pltpu.VMEM((2, CE_SUB, tt), BF16),
            pltpu.VMEM((d, tt), F32),
        ],
        compiler_params=_cparams(("parallel", "arbitrary")),
        name="peer_experts",
    )(h_bf16, st, sc.reshape(hp, SC_ROWS, t), u_bf16, vt_bf16, x2d, gfin.reshape(1, d))


def _peer(h_bf16, pq, sub_keys, u_bf16, vt_bf16, x2d, gfin, final_norm):
    st, sc = _peer_router(pq, sub_keys)
    return _peer_experts(h_bf16, st, sc, u_bf16, vt_bf16, x2d, gfin, final_norm)


def _glu_kernel(x_ref, g_ref, w_ref, b_ref, u_ref, *, d):
    h = _rms(x_ref[...], g_ref[...]).astype(BF16)
    a = jnp.dot(h, w_ref[...], preferred_element_type=F32) + b_ref[...]
    u_ref[...] = a[:, :d] * jax.nn.sigmoid(a[:, d:])


def _glu_proj(x2d, g, w_bf16, bias):
    t, d = x2d.shape
    tm = TM_PROJ
    return pl.pallas_call(
        functools.partial(_glu_kernel, d=d),
        grid=(t // tm,),
        in_specs=[
            pl.BlockSpec((tm, d), lambda i: (i, 0)),
            pl.BlockSpec((1, d), lambda i: (0, 0)),
            pl.BlockSpec((d, 2 * d), lambda i: (0, 0)),
            pl.BlockSpec((1, 2 * d), lambda i: (0, 0)),
        ],
        out_specs=pl.BlockSpec((tm, d), lambda i: (i, 0)),
        out_shape=jax.ShapeDtypeStruct((t, d), F32),
        compiler_params=_cparams(("parallel",)),
        name="glu_proj",
    )(x2d, g.reshape(1, d), w_bf16, bias.reshape(1, 2 * d))


def _conv_kernel(ucur_ref, uprev_ref, wdw_ref, bdw_ref, lng_ref, lnb_ref, w2_ref, b2_ref,
                 x_ref, gf_ref, wpq_ref, x_out_ref, h_ref, pq_ref, ext_scr, y_scr):
    tm, d = ucur_ref.shape
    first = pl.program_id(1) == 0
    ext_scr[0:CONV_HALO, :] = jnp.where(first, 0.0, uprev_ref[...])
    ext_scr[CONV_HALO:, :] = ucur_ref[...]
    lead = CONV_HALO - (CONV_WIDTH - 1)

    def chunk(ci, carry):
        r0 = pl.multiple_of(ci * CONV_ROWS, CONV_ROWS)
        window = ext_scr[pl.ds(r0, CONV_ROWS + CONV_HALO), :]
        acc = jnp.broadcast_to(bdw_ref[...], (CONV_ROWS, d))
        for w in range(CONV_WIDTH):
            acc = acc + wdw_ref[w:w + 1, :] * window[lead + w:lead + w + CONV_ROWS, :]
        y_scr[pl.ds(r0, CONV_ROWS), :] = acc
        return carry

    lax.fori_loop(0, tm // CONV_ROWS, chunk, 0)
    y = y_scr[...]
    mu = jnp.mean(y, axis=-1, keepdims=True)
    var = jnp.mean(jnp.square(y - mu), axis=-1, keepdims=True)
    y = (y - mu) * lax.rsqrt(var + EPS) * lng_ref[...] + lnb_ref[...]
    y = y * jax.nn.sigmoid(y)
    x_new = x_ref[...] + jnp.dot(y.astype(BF16), w2_ref[...], preferred_element_type=F32) + b2_ref[...]
    _peerq_epilogue(x_new, gf_ref, wpq_ref, x_out_ref, h_ref, pq_ref)


def _conv_tail(u3, wdw, bdw, lng, lnb, w2_bf16, b2, x3, gf, wpq_bf16):
    b, s, d = x3.shape
    nq = wpq_bf16.shape[1]
    tm = TM_CONV
    hpb = tm // CONV_HALO
    row = lambda v: v.reshape(1, d)
    tok = pl.BlockSpec((None, tm, d), lambda bi, i: (bi, i, 0))
    vec = pl.BlockSpec((1, d), lambda bi, i: (0, 0))
    return pl.pallas_call(
        _conv_kernel,
        grid=(b, s // tm),
        in_specs=[
            tok,
            pl.BlockSpec((None, CONV_HALO, d), lambda bi, i: (bi, jnp.maximum(i * hpb - 1, 0), 0)),
            pl.BlockSpec((CONV_WIDTH, d), lambda bi, i: (0, 0)),
            vec, vec, vec,
            pl.BlockSpec((d, d), lambda bi, i: (0, 0)),
            vec,
            tok,
            vec,
            pl.BlockSpec((d, nq), lambda bi, i: (0, 0)),
        ],
        out_specs=[
            tok,
            tok,
            pl.BlockSpec((None, tm, nq), lambda bi, i: (bi, i, 0)),
        ],
        out_shape=[
            jax.ShapeDtypeStruct((b, s, d), F32),
            jax.ShapeDtypeStruct((b, s, d), BF16),
            jax.ShapeDtypeStruct((b, s, nq), F32),
        ],
        scratch_shapes=[
            pltpu.VMEM((tm + CONV_HALO, d), F32),
            pltpu.VMEM((tm, d), F32),
        ],
        compiler_params=_cparams(("parallel", "arbitrary")),
        name="conv_tail",
    )(u3, u3, wdw, row(bdw), row(lng), row(lnb), w2_bf16, row(b2), x3, row(gf), wpq_bf16)


def kernel(x, rel_bias, norm_mix, norm_ffn, attn_w_qkv, attn_w_o, conv_w_pw1, conv_b_pw1,
           conv_w_dw, conv_b_dw, conv_ln_g, conv_ln_b, conv_w_pw2, conv_b_pw2, peer_w_q,
           peer_sub_keys, peer_u, peer_v, norm_final):
    b, s, d = x.shape
    t = b * s
    depth = norm_mix.shape[0]
    assert d == N_HEADS * HEAD_DIM and s % MOBA_BLOCK == 0 and s // MOBA_BLOCK <= LANES
    assert t % TT_ROUTER == 0 and s % TM_CONV == 0 and t % TM_PROJ == 0
    x2d = x.reshape(t, d)
    for i in range(depth):
        j = i // 2
        wpq = peer_w_q[i].astype(BF16)
        if i % 2 == 0:
            wq, wk, wv = (attn_w_qkv[j][:, c * d:(c + 1) * d] for c in range(3))
            qt, k, vt, kmean = _qkv_proj(x2d.reshape(b, s, d), norm_mix[i], wq.T.astype(BF16),
                                         wk.astype(BF16), wv.T.astype(BF16))
            o = _moba_attention(qt, k, vt, kmean, _moba_bias_tables(rel_bias))
            x2d, h, pq = _attn_out_proj(o.reshape(t, d), attn_w_o[j].astype(BF16), x2d,
                                        norm_ffn[i], wpq)
        else:
            u = _glu_proj(x2d, norm_mix[i], conv_w_pw1[j].astype(BF16), conv_b_pw1[j])
            x3, h3, pq3 = _conv_tail(u.reshape(b, s, d), conv_w_dw[j], conv_b_dw[j], conv_ln_g[j],
                                     conv_ln_b[j], conv_w_pw2[j].astype(BF16), conv_b_pw2[j],
                                     x2d.reshape(b, s, d), norm_ffn[i], wpq)
            x2d, h, pq = x3.reshape(t, d), h3.reshape(t, d), pq3.reshape(t, -1)
        last = i == depth - 1
        x2d = _peer(h, pq, peer_sub_keys[i], peer_u[i].astype(BF16),
                    peer_v[i].T.astype(BF16), x2d, norm_final, final_norm=last)
    return x2d.reshape(b, s, d)
```

```python
import functools
import math

import numpy as np
import jax
import jax.numpy as jnp
from jax import lax
from jax.experimental import pallas as pl
from jax.experimental.pallas import tpu as pltpu

F32 = jnp.float32
BF16 = jnp.bfloat16
EPS = 1e-6
NEG = -1e30

N_HEADS = 16
HEAD_DIM = 64
MOBA_BLOCK = 256
MOBA_TOPK = 3
REL_BUCKETS = 32
REL_MAX_DIST = 128
CONV_WIDTH = 31
PEER_HEADS = 8
PEER_NKEYS = 128
PEER_HALF = 128
PEER_TOPK = 16
SC_TOP1, SC_TOP2, SC_TAU, SC_ZINV, SC_ROWS = 0, PEER_TOPK, 2 * PEER_TOPK, 2 * PEER_TOPK + 1, 2 * PEER_TOPK + 2

LANES = 128
SUBLANES = 8
VMEM_LIMIT = 56 * 1024 * 1024

MOBA_QLANES = 128
TM_PROJ = 512
TT_ROUTER = 1024
TT_PEER = 512
CE_STEP = 2048
CE_SUB = 512
TM_CONV = 512
CONV_HALO = 32
CONV_ROWS = 16

NT_DIMS = (((1,), (1,)), ((), ()))


def _cparams(sem):
    return pltpu.CompilerParams(dimension_semantics=sem, vmem_limit_bytes=VMEM_LIMIT)


def _rms(x, g):
    return x * lax.rsqrt(jnp.mean(x * x, axis=-1, keepdims=True) + EPS) * g


def _qkv_kernel(x_ref, g_ref, wqt_ref, wk_ref, wvt_ref, qt_ref, k_ref, vt_ref, km_ref):
    h = _rms(x_ref[...], g_ref[...]).astype(BF16)
    scale = HEAD_DIM ** -0.5
    qt = lax.dot_general(wqt_ref[...], h, NT_DIMS, preferred_element_type=F32)
    qt_ref[...] = (qt * scale).astype(BF16)
    k = jnp.dot(h, wk_ref[...], preferred_element_type=F32)
    k_ref[...] = k.astype(BF16)
    vt = lax.dot_general(wvt_ref[...], h, NT_DIMS, preferred_element_type=F32)
    vt_ref[...] = vt.astype(BF16)
    nblk = k.shape[0] // MOBA_BLOCK
    km_ref[...] = jnp.mean(k.reshape(nblk, MOBA_BLOCK, k.shape[1]), axis=1)


def _qkv_proj(x3, g, wqt_bf16, wk_bf16, wvt_bf16):
    b, s, d = x3.shape
    tm = TM_PROJ
    nblk = tm // MOBA_BLOCK
    nb = s // MOBA_BLOCK
    wspec = pl.BlockSpec((d, d), lambda bi, i: (0, 0))
    tspec = pl.BlockSpec((None, d, tm), lambda bi, i: (bi, 0, i))
    return pl.pallas_call(
        _qkv_kernel,
        grid=(b, s // tm),
        in_specs=[
            pl.BlockSpec((None, tm, d), lambda bi, i: (bi, i, 0)),
            pl.BlockSpec((1, d), lambda bi, i: (0, 0)),
            wspec, wspec, wspec,
        ],
        out_specs=[
            tspec,
            pl.BlockSpec((None, tm, d), lambda bi, i: (bi, i, 0)),
            tspec,
            pl.BlockSpec((None, None, nblk, d), lambda bi, i: (bi, i, 0, 0)),
        ],
        out_shape=[
            jax.ShapeDtypeStruct((b, d, s), BF16),
            jax.ShapeDtypeStruct((b, s, d), BF16),
            jax.ShapeDtypeStruct((b, d, s), BF16),
            jax.ShapeDtypeStruct((b, s // tm, nblk, d), F32),
        ],
        compiler_params=_cparams(("parallel", "parallel")),
        name="qkv_proj",
    )(x3, g.reshape(1, d), wqt_bf16, wk_bf16, wvt_bf16)


def _moba_kernel(qt_ref, k_ref, vt_ref, km_ref, tbl_ref, o_ref, sel_scr, *, nb):
    blk = MOBA_BLOCK
    i = pl.program_id(2)
    jp = jnp.maximum(i - 1, 0)
    qt = qt_ref[...]
    frow = lax.broadcasted_iota(jnp.int32, qt.shape, 0)
    blk_id = lax.broadcasted_iota(jnp.int32, (nb, blk), 0)
    valid = blk_id < i
    km = km_ref[...]

    qh = []
    for hh in range(2):
        qh_t = jnp.where(frow // HEAD_DIM == hh, qt, jnp.zeros_like(qt))
        qh.append(qh_t)
        gate = jnp.dot(km, qh_t.astype(F32), precision=lax.Precision.HIGHEST,
                       preferred_element_type=F32)
        gate = jnp.where(valid, gate, NEG)
        g = gate
        for _ in range(MOBA_TOPK - 1):
            g = jnp.where(g >= jnp.max(g, axis=0, keepdims=True), NEG, g)
        tau = jnp.max(g, axis=0, keepdims=True)
        sel = jnp.where(valid, jnp.where(gate >= tau, 0.0, NEG), NEG)
        for qs in range(blk // MOBA_QLANES):
            sel_scr[hh, qs] = sel[:, qs * MOBA_QLANES:(qs + 1) * MOBA_QLANES]

    def k_block(j):
        return k_ref[pl.ds(pl.multiple_of(j * blk, blk), blk), :]

    chains = [(hh, qs) for hh in range(2) for qs in range(blk // MOBA_QLANES)]

    def qcols(qs):
        return slice(qs * MOBA_QLANES, (qs + 1) * MOBA_QLANES)

    def scores(kj, hh, qs):
        return jnp.dot(kj, qh[hh][:, qcols(qs)], preferred_element_type=F32)

    def sel_row(hh, qs, j):
        return sel_scr[hh, qs, pl.ds(j, 1), :]

    def pv(hh, p, j):
        vj = vt_ref[hh * HEAD_DIM:(hh + 1) * HEAD_DIM, pl.ds(pl.multiple_of(j * blk, blk), blk)]
        return jnp.dot(vj, p.astype(BF16), preferred_element_type=F32)

    def softmax_step(m_prev, l_prev, s):
        m_new = jnp.maximum(m_prev, jnp.max(s, axis=0, keepdims=True))
        alpha = jnp.exp(m_prev - m_new)
        p = jnp.exp(s - m_new)
        l_new = alpha * l_prev + jnp.sum(p, axis=0, keepdims=True)
        return m_new, l_new, alpha, p.astype(BF16)

    def block_scores(j):
        kj = k_block(j)
        return tuple(scores(kj, hh, qs) + sel_row(hh, qs, j) for hh, qs in chains)

    k_own, k_prev = k_block(i), k_block(jp)
    ss = [scores(k_own, hh, qs) + tbl_ref[hh, 0, :, qcols(qs)] for hh, qs in chains]
    ss_prev = [scores(k_prev, hh, qs) + tbl_ref[hh, 1, :, qcols(qs)] + sel_row(hh, qs, jp)
               for hh, qs in chains]
    ms = [jnp.max(s, axis=0, keepdims=True) for s in ss]
    ps = [jnp.exp(s - m) for s, m in zip(ss, ms)]
    ls = [jnp.sum(p, axis=0, keepdims=True) for p in ps]
    accs = [pv(hh, p, i) for (hh, _), p in zip(chains, ps)]
    steps = [softmax_step(m, l, s) for m, l, s in zip(ms, ls, ss_prev)]
    accs = [alpha * acc + pv(hh, p, jp) for (hh, _), acc, (_, _, alpha, p) in zip(chains, accs, steps)]
    ms = [st[0] for st in steps]
    ls = [st[1] for st in steps]

    def body(j, state):
        ss_cur, ms, ls, accs, ps_prev = state
        ss_next = block_scores(j + 1)
        pvs = [pv(hh, p, jnp.maximum(j - 1, 0)) for (hh, _), p in zip(chains, ps_prev)]
        steps = [softmax_step(m, l, s) for m, l, s in zip(ms, ls, ss_cur)]
        accs = [alpha * (acc + r) for acc, r, (_, _, alpha, _) in zip(accs, pvs, steps)]
        return (ss_next, tuple(st[0] for st in steps), tuple(st[1] for st in steps), tuple(accs),
                tuple(st[3] for st in steps))

    zero_p = tuple(jnp.zeros((blk, MOBA_QLANES), BF16) for _ in chains)
    state = (block_scores(0), tuple(ms), tuple(ls), tuple(accs), zero_p)
    _, ms, ls, accs, ps_last = lax.fori_loop(0, jp, body, state)
    accs = [acc + pv(hh, p, jnp.maximum(jp - 1, 0)) for (hh, _), acc, p in zip(chains, accs, ps_last)]
    outs = [acc / l for l, acc in zip(ls, accs)]
    nqs = blk // MOBA_QLANES
    o_t = jnp.concatenate([jnp.concatenate(outs[hh * nqs:(hh + 1) * nqs], axis=1) for hh in range(2)],
                          axis=0)
    o_ref[...] = o_t.T.astype(o_ref.dtype)


def _t5_bucket_np(dist):
    max_exact = REL_BUCKETS // 2
    d = np.maximum(dist, 0)
    df = np.maximum(d, 1).astype(np.float32)
    large = max_exact + (np.log(df / max_exact) / math.log(REL_MAX_DIST / max_exact)
                         * (REL_BUCKETS - max_exact)).astype(np.int32)
    large = np.minimum(large, REL_BUCKETS - 1)
    return np.where(d < max_exact, d, large)


def _moba_bias_tables(rel_bias):
    blk = MOBA_BLOCK
    qi = np.arange(blk)[None, :]
    ki = np.arange(blk)[:, None]
    assert int(_t5_bucket_np(np.array([blk + 1]))[0]) == REL_BUCKETS - 1
    b_own = _t5_bucket_np(qi - ki)
    b_prev = _t5_bucket_np(qi - ki + blk)
    far = rel_bias[:, REL_BUCKETS - 1][:, None, None]
    own = jnp.where((qi >= ki)[None], rel_bias[:, b_own] - far, NEG)
    prev = rel_bias[:, b_prev] - far
    return jnp.stack([own, prev], axis=1).astype(F32)


def _moba_attention(qt, k, vt, kmean, tbl):
    b, s, d = k.shape
    blk = MOBA_BLOCK
    nb = s // blk
    nhp = d // LANES
    return pl.pallas_call(
        functools.partial(_moba_kernel, nb=nb),
        grid=(b, nhp, nb),
        in_specs=[
            pl.BlockSpec((None, LANES, blk), lambda bi, hp, i: (bi, hp, i)),
            pl.BlockSpec((None, s, LANES), lambda bi, hp, i: (bi, 0, hp)),
            pl.BlockSpec((None, LANES, s), lambda bi, hp, i: (bi, hp, 0)),
            pl.BlockSpec((None, nb, LANES), lambda bi, hp, i: (bi, 0, hp)),
            pl.BlockSpec((2, 2, blk, blk), lambda bi, hp, i: (hp, 0, 0, 0)),
        ],
        out_specs=pl.BlockSpec((None, blk, LANES), lambda bi, hp, i: (bi, i, hp)),
        out_shape=jax.ShapeDtypeStruct((b, s, d), BF16),
        scratch_shapes=[pltpu.VMEM((2, blk // MOBA_QLANES, nb, MOBA_QLANES), F32)],
        compiler_params=_cparams(("parallel", "parallel", "arbitrary")),
        name="moba_attention",
    )(qt, k, vt, kmean.reshape(b, nb, d), tbl)


def _peerq_epilogue(x_new, gf_ref, wpq_ref, x_out_ref, h_ref, pq_ref):
    x_out_ref[...] = x_new
    h = _rms(x_new, gf_ref[...]).astype(BF16)
    h_ref[...] = h
    pq_ref[...] = jnp.dot(h, wpq_ref[...], preferred_element_type=F32)


def _attn_out_kernel(o_ref, wo_ref, x_ref, gf_ref, wpq_ref, x_out_ref, h_ref, pq_ref):
    x_new = x_ref[...] + jnp.dot(o_ref[...], wo_ref[...], preferred_element_type=F32)
    _peerq_epilogue(x_new, gf_ref, wpq_ref, x_out_ref, h_ref, pq_ref)


def _attn_out_proj(o2d, wo_bf16, x2d, gf, wpq_bf16):
    t, d = x2d.shape
    nq = wpq_bf16.shape[1]
    tm = TM_PROJ
    return pl.pallas_call(
        _attn_out_kernel,
        grid=(t // tm,),
        in_specs=[
            pl.BlockSpec((tm, d), lambda i: (i, 0)),
            pl.BlockSpec((d, d), lambda i: (0, 0)),
            pl.BlockSpec((tm, d), lambda i: (i, 0)),
            pl.BlockSpec((1, d), lambda i: (0, 0)),
            pl.BlockSpec((d, nq), lambda i: (0, 0)),
        ],
        out_specs=[
            pl.BlockSpec((tm, d), lambda i: (i, 0)),
            pl.BlockSpec((tm, d), lambda i: (i, 0)),
            pl.BlockSpec((tm, nq), lambda i: (i, 0)),
        ],
        out_shape=[
            jax.ShapeDtypeStruct((t, d), F32),
            jax.ShapeDtypeStruct((t, d), BF16),
            jax.ShapeDtypeStruct((t, nq), F32),
        ],
        compiler_params=_cparams(("parallel",)),
        name="attn_out_proj",
    )(o2d, wo_bf16, x2d, gf.reshape(1, d), wpq_bf16)


def _cmpx(vals, a, b):
    hi = jnp.maximum(vals[a], vals[b])
    lo = jnp.minimum(vals[a], vals[b])
    vals[a], vals[b] = hi, lo


def _bitonic_merge_desc(vals):
    n = len(vals)
    dist = n // 2
    while dist >= 1:
        for s in range(0, n, 2 * dist):
            for t in range(s, s + dist):
                _cmpx(vals, t, t + dist)
        dist //= 2
    return vals


def _sort_desc(vals):
    n = len(vals)
    if n == 1:
        return vals
    top = _sort_desc(vals[: n // 2])
    bot = _sort_desc(vals[n // 2:])
    return _bitonic_merge_desc(top + bot[::-1])


def _top_merge(a, b):
    n = len(a)
    return _bitonic_merge_desc([jnp.maximum(a[r], b[n - 1 - r]) for r in range(n)])


def _sorted_top16(vals):
    groups = [_sort_desc(vals[g:g + PEER_TOPK]) for g in range(0, len(vals), PEER_TOPK)]
    while len(groups) > 1:
        groups = [_top_merge(groups[g], groups[g + 1]) for g in range(0, len(groups), 2)]
    return groups[0]


def _router_kernel(pq_ref, keys_ref, st_ref, sc_ref, scr_ref):
    nch = TT_ROUTER // LANES
    tops = []
    for c in range(2):
        qc = pq_ref[:, c * PEER_HALF:(c + 1) * PEER_HALF]
        s_t = lax.dot_general(keys_ref[0, c], qc, NT_DIMS,
                              precision=lax.Precision.HIGHEST,
                              preferred_element_type=F32)
        st_ref[0, c] = s_t
        for ch in range(nch):
            scr_ref[c, pl.ds(ch, PEER_NKEYS, stride=nch), :] = s_t[:, ch * LANES:(ch + 1) * LANES]
        vals = [scr_ref[c, k * nch:(k + 1) * nch, :] for k in range(PEER_NKEYS)]
        tops.append(_sorted_top16(vals))
    a, b = tops
    cands = [a[r] + b[c] for r in range(PEER_TOPK) for c in range(PEER_TOPK)
             if (r + 1) * (c + 1) <= PEER_TOPK]
    npad = 1 << (len(cands) - 1).bit_length()
    cands = cands + [jnp.full_like(cands[0], NEG)] * (npad - len(cands))
    best = _sort_desc(cands)[:PEER_TOPK]
    z = jnp.ones_like(best[0])
    for r in range(1, PEER_TOPK):
        z = z + jnp.exp(best[r] - best[0])
    for r in range(PEER_TOPK):
        sc_ref[0, SC_TOP1 + r] = a[r]
        sc_ref[0, SC_TOP2 + r] = b[r]
    sc_ref[0, SC_TAU] = best[PEER_TOPK - 1]
    sc_ref[0, SC_ZINV] = 1.0 / z


def _peer_router(pq, sub_keys):
    t = pq.shape[0]
    tt = TT_ROUTER
    nch = tt // LANES
    hp = PEER_HEADS
    return pl.pallas_call(
        _router_kernel,
        grid=(t // tt, hp),
        in_specs=[
            pl.BlockSpec((tt, 2 * PEER_HALF), lambda i, h: (i, h)),
            pl.BlockSpec((1, 2, PEER_NKEYS, PEER_HALF), lambda i, h: (h, 0, 0, 0)),
        ],
        out_specs=[
            pl.BlockSpec((1, 2, PEER_NKEYS, tt), lambda i, h: (h, 0, 0, i)),
            pl.BlockSpec((1, SC_ROWS, nch, LANES), lambda i, h: (h, 0, i, 0)),
        ],
        out_shape=[
            jax.ShapeDtypeStruct((hp, 2, PEER_NKEYS, t), F32),
            jax.ShapeDtypeStruct((hp, SC_ROWS, t // LANES, LANES), F32),
        ],
        scratch_shapes=[pltpu.VMEM((2, PEER_NKEYS * nch, LANES), F32)],
        compiler_params=_cparams(("parallel", "parallel")),
        name="peer_router",
    )(pq, sub_keys)


def _gelu2(x):
    c0 = math.sqrt(2.0 / math.pi)
    z = x * (c0 + (c0 * 0.044715) * (x * x))
    return x + x * jnp.tanh(z)


def _dup_bf16_bits(v):
    bits = pltpu.bitcast(v.astype(BF16).astype(F32), jnp.uint32)
    return bits | (bits >> 16)


def _peer_kernel(h_ref, st_ref, sc_ref, u_ref, vt_ref, x_ref, gfin_ref, o_ref,
                 rank_scr, bw_scr, cnt_scr, aw_scr, act_scr, wa_scr, acc_scr, *, final_norm):
    e = pl.program_id(1)
    tt = h_ref.shape[0]
    nsub = CE_STEP // CE_SUB
    ipc = CE_SUB // PEER_NKEYS
    pack = 2 * SUBLANES

    @pl.when(e == 0)
    def _():
        acc_scr[...] = jnp.zeros_like(acc_scr)
        for hd in range(PEER_HEADS):
            s1 = st_ref[hd, 0]
            s2 = st_ref[hd, 1]
            tau = sc_ref[hd, SC_TAU:SC_TAU + 1, :]
            rank2 = jnp.zeros_like(s2)
            cnt = jnp.zeros_like(s1)
            for c in range(PEER_TOPK):
                b_c = sc_ref[hd, SC_TOP2 + c:SC_TOP2 + c + 1, :]
                rank2 = jnp.where(b_c > s2, c + 1.0, rank2)
                cnt = jnp.where(s1 + b_c >= tau, c + 1.0, cnt)
            a0 = sc_ref[hd, SC_TOP1:SC_TOP1 + 1, :]
            b0 = sc_ref[hd, SC_TOP2:SC_TOP2 + 1, :]
            zinv = sc_ref[hd, SC_ZINV:SC_ZINV + 1, :]
            rank_scr[hd] = rank2.astype(BF16)
            bw_scr[hd] = (jnp.exp(s2 - b0) * (0.5 * zinv)).astype(BF16)
            cnt_w = _dup_bf16_bits(cnt)
            a_w = _dup_bf16_bits(jnp.exp(s1 - a0))
            for lg in range(tt // LANES):
                cnt_scr[hd, lg] = cnt_w[:, lg * LANES:(lg + 1) * LANES]
                aw_scr[hd, lg] = a_w[:, lg * LANES:(lg + 1) * LANES]

    def row_bf16(tile, r):
        words = jnp.broadcast_to(tile[r:r + 1, :], (SUBLANES, LANES))
        return pltpu.bitcast(words, BF16)

    def up_proj(c):
        act_scr[c % 2] = lax.dot_general(u_ref[c * CE_SUB:(c + 1) * CE_SUB, :], h_ref[...], NT_DIMS,
                                         preferred_element_type=F32)

    def down_proj(c):
        acc_scr[...] += jnp.dot(vt_ref[:, c * CE_SUB:(c + 1) * CE_SUB], wa_scr[c % 2],
                                preferred_element_type=F32)

    def gate(c):
        njg = PEER_NKEYS // pack
        zero = jnp.zeros((pack, LANES), BF16)
        for lg in range(tt // LANES):
            lanes = slice(lg * LANES, (lg + 1) * LANES)
            w = [[None] * njg for _ in range(ipc)]
            for hd in range(PEER_HEADS):
                rank = [rank_scr[hd, jg * pack:(jg + 1) * pack, lanes] for jg in range(njg)]
                bw = [bw_scr[hd, jg * pack:(jg + 1) * pack, lanes] for jg in range(njg)]
                first = (e * nsub + c) * ipc
                base = pl.multiple_of((first // SUBLANES) * SUBLANES, SUBLANES)
                off = (c * ipc) % SUBLANES
                cnt_tile = cnt_scr[hd, lg, pl.ds(base, SUBLANES), :]
                a_tile = aw_scr[hd, lg, pl.ds(base, SUBLANES), :]
                for il in range(ipc):
                    cnt_row = row_bf16(cnt_tile, off + il)
                    a_row = row_bf16(a_tile, off + il)
                    for jg in range(njg):
                        term = jnp.where(rank[jg] < cnt_row, a_row * bw[jg], zero)
                        w[il][jg] = term if w[il][jg] is None else w[il][jg] + term
            for il in range(ipc):
                for jg in range(njg):
                    rows = slice(il * PEER_NKEYS + jg * pack, il * PEER_NKEYS + (jg + 1) * pack)
                    wa_scr[c % 2, rows, lanes] = w[il][jg] * _gelu2(act_scr[c % 2, rows, lanes]).astype(BF16)

    up_proj(0)
    for c in range(nsub):
        if c + 1 < nsub:
            up_proj(c + 1)
        if c >= 1:
            down_proj(c - 1)
        gate(c)
    down_proj(nsub - 1)

    @pl.when(e == pl.num_programs(1) - 1)
    def _():
        y = x_ref[...] + acc_scr[...].T
        if final_norm:
            y = _rms(y, gfin_ref[...])
        o_ref[...] = y


def _peer_experts(h_bf16, st, sc, u_bf16, vt_bf16, x2d, gfin, final_norm):
    t, d = x2d.shape
    ne = u_bf16.shape[0]
    tt, ce = TT_PEER, CE_STEP
    hp = PEER_HEADS
    return pl.pallas_call(
        functools.partial(_peer_kernel, final_norm=final_norm),
        grid=(t // tt, ne // ce),
        in_specs=[
            pl.BlockSpec((tt, d), lambda i, e: (i, 0)),
            pl.BlockSpec((hp, 2, PEER_NKEYS, tt), lambda i, e: (0, 0, 0, i)),
            pl.BlockSpec((hp, SC_ROWS, tt), lambda i, e: (0, 0, i)),
            pl.BlockSpec((ce, d), lambda i, e: (e, 0)),
            pl.BlockSpec((d, ce), lambda i, e: (0, e)),
            pl.BlockSpec((tt, d), lambda i, e: (i, 0)),
            pl.BlockSpec((1, d), lambda i, e: (0, 0)),
        ],
        out_specs=pl.BlockSpec((tt, d), lambda i, e: (i, 0)),
        out_shape=jax.ShapeDtypeStruct((t, d), F32),
        scratch_shapes=[
            pltpu.VMEM((hp, PEER_NKEYS, tt), BF16),
            pltpu.VMEM((hp, PEER_NKEYS, tt), BF16),
            pltpu.VMEM((hp, tt // LANES, PEER_NKEYS, LANES), jnp.uint32),
            pltpu.VMEM((hp, tt // LANES, PEER_NKEYS, LANES), jnp.uint32),
            pltpu.VMEM((2, CE_SUB, tt), F32),
            pltpu.VMEM((2, CE_SUB, tt), BF16),
            pltpu.VMEM((d, tt), F32),
        ],
        compiler_params=_cparams(("parallel", "arbitrary")),
        name="peer_experts",
    )(h_bf16, st, sc.reshape(hp, SC_ROWS, t), u_bf16, vt_bf16, x2d, gfin.reshape(1, d))


def _peer(h_bf16, pq, sub_keys, u_bf16, vt_bf16, x2d, gfin, final_norm):
    st, sc = _peer_router(pq, sub_keys)
    return _peer_experts(h_bf16, st, sc, u_bf16, vt_bf16, x2d, gfin, final_norm)


def _glu_kernel(x_ref, g_ref, w_ref, b_ref, u_ref, *, d):
    h = _rms(x_ref[...], g_ref[...]).astype(BF16)
    a = jnp.dot(h, w_ref[...], preferred_element_type=F32) + b_ref[...]
    u_ref[...] = a[:, :d] * jax.nn.sigmoid(a[:, d:])


def _glu_proj(x2d, g, w_bf16, bias):
    t, d = x2d.shape
    tm = TM_PROJ
    return pl.pallas_call(
        functools.partial(_glu_kernel, d=d),
        grid=(t // tm,),
        in_specs=[
            pl.BlockSpec((tm, d), lambda i: (i, 0)),
            pl.BlockSpec((1, d), lambda i: (0, 0)),
            pl.BlockSpec((d, 2 * d), lambda i: (0, 0)),
            pl.BlockSpec((1, 2 * d), lambda i: (0, 0)),
        ],
        out_specs=pl.BlockSpec((tm, d), lambda i: (i, 0)),
        out_shape=jax.ShapeDtypeStruct((t, d), F32),
        compiler_params=_cparams(("parallel",)),
        name="glu_proj",
    )(x2d, g.reshape(1, d), w_bf16, bias.reshape(1, 2 * d))


def _conv_kernel(ucur_ref, uprev_ref, wdw_ref, bdw_ref, lng_ref, lnb_ref, w2_ref, b2_ref,
                 x_ref, gf_ref, wpq_ref, x_out_ref, h_ref, pq_ref, ext_scr, y_scr):
    tm, d = ucur_ref.shape
    first = pl.program_id(1) == 0
    ext_scr[0:CONV_HALO, :] = jnp.where(first, 0.0, uprev_ref[...])
    ext_scr[CONV_HALO:, :] = ucur_ref[...]
    lead = CONV_HALO - (CONV_WIDTH - 1)

    def chunk(ci, carry):
        r0 = pl.multiple_of(ci * CONV_ROWS, CONV_ROWS)
        window = ext_scr[pl.ds(r0, CONV_ROWS + CONV_HALO), :]
        acc = jnp.broadcast_to(bdw_ref[...], (CONV_ROWS, d))
        for w in range(CONV_WIDTH):
            acc = acc + wdw_ref[w:w + 1, :] * window[lead + w:lead + w + CONV_ROWS, :]
        y_scr[pl.ds(r0, CONV_ROWS), :] = acc
        return carry

    lax.fori_loop(0, tm // CONV_ROWS, chunk, 0)
    y = y_scr[...]
    mu = jnp.mean(y, axis=-1, keepdims=True)
    var = jnp.mean(jnp.square(y - mu), axis=-1, keepdims=True)
    y = (y - mu) * lax.rsqrt(var + EPS) * lng_ref[...] + lnb_ref[...]
    y = y * jax.nn.sigmoid(y)
    x_new = x_ref[...] + jnp.dot(y.astype(BF16), w2_ref[...], preferred_element_type=F32) + b2_ref[...]
    _peerq_epilogue(x_new, gf_ref, wpq_ref, x_out_ref, h_ref, pq_ref)


def _conv_tail(u3, wdw, bdw, lng, lnb, w2_bf16, b2, x3, gf, wpq_bf16):
    b, s, d = x3.shape
    nq = wpq_bf16.shape[1]
    tm = TM_CONV
    hpb = tm // CONV_HALO
    row = lambda v: v.reshape(1, d)
    tok = pl.BlockSpec((None, tm, d), lambda bi, i: (bi, i, 0))
    vec = pl.BlockSpec((1, d), lambda bi, i: (0, 0))
    return pl.pallas_call(
        _conv_kernel,
        grid=(b, s // tm),
        in_specs=[
            tok,
            pl.BlockSpec((None, CONV_HALO, d), lambda bi, i: (bi, jnp.maximum(i * hpb - 1, 0), 0)),
            pl.BlockSpec((CONV_WIDTH, d), lambda bi, i: (0, 0)),
            vec, vec, vec,
            pl.BlockSpec((d, d), lambda bi, i: (0, 0)),
            vec,
            tok,
            vec,
            pl.BlockSpec((d, nq), lambda bi, i: (0, 0)),
        ],
        out_specs=[
            tok,
            tok,
            pl.BlockSpec((None, tm, nq), lambda bi, i: (bi, i, 0)),
        ],
        out_shape=[
            jax.ShapeDtypeStruct((b, s, d), F32),
            jax.ShapeDtypeStruct((b, s, d), BF16),
            jax.ShapeDtypeStruct((b, s, nq), F32),
        ],
        scratch_shapes=[
            pltpu.VMEM((tm + CONV_HALO, d), F32),
            pltpu.VMEM((tm, d), F32),
        ],
        compiler_params=_cparams(("parallel", "arbitrary")),
        name="conv_tail",
    )(u3, u3, wdw, row(bdw), row(lng), row(lnb), w2_bf16, row(b2), x3, row(gf), wpq_bf16)


def kernel(x, rel_bias, norm_mix, norm_ffn, attn_w_qkv, attn_w_o, conv_w_pw1, conv_b_pw1,
           conv_w_dw, conv_b_dw, conv_ln_g, conv_ln_b, conv_w_pw2, conv_b_pw2, peer_w_q,
           peer_sub_keys, peer_u, peer_v, norm_final):
    b, s, d = x.shape
    t = b * s
    depth = norm_mix.shape[0]
    assert d == N_HEADS * HEAD_DIM and s % MOBA_BLOCK == 0 and s // MOBA_BLOCK <= LANES
    assert t % TT_ROUTER == 0 and s % TM_CONV == 0 and t % TM_PROJ == 0
    x2d = x.reshape(t, d)
    for i in range(depth):
        j = i // 2
        wpq = peer_w_q[i].astype(BF16)
        if i % 2 == 0:
            wq, wk, wv = (attn_w_qkv[j][:, c * d:(c + 1) * d] for c in range(3))
            qt, k, vt, kmean = _qkv_proj(x2d.reshape(b, s, d), norm_mix[i], wq.T.astype(BF16),
                                         wk.astype(BF16), wv.T.astype(BF16))
            o = _moba_attention(qt, k, vt, kmean, _moba_bias_tables(rel_bias))
            x2d, h, pq = _attn_out_proj(o.reshape(t, d), attn_w_o[j].astype(BF16), x2d,
                                        norm_ffn[i], wpq)
        else:
            u = _glu_proj(x2d, norm_mix[i], conv_w_pw1[j].astype(BF16), conv_b_pw1[j])
            x3, h3, pq3 = _conv_tail(u.reshape(b, s, d), conv_w_dw[j], conv_b_dw[j], conv_ln_g[j],
                                     conv_ln_b[j], conv_w_pw2[j].astype(BF16), conv_b_pw2[j],
                                     x2d.reshape(b, s, d), norm_ffn[i], wpq)
            x2d, h, pq = x3.reshape(t, d), h3.reshape(t, d), pq3.reshape(t, -1)
        last = i == depth - 1
        x2d = _peer(h, pq, peer_sub_keys[i], peer_u[i].astype(BF16),
                    peer_v[i].T.astype(BF16), x2d, norm_final, final_norm=last)
    return x2d.reshape(b, s, d)
```

```python
import functools
import math

import numpy as np
import jax
import jax.numpy as jnp
from jax import lax
from jax.experimental import pallas as pl
from jax.experimental.pallas import tpu as pltpu

F32 = jnp.float32
BF16 = jnp.bfloat16
EPS = 1e-6
NEG = -1e30

N_HEADS = 16
HEAD_DIM = 64
MOBA_BLOCK = 256
MOBA_TOPK = 3
REL_BUCKETS = 32
REL_MAX_DIST = 128
CONV_WIDTH = 31
PEER_HEADS = 8
PEER_NKEYS = 128
PEER_HALF = 128
PEER_TOPK = 16
SC_TOP1, SC_TOP2, SC_TAU, SC_ZINV, SC_ROWS = 0, PEER_TOPK, 2 * PEER_TOPK, 2 * PEER_TOPK + 1, 2 * PEER_TOPK + 2

LANES = 128
SUBLANES = 8
VMEM_LIMIT = 56 * 1024 * 1024

MOBA_QLANES = 128
TM_PROJ = 512
TT_ROUTER = 1024
TT_PEER = 512
CE_STEP = 2048
CE_SUB = 512
TM_CONV = 512
CONV_HALO = 32
CONV_ROWS = 16

NT_DIMS = (((1,), (1,)), ((), ()))


def _cparams(sem):
    return pltpu.CompilerParams(dimension_semantics=sem, vmem_limit_bytes=VMEM_LIMIT)


def _rms(x, g):
    return x * lax.rsqrt(jnp.mean(x * x, axis=-1, keepdims=True) + EPS) * g


def _qkv_kernel(x_ref, g_ref, wqt_ref, wk_ref, wvt_ref, qt_ref, k_ref, vt_ref, km_ref):
    h = _rms(x_ref[...], g_ref[...]).astype(BF16)
    scale = HEAD_DIM ** -0.5
    qt = lax.dot_general(wqt_ref[...], h, NT_DIMS, preferred_element_type=F32)
    qt_ref[...] = (qt * scale).astype(BF16)
    k = jnp.dot(h, wk_ref[...], preferred_element_type=F32)
    k_ref[...] = k.astype(BF16)
    vt = lax.dot_general(wvt_ref[...], h, NT_DIMS, preferred_element_type=F32)
    vt_ref[...] = vt.astype(BF16)
    nblk = k.shape[0] // MOBA_BLOCK
    km_ref[...] = jnp.mean(k.reshape(nblk, MOBA_BLOCK, k.shape[1]), axis=1)


def _qkv_proj(x3, g, wqt_bf16, wk_bf16, wvt_bf16):
    b, s, d = x3.shape
    tm = TM_PROJ
    nblk = tm // MOBA_BLOCK
    nb = s // MOBA_BLOCK
    wspec = pl.BlockSpec((d, d), lambda bi, i: (0, 0))
    tspec = pl.BlockSpec((None, d, tm), lambda bi, i: (bi, 0, i))
    return pl.pallas_call(
        _qkv_kernel,
        grid=(b, s // tm),
        in_specs=[
            pl.BlockSpec((None, tm, d), lambda bi, i: (bi, i, 0)),
            pl.BlockSpec((1, d), lambda bi, i: (0, 0)),
            wspec, wspec, wspec,
        ],
        out_specs=[
            tspec,
            pl.BlockSpec((None, tm, d), lambda bi, i: (bi, i, 0)),
            tspec,
            pl.BlockSpec((None, None, nblk, d), lambda bi, i: (bi, i, 0, 0)),
        ],
        out_shape=[
            jax.ShapeDtypeStruct((b, d, s), BF16),
            jax.ShapeDtypeStruct((b, s, d), BF16),
            jax.ShapeDtypeStruct((b, d, s), BF16),
            jax.ShapeDtypeStruct((b, s // tm, nblk, d), F32),
        ],
        compiler_params=_cparams(("parallel", "parallel")),
        name="qkv_proj",
    )(x3, g.reshape(1, d), wqt_bf16, wk_bf16, wvt_bf16)


def _moba_kernel(qt_ref, k_ref, vt_ref, km_ref, tbl_ref, o_ref, sel_scr, *, nb):
    blk = MOBA_BLOCK
    i = pl.program_id(2)
    jp = jnp.maximum(i - 1, 0)
    qt = qt_ref[...]
    frow = lax.broadcasted_iota(jnp.int32, qt.shape, 0)
    blk_id = lax.broadcasted_iota(jnp.int32, (nb, blk), 0)
    valid = blk_id < i
    km = km_ref[...]

    qh = []
    for hh in range(2):
        qh_t = jnp.where(frow // HEAD_DIM == hh, qt, jnp.zeros_like(qt))
        qh.append(qh_t)
        gate = jnp.dot(km, qh_t.astype(F32), precision=lax.Precision.HIGHEST,
                       preferred_element_type=F32)
        gate = jnp.where(valid, gate, NEG)
        g = gate
        for _ in range(MOBA_TOPK - 1):
            g = jnp.where(g >= jnp.max(g, axis=0, keepdims=True), NEG, g)
        tau = jnp.max(g, axis=0, keepdims=True)
        sel = jnp.where(valid, jnp.where(gate >= tau, 0.0, NEG), NEG)
        for qs in range(blk // MOBA_QLANES):
            sel_scr[hh, qs] = sel[:, qs * MOBA_QLANES:(qs + 1) * MOBA_QLANES]

    def k_block(j):
        return k_ref[pl.ds(pl.multiple_of(j * blk, blk), blk), :]

    chains = [(hh, qs) for hh in range(2) for qs in range(blk // MOBA_QLANES)]

    def qcols(qs):
        return slice(qs * MOBA_QLANES, (qs + 1) * MOBA_QLANES)

    def scores(kj, hh, qs):
        return jnp.dot(kj, qh[hh][:, qcols(qs)], preferred_element_type=F32)

    def sel_row(hh, qs, j):
        return sel_scr[hh, qs, pl.ds(j, 1), :]

    def pv(hh, p, j):
        vj = vt_ref[hh * HEAD_DIM:(hh + 1) * HEAD_DIM, pl.ds(pl.multiple_of(j * blk, blk), blk)]
        return jnp.dot(vj, p.astype(BF16), preferred_element_type=F32)

    def softmax_step(m_prev, l_prev, s):
        m_new = jnp.maximum(m_prev, jnp.max(s, axis=0, keepdims=True))
        alpha = jnp.exp(m_prev - m_new)
        p = jnp.exp(s - m_new)
        l_new = alpha * l_prev + jnp.sum(p, axis=0, keepdims=True)
        return m_new, l_new, alpha, p.astype(BF16)

    def block_scores(j):
        kj = k_block(j)
        return tuple(scores(kj, hh, qs) + sel_row(hh, qs, j) for hh, qs in chains)

    k_own, k_prev = k_block(i), k_block(jp)
    ss = [scores(k_own, hh, qs) + tbl_ref[hh, 0, :, qcols(qs)] for hh, qs in chains]
    ss_prev = [scores(k_prev, hh, qs) + tbl_ref[hh, 1, :, qcols(qs)] + sel_row(hh, qs, jp)
               for hh, qs in chains]
    ms = [jnp.max(s, axis=0, keepdims=True) for s in ss]
    ps = [jnp.exp(s - m) for s, m in zip(ss, ms)]
    ls = [jnp.sum(p, axis=0, keepdims=True) for p in ps]
    accs = [pv(hh, p, i) for (hh, _), p in zip(chains, ps)]
    steps = [softmax_step(m, l, s) for m, l, s in zip(ms, ls, ss_prev)]
    accs = [alpha * acc + pv(hh, p, jp) for (hh, _), acc, (_, _, alpha, p) in zip(chains, accs, steps)]
    ms = [st[0] for st in steps]
    ls = [st[1] for st in steps]

    def body(j, state):
        ss_cur, ms, ls, accs, ps_prev = state
        ss_next = block_scores(j + 1)
        pvs = [pv(hh, p, jnp.maximum(j - 1, 0)) for (hh, _), p in zip(chains, ps_prev)]
        steps = [softmax_step(m, l, s) for m, l, s in zip(ms, ls, ss_cur)]
        accs = [alpha * (acc + r) for acc, r, (_, _, alpha, _) in zip(accs, pvs, steps)]
        return (ss_next, tuple(st[0] for st in steps), tuple(st[1] for st in steps), tuple(accs),
                tuple(st[3] for st in steps))

    zero_p = tuple(jnp.zeros((blk, MOBA_QLANES), BF16) for _ in chains)
    state = (block_scores(0), tuple(ms), tuple(ls), tuple(accs), zero_p)
    _, ms, ls, accs, ps_last = lax.fori_loop(0, jp, body, state)
    accs = [acc + pv(hh, p, jnp.maximum(jp - 1, 0)) for (hh, _), acc, p in zip(chains, accs, ps_last)]
    outs = [acc / l for l, acc in zip(ls, accs)]
    nqs = blk // MOBA_QLANES
    o_t = jnp.concatenate([jnp.concatenate(outs[hh * nqs:(hh + 1) * nqs], axis=1) for hh in range(2)],
                          axis=0)
    o_ref[...] = o_t.T.astype(o_ref.dtype)


def _t5_bucket_np(dist):
    max_exact = REL_BUCKETS // 2
    d = np.maximum(dist, 0)
    df = np.maximum(d, 1).astype(np.float32)
    large = max_exact + (np.log(df / max_exact) / math.log(REL_MAX_DIST / max_exact)
                         * (REL_BUCKETS - max_exact)).astype(np.int32)
    large = np.minimum(large, REL_BUCKETS - 1)
    return np.where(d < max_exact, d, large)


def _moba_bias_tables(rel_bias):
    blk = MOBA_BLOCK
    qi = np.arange(blk)[None, :]
    ki = np.arange(blk)[:, None]
    assert int(_t5_bucket_np(np.array([blk + 1]))[0]) == REL_BUCKETS - 1
    b_own = _t5_bucket_np(qi - ki)
    b_prev = _t5_bucket_np(qi - ki + blk)
    far = rel_bias[:, REL_BUCKETS - 1][:, None, None]
    own = jnp.where((qi >= ki)[None], rel_bias[:, b_own] - far, NEG)
    prev = rel_bias[:, b_prev] - far
    return jnp.stack([own, prev], axis=1).astype(F32)


def _moba_attention(qt, k, vt, kmean, tbl):
    b, s, d = k.shape
    blk = MOBA_BLOCK
    nb = s // blk
    nhp = d // LANES
    return pl.pallas_call(
        functools.partial(_moba_kernel, nb=nb),
        grid=(b, nhp, nb),
        in_specs=[
            pl.BlockSpec((None, LANES, blk), lambda bi, hp, i: (bi, hp, i)),
            pl.BlockSpec((None, s, LANES), lambda bi, hp, i: (bi, 0, hp)),
            pl.BlockSpec((None, LANES, s), lambda bi, hp, i: (bi, hp, 0)),
            pl.BlockSpec((None, nb, LANES), lambda bi, hp, i: (bi, 0, hp)),
            pl.BlockSpec((2, 2, blk, blk), lambda bi, hp, i: (hp, 0, 0, 0)),
        ],
        out_specs=pl.BlockSpec((None, blk, LANES), lambda bi, hp, i: (bi, i, hp)),
        out_shape=jax.ShapeDtypeStruct((b, s, d), BF16),
        scratch_shapes=[pltpu.VMEM((2, blk // MOBA_QLANES, nb, MOBA_QLANES), F32)],
        compiler_params=_cparams(("parallel", "parallel", "arbitrary")),
        name="moba_attention",
    )(qt, k, vt, kmean.reshape(b, nb, d), tbl)


def _peerq_epilogue(x_new, gf_ref, wpq_ref, x_out_ref, h_ref, pq_ref):
    x_out_ref[...] = x_new
    h = _rms(x_new, gf_ref[...]).astype(BF16)
    h_ref[...] = h
    pq_ref[...] = jnp.dot(h, wpq_ref[...], preferred_element_type=F32)


def _attn_out_kernel(o_ref, wo_ref, x_ref, gf_ref, wpq_ref, x_out_ref, h_ref, pq_ref):
    x_new = x_ref[...] + jnp.dot(o_ref[...], wo_ref[...], preferred_element_type=F32)
    _peerq_epilogue(x_new, gf_ref, wpq_ref, x_out_ref, h_ref, pq_ref)


def _attn_out_proj(o2d, wo_bf16, x2d, gf, wpq_bf16):
    t, d = x2d.shape
    nq = wpq_bf16.shape[1]
    tm = TM_PROJ
    return pl.pallas_call(
        _attn_out_kernel,
        grid=(t // tm,),
        in_specs=[
            pl.BlockSpec((tm, d), lambda i: (i, 0)),
            pl.BlockSpec((d, d), lambda i: (0, 0)),
            pl.BlockSpec((tm, d), lambda i: (i, 0)),
            pl.BlockSpec((1, d), lambda i: (0, 0)),
            pl.BlockSpec((d, nq), lambda i: (0, 0)),
        ],
        out_specs=[
            pl.BlockSpec((tm, d), lambda i: (i, 0)),
            pl.BlockSpec((tm, d), lambda i: (i, 0)),
            pl.BlockSpec((tm, nq), lambda i: (i, 0)),
        ],
        out_shape=[
            jax.ShapeDtypeStruct((t, d), F32),
            jax.ShapeDtypeStruct((t, d), BF16),
            jax.ShapeDtypeStruct((t, nq), F32),
        ],
        compiler_params=_cparams(("parallel",)),
        name="attn_out_proj",
    )(o2d, wo_bf16, x2d, gf.reshape(1, d), wpq_bf16)


def _cmpx(vals, a, b):
    hi = jnp.maximum(vals[a], vals[b])
    lo = jnp.minimum(vals[a], vals[b])
    vals[a], vals[b] = hi, lo


def _bitonic_merge_desc(vals):
    n = len(vals)
    dist = n // 2
    while dist >= 1:
        for s in range(0, n, 2 * dist):
            for t in range(s, s + dist):
                _cmpx(vals, t, t + dist)
        dist //= 2
    return vals


def _sort_desc(vals):
    n = len(vals)
    if n == 1:
        return vals
    top = _sort_desc(vals[: n // 2])
    bot = _sort_desc(vals[n // 2:])
    return _bitonic_merge_desc(top + bot[::-1])


def _top_merge(a, b):
    n = len(a)
    return _bitonic_merge_desc([jnp.maximum(a[r], b[n - 1 - r]) for r in range(n)])


def _sorted_top16(vals):
    groups = [_sort_desc(vals[g:g + PEER_TOPK]) for g in range(0, len(vals), PEER_TOPK)]
    while len(groups) > 1:
        groups = [_top_merge(groups[g], groups[g + 1]) for g in range(0, len(groups), 2)]
    return groups[0]


def _router_kernel(pq_ref, keys_ref, st_ref, sc_ref, scr_ref):
    nch = TT_ROUTER // LANES
    tops = []
    for c in range(2):
        qc = pq_ref[:, c * PEER_HALF:(c + 1) * PEER_HALF]
        s_t = lax.dot_general(keys_ref[0, c], qc, NT_DIMS,
                              precision=lax.Precision.HIGHEST,
                              preferred_element_type=F32)
        st_ref[0, c] = s_t
        for ch in range(nch):
            scr_ref[c, pl.ds(ch, PEER_NKEYS, stride=nch), :] = s_t[:, ch * LANES:(ch + 1) * LANES]
        vals = [scr_ref[c, k * nch:(k + 1) * nch, :] for k in range(PEER_NKEYS)]
        tops.append(_sorted_top16(vals))
    a, b = tops
    cands = [a[r] + b[c] for r in range(PEER_TOPK) for c in range(PEER_TOPK)
             if (r + 1) * (c + 1) <= PEER_TOPK]
    npad = 1 << (len(cands) - 1).bit_length()
    cands = cands + [jnp.full_like(cands[0], NEG)] * (npad - len(cands))
    best = _sort_desc(cands)[:PEER_TOPK]
    z = jnp.ones_like(best[0])
    for r in range(1, PEER_TOPK):
        z = z + jnp.exp(best[r] - best[0])
    for r in range(PEER_TOPK):
        sc_ref[0, SC_TOP1 + r] = a[r]
        sc_ref[0, SC_TOP2 + r] = b[r]
    sc_ref[0, SC_TAU] = best[PEER_TOPK - 1]
    sc_ref[0, SC_ZINV] = 1.0 / z


def _peer_router(pq, sub_keys):
    t = pq.shape[0]
    tt = TT_ROUTER
    nch = tt // LANES
    hp = PEER_HEADS
    return pl.pallas_call(
        _router_kernel,
        grid=(t // tt, hp),
        in_specs=[
            pl.BlockSpec((tt, 2 * PEER_HALF), lambda i, h: (i, h)),
            pl.BlockSpec((1, 2, PEER_NKEYS, PEER_HALF), lambda i, h: (h, 0, 0, 0)),
        ],
        out_specs=[
            pl.BlockSpec((1, 2, PEER_NKEYS, tt), lambda i, h: (h, 0, 0, i)),
            pl.BlockSpec((1, SC_ROWS, nch, LANES), lambda i, h: (h, 0, i, 0)),
        ],
        out_shape=[
            jax.ShapeDtypeStruct((hp, 2, PEER_NKEYS, t), F32),
            jax.ShapeDtypeStruct((hp, SC_ROWS, t // LANES, LANES), F32),
        ],
        scratch_shapes=[pltpu.VMEM((2, PEER_NKEYS * nch, LANES), F32)],
        compiler_params=_cparams(("parallel", "parallel")),
        name="peer_router",
    )(pq, sub_keys)


def _gelu2(x):
    c0 = math.sqrt(2.0 / math.pi)
    z = x * (c0 + (c0 * 0.044715) * (x * x))
    return x + x * jnp.tanh(z)


def _dup_bf16_bits(v):
    bits = pltpu.bitcast(v.astype(BF16).astype(F32), jnp.uint32)
    return bits | (bits >> 16)


def _peer_kernel(h_ref, st_ref, sc_ref, u_ref, vt_ref, x_ref, gfin_ref, o_ref,
                 rank_scr, bw_scr, cnt_scr, aw_scr, act_scr, wa_scr, acc_scr, *, final_norm):
    e = pl.program_id(1)
    tt = h_ref.shape[0]
    nsub = CE_STEP // CE_SUB
    ipc = CE_SUB // PEER_NKEYS
    pack = 2 * SUBLANES

    @pl.when(e == 0)
    def _():
        acc_scr[...] = jnp.zeros_like(acc_scr)
        for hd in range(PEER_HEADS):
            s1 = st_ref[hd, 0]
            s2 = st_ref[hd, 1]
            tau = sc_ref[hd, SC_TAU:SC_TAU + 1, :]
            rank2 = jnp.zeros_like(s2)
            cnt = jnp.zeros_like(s1)
            for c in range(PEER_TOPK):
                b_c = sc_ref[hd, SC_TOP2 + c:SC_TOP2 + c + 1, :]
                rank2 = jnp.where(b_c > s2, c + 1.0, rank2)
                cnt = jnp.where(s1 + b_c >= tau, c + 1.0, cnt)
            a0 = sc_ref[hd, SC_TOP1:SC_TOP1 + 1, :]
            b0 = sc_ref[hd, SC_TOP2:SC_TOP2 + 1, :]
            zinv = sc_ref[hd, SC_ZINV:SC_ZINV + 1, :]
            rank_scr[hd] = rank2.astype(BF16)
            bw_scr[hd] = (jnp.exp(s2 - b0) * (0.5 * zinv)).astype(BF16)
            cnt_w = _dup_bf16_bits(cnt)
            a_w = _dup_bf16_bits(jnp.exp(s1 - a0))
            for lg in range(tt // LANES):
                cnt_scr[hd, lg] = cnt_w[:, lg * LANES:(lg + 1) * LANES]
                aw_scr[hd, lg] = a_w[:, lg * LANES:(lg + 1) * LANES]

    def row_bf16(tile, r):
        words = jnp.broadcast_to(tile[r:r + 1, :], (SUBLANES, LANES))
        return pltpu.bitcast(words, BF16)

    def up_proj(c):
        act_scr[c % 2] = lax.dot_general(u_ref[c * CE_SUB:(c + 1) * CE_SUB, :], h_ref[...], NT_DIMS,
                                         preferred_element_type=F32)

    def down_proj(c):
        acc_scr[...] += jnp.dot(vt_ref[:, c * CE_SUB:(c + 1) * CE_SUB], wa_scr[c % 2],
                                preferred_element_type=F32)

    def gate(c):
        njg = PEER_NKEYS // pack
        zero = jnp.zeros((pack, LANES), BF16)
        for lg in range(tt // LANES):
            lanes = slice(lg * LANES, (lg + 1) * LANES)
            w = [[None] * njg for _ in range(ipc)]
            for hd in range(PEER_HEADS):
                rank = [rank_scr[hd, jg * pack:(jg + 1) * pack, lanes] for jg in range(njg)]
                bw = [bw_scr[hd, jg * pack:(jg + 1) * pack, lanes] for jg in range(njg)]
                first = (e * nsub + c) * ipc
                base = pl.multiple_of((first // SUBLANES) * SUBLANES, SUBLANES)
                off = (c * ipc) % SUBLANES
                cnt_tile = cnt_scr[hd, lg, pl.ds(base, SUBLANES), :]
                a_tile = aw_scr[hd, lg, pl.ds(base, SUBLANES), :]
                for il in range(ipc):
                    cnt_row = row_bf16(cnt_tile, off + il)
                    a_row = row_bf16(a_tile, off + il)
                    for jg in range(njg):
                        term = jnp.where(rank[jg] < cnt_row, a_row * bw[jg], zero)
                        w[il][jg] = term if w[il][jg] is None else w[il][jg] + term
            for il in range(ipc):
                for jg in range(njg):
                    rows = slice(il * PEER_NKEYS + jg * pack, il * PEER_NKEYS + (jg + 1) * pack)
                    wa_scr[c % 2, rows, lanes] = w[il][jg] * _gelu2(act_scr[c % 2, rows, lanes]).astype(BF16)

    up_proj(0)
    for c in range(nsub):
        if c + 1 < nsub:
            up_proj(c + 1)
        if c >= 1:
            down_proj(c - 1)
        gate(c)
    down_proj(nsub - 1)

    @pl.when(e == pl.num_programs(1) - 1)
    def _():
        y = x_ref[...] + acc_scr[...].T
        if final_norm:
            y = _rms(y, gfin_ref[...])
        o_ref[...] = y


def _peer_experts(h_bf16, st, sc, u_bf16, vt_bf16, x2d, gfin, final_norm):
    t, d = x2d.shape
    ne = u_bf16.shape[0]
    tt, ce = TT_PEER, CE_STEP
    hp = PEER_HEADS
    return pl.pallas_call(
        functools.partial(_peer_kernel, final_norm=final_norm),
        grid=(t // tt, ne // ce),
        in_specs=[
            pl.BlockSpec((tt, d), lambda i, e: (i, 0)),
            pl.BlockSpec((hp, 2, PEER_NKEYS, tt), lambda i, e: (0, 0, 0, i)),
            pl.BlockSpec((hp, SC_ROWS, tt), lambda i, e: (0, 0, i)),
            pl.BlockSpec((ce, d), lambda i, e: (e, 0)),
            pl.BlockSpec((None, d, ce), lambda i, e: (e, 0, 0)),
            pl.BlockSpec((tt, d), lambda i, e: (i, 0)),
            pl.BlockSpec((1, d), lambda i, e: (0, 0)),
        ],
        out_specs=pl.BlockSpec((tt, d), lambda i, e: (i, 0)),
        out_shape=jax.ShapeDtypeStruct((t, d), F32),
        scratch_shapes=[
            pltpu.VMEM((hp, PEER_NKEYS, tt), BF16),
            pltpu.VMEM((hp, PEER_NKEYS, tt), BF16),
            pltpu.VMEM((hp, tt // LANES, PEER_NKEYS, LANES), jnp.uint32),
            pltpu.VMEM((hp, tt // LANES, PEER_NKEYS, LANES), jnp.uint32),
            pltpu.VMEM((2, CE_SUB, tt), F32),
            pltpu.VMEM((2, CE_SUB, tt), BF16),
            pltpu.VMEM((d, tt), F32),
        ],
        compiler_params=_cparams(("parallel", "arbitrary")),
        name="peer_experts",
    )(h_bf16, st, sc.reshape(hp, SC_ROWS, t), u_bf16, vt_bf16, x2d, gfin.reshape(1, d))


def _expert_major_t(v):
    ne, d = v.shape
    return v.astype(BF16).reshape(ne // CE_STEP, CE_STEP, d).transpose(0, 2, 1)


def _peer(h_bf16, pq, sub_keys, u_bf16, vt_bf16, x2d, gfin, final_norm):
    st, sc = _peer_router(pq, sub_keys)
    return _peer_experts(h_bf16, st, sc, u_bf16, vt_bf16, x2d, gfin, final_norm)


def _glu_kernel(x_ref, g_ref, w_ref, b_ref, u_ref, *, d):
    h = _rms(x_ref[...], g_ref[...]).astype(BF16)
    a = jnp.dot(h, w_ref[...], preferred_element_type=F32) + b_ref[...]
    u_ref[...] = a[:, :d] * jax.nn.sigmoid(a[:, d:])


def _glu_proj(x2d, g, w_bf16, bias):
    t, d = x2d.shape
    tm = TM_PROJ
    return pl.pallas_call(
        functools.partial(_glu_kernel, d=d),
        grid=(t // tm,),
        in_specs=[
            pl.BlockSpec((tm, d), lambda i: (i, 0)),
            pl.BlockSpec((1, d), lambda i: (0, 0)),
            pl.BlockSpec((d, 2 * d), lambda i: (0, 0)),
            pl.BlockSpec((1, 2 * d), lambda i: (0, 0)),
        ],
        out_specs=pl.BlockSpec((tm, d), lambda i: (i, 0)),
        out_shape=jax.ShapeDtypeStruct((t, d), F32),
        compiler_params=_cparams(("parallel",)),
        name="glu_proj",
    )(x2d, g.reshape(1, d), w_bf16, bias.reshape(1, 2 * d))


def _conv_kernel(ucur_ref, uprev_ref, wdw_ref, bdw_ref, lng_ref, lnb_ref, w2_ref, b2_ref,
                 x_ref, gf_ref, wpq_ref, x_out_ref, h_ref, pq_ref, ext_scr, y_scr):
    tm, d = ucur_ref.shape
    first = pl.program_id(1) == 0
    ext_scr[0:CONV_HALO, :] = jnp.where(first, 0.0, uprev_ref[...])
    ext_scr[CONV_HALO:, :] = ucur_ref[...]
    lead = CONV_HALO - (CONV_WIDTH - 1)

    def chunk(ci, carry):
        r0 = pl.multiple_of(ci * CONV_ROWS, CONV_ROWS)
        window = ext_scr[pl.ds(r0, CONV_ROWS + CONV_HALO), :]
        acc = jnp.broadcast_to(bdw_ref[...], (CONV_ROWS, d))
        for w in range(CONV_WIDTH):
            acc = acc + wdw_ref[w:w + 1, :] * window[lead + w:lead + w + CONV_ROWS, :]
        y_scr[pl.ds(r0, CONV_ROWS), :] = acc
        return carry

    lax.fori_loop(0, tm // CONV_ROWS, chunk, 0)
    y = y_scr[...]
    mu = jnp.mean(y, axis=-1, keepdims=True)
    var = jnp.mean(jnp.square(y - mu), axis=-1, keepdims=True)
    y = (y - mu) * lax.rsqrt(var + EPS) * lng_ref[...] + lnb_ref[...]
    y = y * jax.nn.sigmoid(y)
    x_new = x_ref[...] + jnp.dot(y.astype(BF16), w2_ref[...], preferred_element_type=F32) + b2_ref[...]
    _peerq_epilogue(x_new, gf_ref, wpq_ref, x_out_ref, h_ref, pq_ref)


def _conv_tail(u3, wdw, bdw, lng, lnb, w2_bf16, b2, x3, gf, wpq_bf16):
    b, s, d = x3.shape
    nq = wpq_bf16.shape[1]
    tm = TM_CONV
    hpb = tm // CONV_HALO
    row = lambda v: v.reshape(1, d)
    tok = pl.BlockSpec((None, tm, d), lambda bi, i: (bi, i, 0))
    vec = pl.BlockSpec((1, d), lambda bi, i: (0, 0))
    return pl.pallas_call(
        _conv_kernel,
        grid=(b, s // tm),
        in_specs=[
            tok,
            pl.BlockSpec((None, CONV_HALO, d), lambda bi, i: (bi, jnp.maximum(i * hpb - 1, 0), 0)),
            pl.BlockSpec((CONV_WIDTH, d), lambda bi, i: (0, 0)),
            vec, vec, vec,
            pl.BlockSpec((d, d), lambda bi, i: (0, 0)),
            vec,
            tok,
            vec,
            pl.BlockSpec((d, nq), lambda bi, i: (0, 0)),
        ],
        out_specs=[
            tok,
            tok,
            pl.BlockSpec((None, tm, nq), lambda bi, i: (bi, i, 0)),
        ],
        out_shape=[
            jax.ShapeDtypeStruct((b, s, d), F32),
            jax.ShapeDtypeStruct((b, s, d), BF16),
            jax.ShapeDtypeStruct((b, s, nq), F32),
        ],
        scratch_shapes=[
            pltpu.VMEM((tm + CONV_HALO, d), F32),
            pltpu.VMEM((tm, d), F32),
        ],
        compiler_params=_cparams(("parallel", "arbitrary")),
        name="conv_tail",
    )(u3, u3, wdw, row(bdw), row(lng), row(lnb), w2_bf16, row(b2), x3, row(gf), wpq_bf16)


def kernel(x, rel_bias, norm_mix, norm_ffn, attn_w_qkv, attn_w_o, conv_w_pw1, conv_b_pw1,
           conv_w_dw, conv_b_dw, conv_ln_g, conv_ln_b, conv_w_pw2, conv_b_pw2, peer_w_q,
           peer_sub_keys, peer_u, peer_v, norm_final):
    b, s, d = x.shape
    t = b * s
    depth = norm_mix.shape[0]
    assert d == N_HEADS * HEAD_DIM and s % MOBA_BLOCK == 0 and s // MOBA_BLOCK <= LANES
    assert t % TT_ROUTER == 0 and s % TM_CONV == 0 and t % TM_PROJ == 0
    x2d = x.reshape(t, d)
    for i in range(depth):
        j = i // 2
        wpq = peer_w_q[i].astype(BF16)
        if i % 2 == 0:
            wq, wk, wv = (attn_w_qkv[j][:, c * d:(c + 1) * d] for c in range(3))
            qt, k, vt, kmean = _qkv_proj(x2d.reshape(b, s, d), norm_mix[i], wq.T.astype(BF16),
                                         wk.astype(BF16), wv.T.astype(BF16))
            o = _moba_attention(qt, k, vt, kmean, _moba_bias_tables(rel_bias))
            x2d, h, pq = _attn_out_proj(o.reshape(t, d), attn_w_o[j].astype(BF16), x2d,
                                        norm_ffn[i], wpq)
        else:
            u = _glu_proj(x2d, norm_mix[i], conv_w_pw1[j].astype(BF16), conv_b_pw1[j])
            x3, h3, pq3 = _conv_tail(u.reshape(b, s, d), conv_w_dw[j], conv_b_dw[j], conv_ln_g[j],
                                     conv_ln_b[j], conv_w_pw2[j].astype(BF16), conv_b_pw2[j],
                                     x2d.reshape(b, s, d), norm_ffn[i], wpq)
            x2d, h, pq = x3.reshape(t, d), h3.reshape(t, d), pq3.reshape(t, -1)
        last = i == depth - 1
        x2d = _peer(h, pq, peer_sub_keys[i], peer_u[i].astype(BF16),
                    _expert_major_t(peer_v[i]), x2d, norm_final, final_norm=last)
    return x2d.reshape(b, s, d)
```

```python
import functools
import math

import numpy as np
import jax
import jax.numpy as jnp
from jax import lax
from jax.experimental import pallas as pl
from jax.experimental.pallas import tpu as pltpu

F32 = jnp.float32
BF16 = jnp.bfloat16
EPS = 1e-6
NEG = -1e30

N_HEADS = 16
HEAD_DIM = 64
MOBA_BLOCK = 256
MOBA_TOPK = 3
REL_BUCKETS = 32
REL_MAX_DIST = 128
CONV_WIDTH = 31
PEER_HEADS = 8
PEER_NKEYS = 128
PEER_HALF = 128
PEER_TOPK = 16
SC_TOP1, SC_TOP2, SC_TAU, SC_ZINV, SC_ROWS = 0, PEER_TOPK, 2 * PEER_TOPK, 2 * PEER_TOPK + 1, 2 * PEER_TOPK + 2

LANES = 128
SUBLANES = 8
VMEM_LIMIT = 60 * 1024 * 1024

MOBA_QLANES = 128
TM_PROJ = 512
TT_ROUTER = 1024
TT_PEER = 1024
CE_STEP = 1024
CE_SUB = 512
TM_CONV = 512
CONV_HALO = 32
CONV_ROWS = 16

NT_DIMS = (((1,), (1,)), ((), ()))


def _cparams(sem):
    return pltpu.CompilerParams(dimension_semantics=sem, vmem_limit_bytes=VMEM_LIMIT)


def _rms(x, g):
    return x * lax.rsqrt(jnp.mean(x * x, axis=-1, keepdims=True) + EPS) * g


def _qkv_kernel(x_ref, g_ref, wqt_ref, wk_ref, wvt_ref, qt_ref, k_ref, vt_ref, km_ref):
    h = _rms(x_ref[...], g_ref[...]).astype(BF16)
    scale = HEAD_DIM ** -0.5
    qt = lax.dot_general(wqt_ref[...], h, NT_DIMS, preferred_element_type=F32)
    qt_ref[...] = (qt * scale).astype(BF16)
    k = jnp.dot(h, wk_ref[...], preferred_element_type=F32)
    k_ref[...] = k.astype(BF16)
    vt = lax.dot_general(wvt_ref[...], h, NT_DIMS, preferred_element_type=F32)
    vt_ref[...] = vt.astype(BF16)
    nblk = k.shape[0] // MOBA_BLOCK
    km_ref[...] = jnp.mean(k.reshape(nblk, MOBA_BLOCK, k.shape[1]), axis=1)


def _qkv_proj(x3, g, wqt_bf16, wk_bf16, wvt_bf16):
    b, s, d = x3.shape
    tm = TM_PROJ
    nblk = tm // MOBA_BLOCK
    nb = s // MOBA_BLOCK
    wspec = pl.BlockSpec((d, d), lambda bi, i: (0, 0))
    tspec = pl.BlockSpec((None, d, tm), lambda bi, i: (bi, 0, i))
    return pl.pallas_call(
        _qkv_kernel,
        grid=(b, s // tm),
        in_specs=[
            pl.BlockSpec((None, tm, d), lambda bi, i: (bi, i, 0)),
            pl.BlockSpec((1, d), lambda bi, i: (0, 0)),
            wspec, wspec, wspec,
        ],
        out_specs=[
            tspec,
            pl.BlockSpec((None, tm, d), lambda bi, i: (bi, i, 0)),
            tspec,
            pl.BlockSpec((None, None, nblk, d), lambda bi, i: (bi, i, 0, 0)),
        ],
        out_shape=[
            jax.ShapeDtypeStruct((b, d, s), BF16),
            jax.ShapeDtypeStruct((b, s, d), BF16),
            jax.ShapeDtypeStruct((b, d, s), BF16),
            jax.ShapeDtypeStruct((b, s // tm, nblk, d), F32),
        ],
        compiler_params=_cparams(("parallel", "parallel")),
        name="qkv_proj",
    )(x3, g.reshape(1, d), wqt_bf16, wk_bf16, wvt_bf16)


def _moba_kernel(qt_ref, k_ref, vt_ref, km_ref, tbl_ref, o_ref, sel_scr, *, nb):
    blk = MOBA_BLOCK
    i = pl.program_id(2)
    jp = jnp.maximum(i - 1, 0)
    qt = qt_ref[...]
    frow = lax.broadcasted_iota(jnp.int32, qt.shape, 0)
    blk_id = lax.broadcasted_iota(jnp.int32, (nb, blk), 0)
    valid = blk_id < i
    km = km_ref[...]

    qh = []
    for hh in range(2):
        qh_t = jnp.where(frow // HEAD_DIM == hh, qt, jnp.zeros_like(qt))
        qh.append(qh_t)
        gate = jnp.dot(km, qh_t.astype(F32), precision=lax.Precision.HIGHEST,
                       preferred_element_type=F32)
        gate = jnp.where(valid, gate, NEG)
        g = gate
        for _ in range(MOBA_TOPK - 1):
            g = jnp.where(g >= jnp.max(g, axis=0, keepdims=True), NEG, g)
        tau = jnp.max(g, axis=0, keepdims=True)
        sel = jnp.where(valid, jnp.where(gate >= tau, 0.0, NEG), NEG)
        for qs in range(blk // MOBA_QLANES):
            sel_scr[hh, qs] = sel[:, qs * MOBA_QLANES:(qs + 1) * MOBA_QLANES]

    def k_block(j):
        return k_ref[pl.ds(pl.multiple_of(j * blk, blk), blk), :]

    chains = [(hh, qs) for hh in range(2) for qs in range(blk // MOBA_QLANES)]

    def qcols(qs):
        return slice(qs * MOBA_QLANES, (qs + 1) * MOBA_QLANES)

    def scores(kj, hh, qs):
        return jnp.dot(kj, qh[hh][:, qcols(qs)], preferred_element_type=F32)

    def sel_row(hh, qs, j):
        return sel_scr[hh, qs, pl.ds(j, 1), :]

    def pv(hh, p, j):
        vj = vt_ref[hh * HEAD_DIM:(hh + 1) * HEAD_DIM, pl.ds(pl.multiple_of(j * blk, blk), blk)]
        return jnp.dot(vj, p.astype(BF16), preferred_element_type=F32)

    def softmax_step(m_prev, l_prev, s):
        m_new = jnp.maximum(m_prev, jnp.max(s, axis=0, keepdims=True))
        alpha = jnp.exp(m_prev - m_new)
        p = jnp.exp(s - m_new)
        l_new = alpha * l_prev + jnp.sum(p, axis=0, keepdims=True)
        return m_new, l_new, alpha, p.astype(BF16)

    def block_scores(j):
        kj = k_block(j)
        return tuple(scores(kj, hh, qs) + sel_row(hh, qs, j) for hh, qs in chains)

    k_own, k_prev = k_block(i), k_block(jp)
    ss = [scores(k_own, hh, qs) + tbl_ref[hh, 0, :, qcols(qs)] for hh, qs in chains]
    ss_prev = [scores(k_prev, hh, qs) + tbl_ref[hh, 1, :, qcols(qs)] + sel_row(hh, qs, jp)
               for hh, qs in chains]
    ms = [jnp.max(s, axis=0, keepdims=True) for s in ss]
    ps = [jnp.exp(s - m) for s, m in zip(ss, ms)]
    ls = [jnp.sum(p, axis=0, keepdims=True) for p in ps]
    accs = [pv(hh, p, i) for (hh, _), p in zip(chains, ps)]
    steps = [softmax_step(m, l, s) for m, l, s in zip(ms, ls, ss_prev)]
    accs = [alpha * acc + pv(hh, p, jp) for (hh, _), acc, (_, _, alpha, p) in zip(chains, accs, steps)]
    ms = [st[0] for st in steps]
    ls = [st[1] for st in steps]

    def body(j, state):
        ss_cur, ms, ls, accs, ps_prev = state
        ss_next = block_scores(j + 1)
        pvs = [pv(hh, p, jnp.maximum(j - 1, 0)) for (hh, _), p in zip(chains, ps_prev)]
        steps = [softmax_step(m, l, s) for m, l, s in zip(ms, ls, ss_cur)]
        accs = [alpha * (acc + r) for acc, r, (_, _, alpha, _) in zip(accs, pvs, steps)]
        return (ss_next, tuple(st[0] for st in steps), tuple(st[1] for st in steps), tuple(accs),
                tuple(st[3] for st in steps))

    zero_p = tuple(jnp.zeros((blk, MOBA_QLANES), BF16) for _ in chains)
    state = (block_scores(0), tuple(ms), tuple(ls), tuple(accs), zero_p)
    _, ms, ls, accs, ps_last = lax.fori_loop(0, jp, body, state)
    accs = [acc + pv(hh, p, jnp.maximum(jp - 1, 0)) for (hh, _), acc, p in zip(chains, accs, ps_last)]
    outs = [acc / l for l, acc in zip(ls, accs)]
    nqs = blk // MOBA_QLANES
    o_t = jnp.concatenate([jnp.concatenate(outs[hh * nqs:(hh + 1) * nqs], axis=1) for hh in range(2)],
                          axis=0)
    o_ref[...] = o_t.T.astype(o_ref.dtype)


def _t5_bucket_np(dist):
    max_exact = REL_BUCKETS // 2
    d = np.maximum(dist, 0)
    df = np.maximum(d, 1).astype(np.float32)
    large = max_exact + (np.log(df / max_exact) / math.log(REL_MAX_DIST / max_exact)
                         * (REL_BUCKETS - max_exact)).astype(np.int32)
    large = np.minimum(large, REL_BUCKETS - 1)
    return np.where(d < max_exact, d, large)


def _moba_bias_tables(rel_bias):
    blk = MOBA_BLOCK
    qi = np.arange(blk)[None, :]
    ki = np.arange(blk)[:, None]
    assert int(_t5_bucket_np(np.array([blk + 1]))[0]) == REL_BUCKETS - 1
    b_own = _t5_bucket_np(qi - ki)
    b_prev = _t5_bucket_np(qi - ki + blk)
    far = rel_bias[:, REL_BUCKETS - 1][:, None, None]
    own = jnp.where((qi >= ki)[None], rel_bias[:, b_own] - far, NEG)
    prev = rel_bias[:, b_prev] - far
    return jnp.stack([own, prev], axis=1).astype(F32)


def _moba_attention(qt, k, vt, kmean, tbl):
    b, s, d = k.shape
    blk = MOBA_BLOCK
    nb = s // blk
    nhp = d // LANES
    return pl.pallas_call(
        functools.partial(_moba_kernel, nb=nb),
        grid=(b, nhp, nb),
        in_specs=[
            pl.BlockSpec((None, LANES, blk), lambda bi, hp, i: (bi, hp, i)),
            pl.BlockSpec((None, s, LANES), lambda bi, hp, i: (bi, 0, hp)),
            pl.BlockSpec((None, LANES, s), lambda bi, hp, i: (bi, hp, 0)),
            pl.BlockSpec((None, nb, LANES), lambda bi, hp, i: (bi, 0, hp)),
            pl.BlockSpec((2, 2, blk, blk), lambda bi, hp, i: (hp, 0, 0, 0)),
        ],
        out_specs=pl.BlockSpec((None, blk, LANES), lambda bi, hp, i: (bi, i, hp)),
        out_shape=jax.ShapeDtypeStruct((b, s, d), BF16),
        scratch_shapes=[pltpu.VMEM((2, blk // MOBA_QLANES, nb, MOBA_QLANES), F32)],
        compiler_params=_cparams(("parallel", "parallel", "arbitrary")),
        name="moba_attention",
    )(qt, k, vt, kmean.reshape(b, nb, d), tbl)


def _peerq_epilogue(x_new, gf_ref, wpq_ref, x_out_ref, h_ref, pq_ref):
    x_out_ref[...] = x_new
    h = _rms(x_new, gf_ref[...]).astype(BF16)
    h_ref[...] = h
    pq_ref[...] = jnp.dot(h, wpq_ref[...], preferred_element_type=F32)


def _attn_out_kernel(o_ref, wo_ref, x_ref, gf_ref, wpq_ref, x_out_ref, h_ref, pq_ref):
    x_new = x_ref[...] + jnp.dot(o_ref[...], wo_ref[...], preferred_element_type=F32)
    _peerq_epilogue(x_new, gf_ref, wpq_ref, x_out_ref, h_ref, pq_ref)


def _attn_out_proj(o2d, wo_bf16, x2d, gf, wpq_bf16):
    t, d = x2d.shape
    nq = wpq_bf16.shape[1]
    tm = TM_PROJ
    return pl.pallas_call(
        _attn_out_kernel,
        grid=(t // tm,),
        in_specs=[
            pl.BlockSpec((tm, d), lambda i: (i, 0)),
            pl.BlockSpec((d, d), lambda i: (0, 0)),
            pl.BlockSpec((tm, d), lambda i: (i, 0)),
            pl.BlockSpec((1, d), lambda i: (0, 0)),
            pl.BlockSpec((d, nq), lambda i: (0, 0)),
        ],
        out_specs=[
            pl.BlockSpec((tm, d), lambda i: (i, 0)),
            pl.BlockSpec((tm, d), lambda i: (i, 0)),
            pl.BlockSpec((tm, nq), lambda i: (i, 0)),
        ],
        out_shape=[
            jax.ShapeDtypeStruct((t, d), F32),
            jax.ShapeDtypeStruct((t, d), BF16),
            jax.ShapeDtypeStruct((t, nq), F32),
        ],
        compiler_params=_cparams(("parallel",)),
        name="attn_out_proj",
    )(o2d, wo_bf16, x2d, gf.reshape(1, d), wpq_bf16)


def _cmpx(vals, a, b):
    hi = jnp.maximum(vals[a], vals[b])
    lo = jnp.minimum(vals[a], vals[b])
    vals[a], vals[b] = hi, lo


def _bitonic_merge_desc(vals):
    n = len(vals)
    dist = n // 2
    while dist >= 1:
        for s in range(0, n, 2 * dist):
            for t in range(s, s + dist):
                _cmpx(vals, t, t + dist)
        dist //= 2
    return vals


def _sort_desc(vals):
    n = len(vals)
    if n == 1:
        return vals
    top = _sort_desc(vals[: n // 2])
    bot = _sort_desc(vals[n // 2:])
    return _bitonic_merge_desc(top + bot[::-1])


def _top_merge(a, b):
    n = len(a)
    return _bitonic_merge_desc([jnp.maximum(a[r], b[n - 1 - r]) for r in range(n)])


def _sorted_top16(vals):
    groups = [_sort_desc(vals[g:g + PEER_TOPK]) for g in range(0, len(vals), PEER_TOPK)]
    while len(groups) > 1:
        groups = [_top_merge(groups[g], groups[g + 1]) for g in range(0, len(groups), 2)]
    return groups[0]


def _router_kernel(pq_ref, keys_ref, st_ref, sc_ref, scr_ref):
    nch = TT_ROUTER // LANES
    tops = []
    for c in range(2):
        qc = pq_ref[:, c * PEER_HALF:(c + 1) * PEER_HALF]
        s_t = lax.dot_general(keys_ref[0, c], qc, NT_DIMS,
                              precision=lax.Precision.HIGHEST,
                              preferred_element_type=F32)
        st_ref[0, c] = s_t
        for ch in range(nch):
            scr_ref[c, pl.ds(ch, PEER_NKEYS, stride=nch), :] = s_t[:, ch * LANES:(ch + 1) * LANES]
        vals = [scr_ref[c, k * nch:(k + 1) * nch, :] for k in range(PEER_NKEYS)]
        tops.append(_sorted_top16(vals))
    a, b = tops
    cands = [a[r] + b[c] for r in range(PEER_TOPK) for c in range(PEER_TOPK)
             if (r + 1) * (c + 1) <= PEER_TOPK]
    npad = 1 << (len(cands) - 1).bit_length()
    cands = cands + [jnp.full_like(cands[0], NEG)] * (npad - len(cands))
    best = _sort_desc(cands)[:PEER_TOPK]
    z = jnp.ones_like(best[0])
    for r in range(1, PEER_TOPK):
        z = z + jnp.exp(best[r] - best[0])
    for r in range(PEER_TOPK):
        sc_ref[0, SC_TOP1 + r] = a[r]
        sc_ref[0, SC_TOP2 + r] = b[r]
    sc_ref[0, SC_TAU] = best[PEER_TOPK - 1]
    sc_ref[0, SC_ZINV] = 1.0 / z


def _peer_router(pq, sub_keys):
    t = pq.shape[0]
    tt = TT_ROUTER
    nch = tt // LANES
    hp = PEER_HEADS
    return pl.pallas_call(
        _router_kernel,
        grid=(t // tt, hp),
        in_specs=[
            pl.BlockSpec((tt, 2 * PEER_HALF), lambda i, h: (i, h)),
            pl.BlockSpec((1, 2, PEER_NKEYS, PEER_HALF), lambda i, h: (h, 0, 0, 0)),
        ],
        out_specs=[
            pl.BlockSpec((1, 2, PEER_NKEYS, tt), lambda i, h: (h, 0, 0, i)),
            pl.BlockSpec((1, SC_ROWS, nch, LANES), lambda i, h: (h, 0, i, 0)),
        ],
        out_shape=[
            jax.ShapeDtypeStruct((hp, 2, PEER_NKEYS, t), F32),
            jax.ShapeDtypeStruct((hp, SC_ROWS, t // LANES, LANES), F32),
        ],
        scratch_shapes=[pltpu.VMEM((2, PEER_NKEYS * nch, LANES), F32)],
        compiler_params=_cparams(("parallel", "parallel")),
        name="peer_router",
    )(pq, sub_keys)


def _gelu2(x):
    c0 = math.sqrt(2.0 / math.pi)
    z = x * (c0 + (c0 * 0.044715) * (x * x))
    return x + x * jnp.tanh(z)


def _dup_bf16_bits(v):
    bits = pltpu.bitcast(v.astype(BF16).astype(F32), jnp.uint32)
    return bits | (bits >> 16)


def _peer_kernel(h_ref, st_hbm, sc_ref, u_ref, vt_ref, x_ref, gfin_ref, o_ref,
                 rank_scr, bw_scr, cnt_scr, aw_scr, act_scr, wa_scr, acc_scr, st_buf, st_sem,
                 *, final_norm):
    e = pl.program_id(1)
    tt = h_ref.shape[0]
    nsub = CE_STEP // CE_SUB
    ipc = CE_SUB // PEER_NKEYS
    pack = 2 * SUBLANES

    def st_copy(hd):
        tok = pl.ds(pl.multiple_of(pl.program_id(0) * tt, tt), tt)
        return pltpu.make_async_copy(st_hbm.at[hd, :, :, tok], st_buf.at[hd % 2], st_sem.at[hd % 2])

    @pl.when(e == 0)
    def _():
        st_copy(0).start()
        acc_scr[...] = jnp.zeros_like(acc_scr)
        for hd in range(PEER_HEADS):
            if hd + 1 < PEER_HEADS:
                st_copy(hd + 1).start()
            st_copy(hd).wait()
            s1 = st_buf[hd % 2, 0]
            s2 = st_buf[hd % 2, 1]
            tau = sc_ref[hd, SC_TAU:SC_TAU + 1, :]
            rank2 = jnp.zeros_like(s2)
            cnt = jnp.zeros_like(s1)
            for c in range(PEER_TOPK):
                b_c = sc_ref[hd, SC_TOP2 + c:SC_TOP2 + c + 1, :]
                rank2 = jnp.where(b_c > s2, c + 1.0, rank2)
                cnt = jnp.where(s1 + b_c >= tau, c + 1.0, cnt)
            a0 = sc_ref[hd, SC_TOP1:SC_TOP1 + 1, :]
            b0 = sc_ref[hd, SC_TOP2:SC_TOP2 + 1, :]
            zinv = sc_ref[hd, SC_ZINV:SC_ZINV + 1, :]
            rank_scr[hd] = rank2.astype(BF16)
            bw_scr[hd] = (jnp.exp(s2 - b0) * (0.5 * zinv)).astype(BF16)
            cnt_w = _dup_bf16_bits(cnt)
            a_w = _dup_bf16_bits(jnp.exp(s1 - a0))
            for lg in range(tt // LANES):
                cnt_scr[hd, lg] = cnt_w[:, lg * LANES:(lg + 1) * LANES]
                aw_scr[hd, lg] = a_w[:, lg * LANES:(lg + 1) * LANES]

    def row_bf16(tile, r):
        words = jnp.broadcast_to(tile[r:r + 1, :], (SUBLANES, LANES))
        return pltpu.bitcast(words, BF16)

    def up_proj(c):
        act_scr[c % 2] = lax.dot_general(u_ref[c * CE_SUB:(c + 1) * CE_SUB, :], h_ref[...], NT_DIMS,
                                         preferred_element_type=F32)

    def down_proj(c):
        acc_scr[...] += jnp.dot(vt_ref[:, c * CE_SUB:(c + 1) * CE_SUB], wa_scr[c % 2],
                                preferred_element_type=F32)

    def gate(c):
        njg = PEER_NKEYS // pack
        zero = jnp.zeros((pack, LANES), BF16)
        for lg in range(tt // LANES):
            lanes = slice(lg * LANES, (lg + 1) * LANES)
            w = [[None] * njg for _ in range(ipc)]
            for hd in range(PEER_HEADS):
                rank = [rank_scr[hd, jg * pack:(jg + 1) * pack, lanes] for jg in range(njg)]
                bw = [bw_scr[hd, jg * pack:(jg + 1) * pack, lanes] for jg in range(njg)]
                first = (e * nsub + c) * ipc
                base = pl.multiple_of((first // SUBLANES) * SUBLANES, SUBLANES)
                off = (c * ipc) % SUBLANES
                cnt_tile = cnt_scr[hd, lg, pl.ds(base, SUBLANES), :]
                a_tile = aw_scr[hd, lg, pl.ds(base, SUBLANES), :]
                for il in range(ipc):
                    cnt_row = row_bf16(cnt_tile, off + il)
                    a_row = row_bf16(a_tile, off + il)
                    for jg in range(njg):
                        term = jnp.where(rank[jg] < cnt_row, a_row * bw[jg], zero)
                        w[il][jg] = term if w[il][jg] is None else w[il][jg] + term
            for il in range(ipc):
                for jg in range(njg):
                    rows = slice(il * PEER_NKEYS + jg * pack, il * PEER_NKEYS + (jg + 1) * pack)
                    wa_scr[c % 2, rows, lanes] = w[il][jg] * _gelu2(act_scr[c % 2, rows, lanes]).astype(BF16)

    up_proj(0)
    for c in range(nsub):
        if c + 1 < nsub:
            up_proj(c + 1)
        if c >= 1:
            down_proj(c - 1)
        gate(c)
    down_proj(nsub - 1)

    @pl.when(e == pl.num_programs(1) - 1)
    def _():
        y = x_ref[...] + acc_scr[...].T
        if final_norm:
            y = _rms(y, gfin_ref[...])
        o_ref[...] = y


def _peer_experts(h_bf16, st, sc, u_bf16, vt_bf16, x2d, gfin, final_norm):
    t, d = x2d.shape
    ne = u_bf16.shape[0]
    tt, ce = TT_PEER, CE_STEP
    hp = PEER_HEADS
    return pl.pallas_call(
        functools.partial(_peer_kernel, final_norm=final_norm),
        grid=(t // tt, ne // ce),
        in_specs=[
            pl.BlockSpec((tt, d), lambda i, e: (i, 0)),
            pl.BlockSpec(memory_space=pl.ANY),
            pl.BlockSpec((hp, SC_ROWS, tt), lambda i, e: (0, 0, i)),
            pl.BlockSpec((ce, d), lambda i, e: (e, 0)),
            pl.BlockSpec((None, d, ce), lambda i, e: (e, 0, 0)),
            pl.BlockSpec((tt, d), lambda i, e: (i, 0)),
            pl.BlockSpec((1, d), lambda i, e: (0, 0)),
        ],
        out_specs=pl.BlockSpec((tt, d), lambda i, e: (i, 0)),
        out_shape=jax.ShapeDtypeStruct((t, d), F32),
        scratch_shapes=[
            pltpu.VMEM((hp, PEER_NKEYS, tt), BF16),
            pltpu.VMEM((hp, PEER_NKEYS, tt), BF16),
            pltpu.VMEM((hp, tt // LANES, PEER_NKEYS, LANES), jnp.uint32),
            pltpu.VMEM((hp, tt // LANES, PEER_NKEYS, LANES), jnp.uint32),
            pltpu.VMEM((2, CE_SUB, tt), F32),
            pltpu.VMEM((2, CE_SUB, tt), BF16),
            pltpu.VMEM((d, tt), F32),
            pltpu.VMEM((2, 2, PEER_NKEYS, tt), F32),
            pltpu.SemaphoreType.DMA((2,)),
        ],
        compiler_params=_cparams(("parallel", "arbitrary")),
        name="peer_experts",
    )(h_bf16, st, sc.reshape(hp, SC_ROWS, t), u_bf16, vt_bf16, x2d, gfin.reshape(1, d))


def _expert_major_t(v):
    ne, d = v.shape
    return v.astype(BF16).reshape(ne // CE_STEP, CE_STEP, d).transpose(0, 2, 1)


def _peer(h_bf16, pq, sub_keys, u_bf16, vt_bf16, x2d, gfin, final_norm):
    st, sc = _peer_router(pq, sub_keys)
    return _peer_experts(h_bf16, st, sc, u_bf16, vt_bf16, x2d, gfin, final_norm)


def _glu_kernel(x_ref, g_ref, w_ref, b_ref, u_ref, *, d):
    h = _rms(x_ref[...], g_ref[...]).astype(BF16)
    a = jnp.dot(h, w_ref[...], preferred_element_type=F32) + b_ref[...]
    u_ref[...] = a[:, :d] * jax.nn.sigmoid(a[:, d:])


def _glu_proj(x2d, g, w_bf16, bias):
    t, d = x2d.shape
    tm = TM_PROJ
    return pl.pallas_call(
        functools.partial(_glu_kernel, d=d),
        grid=(t // tm,),
        in_specs=[
            pl.BlockSpec((tm, d), lambda i: (i, 0)),
            pl.BlockSpec((1, d), lambda i: (0, 0)),
            pl.BlockSpec((d, 2 * d), lambda i: (0, 0)),
            pl.BlockSpec((1, 2 * d), lambda i: (0, 0)),
        ],
        out_specs=pl.BlockSpec((tm, d), lambda i: (i, 0)),
        out_shape=jax.ShapeDtypeStruct((t, d), F32),
        compiler_params=_cparams(("parallel",)),
        name="glu_proj",
    )(x2d, g.reshape(1, d), w_bf16, bias.reshape(1, 2 * d))


def _conv_kernel(ucur_ref, uprev_ref, wdw_ref, bdw_ref, lng_ref, lnb_ref, w2_ref, b2_ref,
                 x_ref, gf_ref, wpq_ref, x_out_ref, h_ref, pq_ref, ext_scr, y_scr):
    tm, d = ucur_ref.shape
    first = pl.program_id(1) == 0
    ext_scr[0:CONV_HALO, :] = jnp.where(first, 0.0, uprev_ref[...])
    ext_scr[CONV_HALO:, :] = ucur_ref[...]
    lead = CONV_HALO - (CONV_WIDTH - 1)

    def chunk(ci, carry):
        r0 = pl.multiple_of(ci * CONV_ROWS, CONV_ROWS)
        window = ext_scr[pl.ds(r0, CONV_ROWS + CONV_HALO), :]
        acc = jnp.broadcast_to(bdw_ref[...], (CONV_ROWS, d))
        for w in range(CONV_WIDTH):
            acc = acc + wdw_ref[w:w + 1, :] * window[lead + w:lead + w + CONV_ROWS, :]
        y_scr[pl.ds(r0, CONV_ROWS), :] = acc
        return carry

    lax.fori_loop(0, tm // CONV_ROWS, chunk, 0)
    y = y_scr[...]
    mu = jnp.mean(y, axis=-1, keepdims=True)
    var = jnp.mean(jnp.square(y - mu), axis=-1, keepdims=True)
    y = (y - mu) * lax.rsqrt(var + EPS) * lng_ref[...] + lnb_ref[...]
    y = y * jax.nn.sigmoid(y)
    x_new = x_ref[...] + jnp.dot(y.astype(BF16), w2_ref[...], preferred_element_type=F32) + b2_ref[...]
    _peerq_epilogue(x_new, gf_ref, wpq_ref, x_out_ref, h_ref, pq_ref)


def _conv_tail(u3, wdw, bdw, lng, lnb, w2_bf16, b2, x3, gf, wpq_bf16):
    b, s, d = x3.shape
    nq = wpq_bf16.shape[1]
    tm = TM_CONV
    hpb = tm // CONV_HALO
    row = lambda v: v.reshape(1, d)
    tok = pl.BlockSpec((None, tm, d), lambda bi, i: (bi, i, 0))
    vec = pl.BlockSpec((1, d), lambda bi, i: (0, 0))
    return pl.pallas_call(
        _conv_kernel,
        grid=(b, s // tm),
        in_specs=[
            tok,
            pl.BlockSpec((None, CONV_HALO, d), lambda bi, i: (bi, jnp.maximum(i * hpb - 1, 0), 0)),
            pl.BlockSpec((CONV_WIDTH, d), lambda bi, i: (0, 0)),
            vec, vec, vec,
            pl.BlockSpec((d, d), lambda bi, i: (0, 0)),
            vec,
            tok,
            vec,
            pl.BlockSpec((d, nq), lambda bi, i: (0, 0)),
        ],
        out_specs=[
            tok,
            tok,
            pl.BlockSpec((None, tm, nq), lambda bi, i: (bi, i, 0)),
        ],
        out_shape=[
            jax.ShapeDtypeStruct((b, s, d), F32),
            jax.ShapeDtypeStruct((b, s, d), BF16),
            jax.ShapeDtypeStruct((b, s, nq), F32),
        ],
        scratch_shapes=[
            pltpu.VMEM((tm + CONV_HALO, d), F32),
            pltpu.VMEM((tm, d), F32),
        ],
        compiler_params=_cparams(("parallel", "arbitrary")),
        name="conv_tail",
    )(u3, u3, wdw, row(bdw), row(lng), row(lnb), w2_bf16, row(b2), x3, row(gf), wpq_bf16)


def kernel(x, rel_bias, norm_mix, norm_ffn, attn_w_qkv, attn_w_o, conv_w_pw1, conv_b_pw1,
           conv_w_dw, conv_b_dw, conv_ln_g, conv_ln_b, conv_w_pw2, conv_b_pw2, peer_w_q,
           peer_sub_keys, peer_u, peer_v, norm_final):
    b, s, d = x.shape
    t = b * s
    depth = norm_mix.shape[0]
    assert d == N_HEADS * HEAD_DIM and s % MOBA_BLOCK == 0 and s // MOBA_BLOCK <= LANES
    assert t % TT_ROUTER == 0 and s % TM_CONV == 0 and t % TM_PROJ == 0
    x2d = x.reshape(t, d)
    for i in range(depth):
        j = i // 2
        wpq = peer_w_q[i].astype(BF16)
        if i % 2 == 0:
            wq, wk, wv = (attn_w_qkv[j][:, c * d:(c + 1) * d] for c in range(3))
            qt, k, vt, kmean = _qkv_proj(x2d.reshape(b, s, d), norm_mix[i], wq.T.astype(BF16),
                                         wk.astype(BF16), wv.T.astype(BF16))
            o = _moba_attention(qt, k, vt, kmean, _moba_bias_tables(rel_bias))
            x2d, h, pq = _attn_out_proj(o.reshape(t, d), attn_w_o[j].astype(BF16), x2d,
                                        norm_ffn[i], wpq)
        else:
            u = _glu_proj(x2d, norm_mix[i], conv_w_pw1[j].astype(BF16), conv_b_pw1[j])
            x3, h3, pq3 = _conv_tail(u.reshape(b, s, d), conv_w_dw[j], conv_b_dw[j], conv_ln_g[j],
                                     conv_ln_b[j], conv_w_pw2[j].astype(BF16), conv_b_pw2[j],
                                     x2d.reshape(b, s, d), norm_ffn[i], wpq)
            x2d, h, pq = x3.reshape(t, d), h3.reshape(t, d), pq3.reshape(t, -1)
        last = i == depth - 1
        x2d = _peer(h, pq, peer_sub_keys[i], peer_u[i].astype(BF16),
                    _expert_major_t(peer_v[i]), x2d, norm_final, final_norm=last)
    return x2d.reshape(b, s, d)
```

```python
import functools
import math

import numpy as np
import jax
import jax.numpy as jnp
from jax import lax
from jax.experimental import pallas as pl
from jax.experimental.pallas import tpu as pltpu

F32 = jnp.float32
BF16 = jnp.bfloat16
EPS = 1e-6
NEG = -1e30

N_HEADS = 16
HEAD_DIM = 64
MOBA_BLOCK = 256
MOBA_TOPK = 3
REL_BUCKETS = 32
REL_MAX_DIST = 128
CONV_WIDTH = 31
PEER_HEADS = 8
PEER_NKEYS = 128
PEER_HALF = 128
PEER_TOPK = 16
SC_TOP1, SC_TOP2, SC_TAU, SC_ZINV, SC_ROWS = 0, PEER_TOPK, 2 * PEER_TOPK, 2 * PEER_TOPK + 1, 2 * PEER_TOPK + 2

LANES = 128
SUBLANES = 8
VMEM_LIMIT = 60 * 1024 * 1024

MOBA_QLANES = 128
TM_PROJ = 512
TT_ROUTER = 1024
TT_PEER = 512
CE_STEP = 4096
CE_SUB = 512
TM_CONV = 512
CONV_HALO = 32
CONV_ROWS = 16

NT_DIMS = (((1,), (1,)), ((), ()))


def _cparams(sem):
    return pltpu.CompilerParams(dimension_semantics=sem, vmem_limit_bytes=VMEM_LIMIT)


def _rms(x, g):
    return x * lax.rsqrt(jnp.mean(x * x, axis=-1, keepdims=True) + EPS) * g


def _qkv_kernel(x_ref, g_ref, wqt_ref, wk_ref, wvt_ref, qt_ref, k_ref, vt_ref, km_ref):
    h = _rms(x_ref[...], g_ref[...]).astype(BF16)
    scale = HEAD_DIM ** -0.5
    qt = lax.dot_general(wqt_ref[...], h, NT_DIMS, preferred_element_type=F32)
    qt_ref[...] = (qt * scale).astype(BF16)
    k = jnp.dot(h, wk_ref[...], preferred_element_type=F32)
    k_ref[...] = k.astype(BF16)
    vt = lax.dot_general(wvt_ref[...], h, NT_DIMS, preferred_element_type=F32)
    vt_ref[...] = vt.astype(BF16)
    nblk = k.shape[0] // MOBA_BLOCK
    km_ref[...] = jnp.mean(k.reshape(nblk, MOBA_BLOCK, k.shape[1]), axis=1)


def _qkv_proj(x3, g, wqt_bf16, wk_bf16, wvt_bf16):
    b, s, d = x3.shape
    tm = TM_PROJ
    nblk = tm // MOBA_BLOCK
    nb = s // MOBA_BLOCK
    wspec = pl.BlockSpec((d, d), lambda bi, i: (0, 0))
    tspec = pl.BlockSpec((None, d, tm), lambda bi, i: (bi, 0, i))
    return pl.pallas_call(
        _qkv_kernel,
        grid=(b, s // tm),
        in_specs=[
            pl.BlockSpec((None, tm, d), lambda bi, i: (bi, i, 0)),
            pl.BlockSpec((1, d), lambda bi, i: (0, 0)),
            wspec, wspec, wspec,
        ],
        out_specs=[
            tspec,
            pl.BlockSpec((None, tm, d), lambda bi, i: (bi, i, 0)),
            tspec,
            pl.BlockSpec((None, None, nblk, d), lambda bi, i: (bi, i, 0, 0)),
        ],
        out_shape=[
            jax.ShapeDtypeStruct((b, d, s), BF16),
            jax.ShapeDtypeStruct((b, s, d), BF16),
            jax.ShapeDtypeStruct((b, d, s), BF16),
            jax.ShapeDtypeStruct((b, s // tm, nblk, d), F32),
        ],
        compiler_params=_cparams(("parallel", "parallel")),
        name="qkv_proj",
    )(x3, g.reshape(1, d), wqt_bf16, wk_bf16, wvt_bf16)


def _moba_kernel(qt_ref, k_ref, vt_ref, km_ref, tbl_ref, o_ref, sel_scr, *, nb):
    blk = MOBA_BLOCK
    i = pl.program_id(2)
    jp = jnp.maximum(i - 1, 0)
    qt = qt_ref[...]
    frow = lax.broadcasted_iota(jnp.int32, qt.shape, 0)
    blk_id = lax.broadcasted_iota(jnp.int32, (nb, blk), 0)
    valid = blk_id < i
    km = km_ref[...]

    qh = []
    for hh in range(2):
        qh_t = jnp.where(frow // HEAD_DIM == hh, qt, jnp.zeros_like(qt))
        qh.append(qh_t)
        gate = jnp.dot(km, qh_t.astype(F32), precision=lax.Precision.HIGHEST,
                       preferred_element_type=F32)
        gate = jnp.where(valid, gate, NEG)
        g = gate
        for _ in range(MOBA_TOPK - 1):
            g = jnp.where(g >= jnp.max(g, axis=0, keepdims=True), NEG, g)
        tau = jnp.max(g, axis=0, keepdims=True)
        sel = jnp.where(valid, jnp.where(gate >= tau, 0.0, NEG), NEG)
        for qs in range(blk // MOBA_QLANES):
            sel_scr[hh, qs] = sel[:, qs * MOBA_QLANES:(qs + 1) * MOBA_QLANES]

    def k_block(j):
        return k_ref[pl.ds(pl.multiple_of(j * blk, blk), blk), :]

    chains = [(hh, qs) for hh in range(2) for qs in range(blk // MOBA_QLANES)]

    def qcols(qs):
        return slice(qs * MOBA_QLANES, (qs + 1) * MOBA_QLANES)

    def scores(kj, hh, qs):
        return jnp.dot(kj, qh[hh][:, qcols(qs)], preferred_element_type=F32)

    def sel_row(hh, qs, j):
        return sel_scr[hh, qs, pl.ds(j, 1), :]

    def pv(hh, p, j):
        vj = vt_ref[hh * HEAD_DIM:(hh + 1) * HEAD_DIM, pl.ds(pl.multiple_of(j * blk, blk), blk)]
        return jnp.dot(vj, p.astype(BF16), preferred_element_type=F32)

    def softmax_step(m_prev, l_prev, s):
        m_new = jnp.maximum(m_prev, jnp.max(s, axis=0, keepdims=True))
        alpha = jnp.exp(m_prev - m_new)
        p = jnp.exp(s - m_new)
        l_new = alpha * l_prev + jnp.sum(p, axis=0, keepdims=True)
        return m_new, l_new, alpha, p.astype(BF16)

    def block_scores(j):
        kj = k_block(j)
        return tuple(scores(kj, hh, qs) + sel_row(hh, qs, j) for hh, qs in chains)

    k_own, k_prev = k_block(i), k_block(jp)
    ss = [scores(k_own, hh, qs) + tbl_ref[hh, 0, :, qcols(qs)] for hh, qs in chains]
    ss_prev = [scores(k_prev, hh, qs) + tbl_ref[hh, 1, :, qcols(qs)] + sel_row(hh, qs, jp)
               for hh, qs in chains]
    ms = [jnp.max(s, axis=0, keepdims=True) for s in ss]
    ps = [jnp.exp(s - m) for s, m in zip(ss, ms)]
    ls = [jnp.sum(p, axis=0, keepdims=True) for p in ps]
    accs = [pv(hh, p, i) for (hh, _), p in zip(chains, ps)]
    steps = [softmax_step(m, l, s) for m, l, s in zip(ms, ls, ss_prev)]
    accs = [alpha * acc + pv(hh, p, jp) for (hh, _), acc, (_, _, alpha, p) in zip(chains, accs, steps)]
    ms = [st[0] for st in steps]
    ls = [st[1] for st in steps]

    def body(j, state):
        ss_cur, ms, ls, accs, ps_prev = state
        ss_next = block_scores(j + 1)
        pvs = [pv(hh, p, jnp.maximum(j - 1, 0)) for (hh, _), p in zip(chains, ps_prev)]
        steps = [softmax_step(m, l, s) for m, l, s in zip(ms, ls, ss_cur)]
        accs = [alpha * (acc + r) for acc, r, (_, _, alpha, _) in zip(accs, pvs, steps)]
        return (ss_next, tuple(st[0] for st in steps), tuple(st[1] for st in steps), tuple(accs),
                tuple(st[3] for st in steps))

    zero_p = tuple(jnp.zeros((blk, MOBA_QLANES), BF16) for _ in chains)
    state = (block_scores(0), tuple(ms), tuple(ls), tuple(accs), zero_p)
    _, ms, ls, accs, ps_last = lax.fori_loop(0, jp, body, state)
    accs = [acc + pv(hh, p, jnp.maximum(jp - 1, 0)) for (hh, _), acc, p in zip(chains, accs, ps_last)]
    outs = [acc / l for l, acc in zip(ls, accs)]
    nqs = blk // MOBA_QLANES
    o_t = jnp.concatenate([jnp.concatenate(outs[hh * nqs:(hh + 1) * nqs], axis=1) for hh in range(2)],
                          axis=0)
    o_ref[...] = o_t.T.astype(o_ref.dtype)


def _t5_bucket_np(dist):
    max_exact = REL_BUCKETS // 2
    d = np.maximum(dist, 0)
    df = np.maximum(d, 1).astype(np.float32)
    large = max_exact + (np.log(df / max_exact) / math.log(REL_MAX_DIST / max_exact)
                         * (REL_BUCKETS - max_exact)).astype(np.int32)
    large = np.minimum(large, REL_BUCKETS - 1)
    return np.where(d < max_exact, d, large)


def _moba_bias_tables(rel_bias):
    blk = MOBA_BLOCK
    qi = np.arange(blk)[None, :]
    ki = np.arange(blk)[:, None]
    assert int(_t5_bucket_np(np.array([blk + 1]))[0]) == REL_BUCKETS - 1
    b_own = _t5_bucket_np(qi - ki)
    b_prev = _t5_bucket_np(qi - ki + blk)
    far = rel_bias[:, REL_BUCKETS - 1][:, None, None]
    own = jnp.where((qi >= ki)[None], rel_bias[:, b_own] - far, NEG)
    prev = rel_bias[:, b_prev] - far
    return jnp.stack([own, prev], axis=1).astype(F32)


def _moba_attention(qt, k, vt, kmean, tbl):
    b, s, d = k.shape
    blk = MOBA_BLOCK
    nb = s // blk
    nhp = d // LANES
    return pl.pallas_call(
        functools.partial(_moba_kernel, nb=nb),
        grid=(b, nhp, nb),
        in_specs=[
            pl.BlockSpec((None, LANES, blk), lambda bi, hp, i: (bi, hp, i)),
            pl.BlockSpec((None, s, LANES), lambda bi, hp, i: (bi, 0, hp)),
            pl.BlockSpec((None, LANES, s), lambda bi, hp, i: (bi, hp, 0)),
            pl.BlockSpec((None, nb, LANES), lambda bi, hp, i: (bi, 0, hp)),
            pl.BlockSpec((2, 2, blk, blk), lambda bi, hp, i: (hp, 0, 0, 0)),
        ],
        out_specs=pl.BlockSpec((None, blk, LANES), lambda bi, hp, i: (bi, i, hp)),
        out_shape=jax.ShapeDtypeStruct((b, s, d), BF16),
        scratch_shapes=[pltpu.VMEM((2, blk // MOBA_QLANES, nb, MOBA_QLANES), F32)],
        compiler_params=_cparams(("parallel", "parallel", "arbitrary")),
        name="moba_attention",
    )(qt, k, vt, kmean.reshape(b, nb, d), tbl)


def _peerq_epilogue(x_new, gf_ref, wpq_ref, x_out_ref, h_ref, pq_ref):
    x_out_ref[...] = x_new
    h = _rms(x_new, gf_ref[...]).astype(BF16)
    h_ref[...] = h
    pq_ref[...] = jnp.dot(h, wpq_ref[...], preferred_element_type=F32)


def _attn_out_kernel(o_ref, wo_ref, x_ref, gf_ref, wpq_ref, x_out_ref, h_ref, pq_ref):
    x_new = x_ref[...] + jnp.dot(o_ref[...], wo_ref[...], preferred_element_type=F32)
    _peerq_epilogue(x_new, gf_ref, wpq_ref, x_out_ref, h_ref, pq_ref)


def _attn_out_proj(o2d, wo_bf16, x2d, gf, wpq_bf16):
    t, d = x2d.shape
    nq = wpq_bf16.shape[1]
    tm = TM_PROJ
    return pl.pallas_call(
        _attn_out_kernel,
        grid=(t // tm,),
        in_specs=[
            pl.BlockSpec((tm, d), lambda i: (i, 0)),
            pl.BlockSpec((d, d), lambda i: (0, 0)),
            pl.BlockSpec((tm, d), lambda i: (i, 0)),
            pl.BlockSpec((1, d), lambda i: (0, 0)),
            pl.BlockSpec((d, nq), lambda i: (0, 0)),
        ],
        out_specs=[
            pl.BlockSpec((tm, d), lambda i: (i, 0)),
            pl.BlockSpec((tm, d), lambda i: (i, 0)),
            pl.BlockSpec((tm, nq), lambda i: (i, 0)),
        ],
        out_shape=[
            jax.ShapeDtypeStruct((t, d), F32),
            jax.ShapeDtypeStruct((t, d), BF16),
            jax.ShapeDtypeStruct((t, nq), F32),
        ],
        compiler_params=_cparams(("parallel",)),
        name="attn_out_proj",
    )(o2d, wo_bf16, x2d, gf.reshape(1, d), wpq_bf16)


def _cmpx(vals, a, b):
    hi = jnp.maximum(vals[a], vals[b])
    lo = jnp.minimum(vals[a], vals[b])
    vals[a], vals[b] = hi, lo


def _bitonic_merge_desc(vals):
    n = len(vals)
    dist = n // 2
    while dist >= 1:
        for s in range(0, n, 2 * dist):
            for t in range(s, s + dist):
                _cmpx(vals, t, t + dist)
        dist //= 2
    return vals


def _sort_desc(vals):
    n = len(vals)
    if n == 1:
        return vals
    top = _sort_desc(vals[: n // 2])
    bot = _sort_desc(vals[n // 2:])
    return _bitonic_merge_desc(top + bot[::-1])


def _top_merge(a, b):
    n = len(a)
    return _bitonic_merge_desc([jnp.maximum(a[r], b[n - 1 - r]) for r in range(n)])


def _sorted_top16(vals):
    groups = [_sort_desc(vals[g:g + PEER_TOPK]) for g in range(0, len(vals), PEER_TOPK)]
    while len(groups) > 1:
        groups = [_top_merge(groups[g], groups[g + 1]) for g in range(0, len(groups), 2)]
    return groups[0]


def _router_kernel(pq_ref, keys_ref, st_ref, sc_ref, scr_ref):
    nch = TT_ROUTER // LANES
    tops = []
    for c in range(2):
        qc = pq_ref[:, c * PEER_HALF:(c + 1) * PEER_HALF]
        s_t = lax.dot_general(keys_ref[0, c], qc, NT_DIMS,
                              precision=lax.Precision.HIGHEST,
                              preferred_element_type=F32)
        st_ref[0, c] = s_t
        for ch in range(nch):
            scr_ref[c, pl.ds(ch, PEER_NKEYS, stride=nch), :] = s_t[:, ch * LANES:(ch + 1) * LANES]
        vals = [scr_ref[c, k * nch:(k + 1) * nch, :] for k in range(PEER_NKEYS)]
        tops.append(_sorted_top16(vals))
    a, b = tops
    cands = [a[r] + b[c] for r in range(PEER_TOPK) for c in range(PEER_TOPK)
             if (r + 1) * (c + 1) <= PEER_TOPK]
    npad = 1 << (len(cands) - 1).bit_length()
    cands = cands + [jnp.full_like(cands[0], NEG)] * (npad - len(cands))
    best = _sort_desc(cands)[:PEER_TOPK]
    z = jnp.ones_like(best[0])
    for r in range(1, PEER_TOPK):
        z = z + jnp.exp(best[r] - best[0])
    for r in range(PEER_TOPK):
        sc_ref[0, SC_TOP1 + r] = a[r]
        sc_ref[0, SC_TOP2 + r] = b[r]
    sc_ref[0, SC_TAU] = best[PEER_TOPK - 1]
    sc_ref[0, SC_ZINV] = 1.0 / z


def _peer_router(pq, sub_keys):
    t = pq.shape[0]
    tt = TT_ROUTER
    nch = tt // LANES
    hp = PEER_HEADS
    return pl.pallas_call(
        _router_kernel,
        grid=(t // tt, hp),
        in_specs=[
            pl.BlockSpec((tt, 2 * PEER_HALF), lambda i, h: (i, h)),
            pl.BlockSpec((1, 2, PEER_NKEYS, PEER_HALF), lambda i, h: (h, 0, 0, 0)),
        ],
        out_specs=[
            pl.BlockSpec((1, 2, PEER_NKEYS, tt), lambda i, h: (h, 0, 0, i)),
            pl.BlockSpec((1, SC_ROWS, nch, LANES), lambda i, h: (h, 0, i, 0)),
        ],
        out_shape=[
            jax.ShapeDtypeStruct((hp, 2, PEER_NKEYS, t), F32),
            jax.ShapeDtypeStruct((hp, SC_ROWS, t // LANES, LANES), F32),
        ],
        scratch_shapes=[pltpu.VMEM((2, PEER_NKEYS * nch, LANES), F32)],
        compiler_params=_cparams(("parallel", "parallel")),
        name="peer_router",
    )(pq, sub_keys)


def _gelu2(x):
    c0 = math.sqrt(2.0 / math.pi)
    z = x * (c0 + (c0 * 0.044715) * (x * x))
    return x + x * jnp.tanh(z)


def _dup_bf16_bits(v):
    bits = pltpu.bitcast(v.astype(BF16).astype(F32), jnp.uint32)
    return bits | (bits >> 16)


def _peer_kernel(h_ref, st_hbm, sc_ref, u_ref, vt_ref, x_ref, gfin_ref, o_ref,
                 rank_scr, bw_scr, cnt_scr, aw_scr, act_scr, wa_scr, acc_scr, st_buf, st_sem,
                 *, final_norm):
    e = pl.program_id(1)
    tt = h_ref.shape[0]
    nsub = CE_STEP // CE_SUB
    ipc = CE_SUB // PEER_NKEYS
    pack = 2 * SUBLANES

    def st_copy(hd):
        tok = pl.ds(pl.multiple_of(pl.program_id(0) * tt, tt), tt)
        return pltpu.make_async_copy(st_hbm.at[hd, :, :, tok], st_buf.at[hd % 2], st_sem.at[hd % 2])

    def head_tables(hd, carry):
        @pl.when(hd + 1 < PEER_HEADS)
        def _():
            st_copy(hd + 1).start()

        st_copy(hd).wait()
        s1 = st_buf[hd % 2, 0]
        s2 = st_buf[hd % 2, 1]
        tau = sc_ref[hd, SC_TAU:SC_TAU + 1, :]
        rank2 = jnp.zeros_like(s2)
        cnt = jnp.zeros_like(s1)
        for c in range(PEER_TOPK):
            b_c = sc_ref[hd, SC_TOP2 + c:SC_TOP2 + c + 1, :]
            rank2 = jnp.where(b_c > s2, c + 1.0, rank2)
            cnt = jnp.where(s1 + b_c >= tau, c + 1.0, cnt)
        a0 = sc_ref[hd, SC_TOP1:SC_TOP1 + 1, :]
        b0 = sc_ref[hd, SC_TOP2:SC_TOP2 + 1, :]
        zinv = sc_ref[hd, SC_ZINV:SC_ZINV + 1, :]
        rank_scr[hd] = rank2.astype(BF16)
        bw_scr[hd] = (jnp.exp(s2 - b0) * (0.5 * zinv)).astype(BF16)
        cnt_w = _dup_bf16_bits(cnt)
        a_w = _dup_bf16_bits(jnp.exp(s1 - a0))
        for lg in range(tt // LANES):
            cnt_scr[hd, lg] = cnt_w[:, lg * LANES:(lg + 1) * LANES]
            aw_scr[hd, lg] = a_w[:, lg * LANES:(lg + 1) * LANES]
        return carry

    @pl.when(e == 0)
    def _():
        st_copy(0).start()
        acc_scr[...] = jnp.zeros_like(acc_scr)
        lax.fori_loop(0, PEER_HEADS, head_tables, 0)

    def row_bf16(tile, r):
        words = jnp.broadcast_to(tile[r:r + 1, :], (SUBLANES, LANES))
        return pltpu.bitcast(words, BF16)

    def experts(c):
        return pl.ds(pl.multiple_of(c * CE_SUB, CE_SUB), CE_SUB)

    def up_proj(c, buf):
        act_scr[buf] = lax.dot_general(u_ref[experts(c), :], h_ref[...], NT_DIMS,
                                       preferred_element_type=F32)

    def down_proj(c, buf):
        acc_scr[...] += jnp.dot(vt_ref[:, experts(c)], wa_scr[buf], preferred_element_type=F32)

    def gate_lanes(c, lg):
        njg = PEER_NKEYS // pack
        zero = jnp.zeros((pack, LANES), BF16)
        lanes = slice(lg * LANES, (lg + 1) * LANES)
        first = (e * nsub + c) * ipc
        base = pl.multiple_of((first // SUBLANES) * SUBLANES, SUBLANES)
        to_top = (SUBLANES - first % SUBLANES) % SUBLANES
        w = [[None] * njg for _ in range(ipc)]
        for hd in range(PEER_HEADS):
            rank = [rank_scr[hd, jg * pack:(jg + 1) * pack, lanes] for jg in range(njg)]
            bw = [bw_scr[hd, jg * pack:(jg + 1) * pack, lanes] for jg in range(njg)]
            cnt_tile = cnt_scr[hd, lg, pl.ds(base, SUBLANES), :]
            a_tile = aw_scr[hd, lg, pl.ds(base, SUBLANES), :]
            if ipc < SUBLANES:
                cnt_tile = pltpu.roll(cnt_tile, to_top, 0)
                a_tile = pltpu.roll(a_tile, to_top, 0)
            for il in range(ipc):
                cnt_row = row_bf16(cnt_tile, il)
                a_row = row_bf16(a_tile, il)
                for jg in range(njg):
                    term = jnp.where(rank[jg] < cnt_row, a_row * bw[jg], zero)
                    w[il][jg] = term if w[il][jg] is None else w[il][jg] + term
        for il in range(ipc):
            for jg in range(njg):
                rows = slice(il * PEER_NKEYS + jg * pack, il * PEER_NKEYS + (jg + 1) * pack)
                wa_scr[c % 2, rows, lanes] = w[il][jg] * _gelu2(act_scr[c % 2, rows, lanes]).astype(BF16)

    wa_scr[1] = jnp.zeros(wa_scr.shape[1:], BF16)
    up_proj(0, 0)

    def sub_chunk(c, carry):
        up_proj(jnp.minimum(c + 1, nsub - 1), (c + 1) % 2)
        down_proj(jnp.maximum(c - 1, 0), (c + 1) % 2)

        for lg in range(tt // LANES):
            gate_lanes(c, lg)
        return carry

    lax.fori_loop(0, nsub, sub_chunk, 0)
    down_proj(nsub - 1, (nsub - 1) % 2)

    @pl.when(e == pl.num_programs(1) - 1)
    def _():
        y = x_ref[...] + acc_scr[...].T
        if final_norm:
            y = _rms(y, gfin_ref[...])
        o_ref[...] = y


def _peer_experts(h_bf16, st, sc, u_bf16, vt_bf16, x2d, gfin, final_norm):
    t, d = x2d.shape
    ne = u_bf16.shape[0]
    tt, ce = TT_PEER, CE_STEP
    hp = PEER_HEADS
    return pl.pallas_call(
        functools.partial(_peer_kernel, final_norm=final_norm),
        grid=(t // tt, ne // ce),
        in_specs=[
            pl.BlockSpec((tt, d), lambda i, e: (i, 0)),
            pl.BlockSpec(memory_space=pl.ANY),
            pl.BlockSpec((hp, SC_ROWS, tt), lambda i, e: (0, 0, i)),
            pl.BlockSpec((ce, d), lambda i, e: (e, 0)),
            pl.BlockSpec((None, d, ce), lambda i, e: (e, 0, 0)),
            pl.BlockSpec((tt, d), lambda i, e: (i, 0)),
            pl.BlockSpec((1, d), lambda i, e: (0, 0)),
        ],
        out_specs=pl.BlockSpec((tt, d), lambda i, e: (i, 0)),
        out_shape=jax.ShapeDtypeStruct((t, d), F32),
        scratch_shapes=[
            pltpu.VMEM((hp, PEER_NKEYS, tt), BF16),
            pltpu.VMEM((hp, PEER_NKEYS, tt), BF16),
            pltpu.VMEM((hp, tt // LANES, PEER_NKEYS, LANES), jnp.uint32),
            pltpu.VMEM((hp, tt // LANES, PEER_NKEYS, LANES), jnp.uint32),
            pltpu.VMEM((2, CE_SUB, tt), F32),
            pltpu.VMEM((2, CE_SUB, tt), BF16),
            pltpu.VMEM((d, tt), F32),
            pltpu.VMEM((2, 2, PEER_NKEYS, tt), F32),
            pltpu.SemaphoreType.DMA((2,)),
        ],
        compiler_params=_cparams(("parallel", "arbitrary")),
        name="peer_experts",
    )(h_bf16, st, sc.reshape(hp, SC_ROWS, t), u_bf16, vt_bf16, x2d, gfin.reshape(1, d))


def _expert_major_t(v):
    ne, d = v.shape
    return v.astype(BF16).reshape(ne // CE_STEP, CE_STEP, d).transpose(0, 2, 1)


def _peer(h_bf16, pq, sub_keys, u_bf16, vt_bf16, x2d, gfin, final_norm):
    st, sc = _peer_router(pq, sub_keys)
    return _peer_experts(h_bf16, st, sc, u_bf16, vt_bf16, x2d, gfin, final_norm)


def _glu_kernel(x_ref, g_ref, w_ref, b_ref, u_ref, *, d):
    h = _rms(x_ref[...], g_ref[...]).astype(BF16)
    a = jnp.dot(h, w_ref[...], preferred_element_type=F32) + b_ref[...]
    u_ref[...] = a[:, :d] * jax.nn.sigmoid(a[:, d:])


def _glu_proj(x2d, g, w_bf16, bias):
    t, d = x2d.shape
    tm = TM_PROJ
    return pl.pallas_call(
        functools.partial(_glu_kernel, d=d),
        grid=(t // tm,),
        in_specs=[
            pl.BlockSpec((tm, d), lambda i: (i, 0)),
            pl.BlockSpec((1, d), lambda i: (0, 0)),
            pl.BlockSpec((d, 2 * d), lambda i: (0, 0)),
            pl.BlockSpec((1, 2 * d), lambda i: (0, 0)),
        ],
        out_specs=pl.BlockSpec((tm, d), lambda i: (i, 0)),
        out_shape=jax.ShapeDtypeStruct((t, d), F32),
        compiler_params=_cparams(("parallel",)),
        name="glu_proj",
    )(x2d, g.reshape(1, d), w_bf16, bias.reshape(1, 2 * d))


def _conv_kernel(ucur_ref, uprev_ref, wdw_ref, bdw_ref, lng_ref, lnb_ref, w2_ref, b2_ref,
                 x_ref, gf_ref, wpq_ref, x_out_ref, h_ref, pq_ref, ext_scr, y_scr):
    tm, d = ucur_ref.shape
    first = pl.program_id(1) == 0
    ext_scr[0:CONV_HALO, :] = jnp.where(first, 0.0, uprev_ref[...])
    ext_scr[CONV_HALO:, :] = ucur_ref[...]
    lead = CONV_HALO - (CONV_WIDTH - 1)

    def chunk(ci, carry):
        r0 = pl.multiple_of(ci * CONV_ROWS, CONV_ROWS)
        window = ext_scr[pl.ds(r0, CONV_ROWS + CONV_HALO), :]
        acc = jnp.broadcast_to(bdw_ref[...], (CONV_ROWS, d))
        for w in range(CONV_WIDTH):
            acc = acc + wdw_ref[w:w + 1, :] * window[lead + w:lead + w + CONV_ROWS, :]
        y_scr[pl.ds(r0, CONV_ROWS), :] = acc
        return carry

    lax.fori_loop(0, tm // CONV_ROWS, chunk, 0)
    y = y_scr[...]
    mu = jnp.mean(y, axis=-1, keepdims=True)
    var = jnp.mean(jnp.square(y - mu), axis=-1, keepdims=True)
    y = (y - mu) * lax.rsqrt(var + EPS) * lng_ref[...] + lnb_ref[...]
    y = y * jax.nn.sigmoid(y)
    x_new = x_ref[...] + jnp.dot(y.astype(BF16), w2_ref[...], preferred_element_type=F32) + b2_ref[...]
    _peerq_epilogue(x_new, gf_ref, wpq_ref, x_out_ref, h_ref, pq_ref)


def _conv_tail(u3, wdw, bdw, lng, lnb, w2_bf16, b2, x3, gf, wpq_bf16):
    b, s, d = x3.shape
    nq = wpq_bf16.shape[1]
    tm = TM_CONV
    hpb = tm // CONV_HALO
    row = lambda v: v.reshape(1, d)
    tok = pl.BlockSpec((None, tm, d), lambda bi, i: (bi, i, 0))
    vec = pl.BlockSpec((1, d), lambda bi, i: (0, 0))
    return pl.pallas_call(
        _conv_kernel,
        grid=(b, s // tm),
        in_specs=[
            tok,
            pl.BlockSpec((None, CONV_HALO, d), lambda bi, i: (bi, jnp.maximum(i * hpb - 1, 0), 0)),
            pl.BlockSpec((CONV_WIDTH, d), lambda bi, i: (0, 0)),
            vec, vec, vec,
            pl.BlockSpec((d, d), lambda bi, i: (0, 0)),
            vec,
            tok,
            vec,
            pl.BlockSpec((d, nq), lambda bi, i: (0, 0)),
        ],
        out_specs=[
            tok,
            tok,
            pl.BlockSpec((None, tm, nq), lambda bi, i: (bi, i, 0)),
        ],
        out_shape=[
            jax.ShapeDtypeStruct((b, s, d), F32),
            jax.ShapeDtypeStruct((b, s, d), BF16),
            jax.ShapeDtypeStruct((b, s, nq), F32),
        ],
        scratch_shapes=[
            pltpu.VMEM((tm + CONV_HALO, d), F32),
            pltpu.VMEM((tm, d), F32),
        ],
        compiler_params=_cparams(("parallel", "arbitrary")),
        name="conv_tail",
    )(u3, u3, wdw, row(bdw), row(lng), row(lnb), w2_bf16, row(b2), x3, row(gf), wpq_bf16)


def kernel(x, rel_bias, norm_mix, norm_ffn, attn_w_qkv, attn_w_o, conv_w_pw1, conv_b_pw1,
           conv_w_dw, conv_b_dw, conv_ln_g, conv_ln_b, conv_w_pw2, conv_b_pw2, peer_w_q,
           peer_sub_keys, peer_u, peer_v, norm_final):
    b, s, d = x.shape
    t = b * s
    depth = norm_mix.shape[0]
    assert d == N_HEADS * HEAD_DIM and s % MOBA_BLOCK == 0 and s // MOBA_BLOCK <= LANES
    assert t % TT_ROUTER == 0 and s % TM_CONV == 0 and t % TM_PROJ == 0
    x2d = x.reshape(t, d)
    for i in range(depth):
        j = i // 2
        wpq = peer_w_q[i].astype(BF16)
        if i % 2 == 0:
            wq, wk, wv = (attn_w_qkv[j][:, c * d:(c + 1) * d] for c in range(3))
            qt, k, vt, kmean = _qkv_proj(x2d.reshape(b, s, d), norm_mix[i], wq.T.astype(BF16),
                                         wk.astype(BF16), wv.T.astype(BF16))
            o = _moba_attention(qt, k, vt, kmean, _moba_bias_tables(rel_bias))
            x2d, h, pq = _attn_out_proj(o.reshape(t, d), attn_w_o[j].astype(BF16), x2d,
                                        norm_ffn[i], wpq)
        else:
            u = _glu_proj(x2d, norm_mix[i], conv_w_pw1[j].astype(BF16), conv_b_pw1[j])
            x3, h3, pq3 = _conv_tail(u.reshape(b, s, d), conv_w_dw[j], conv_b_dw[j], conv_ln_g[j],
                                     conv_ln_b[j], conv_w_pw2[j].astype(BF16), conv_b_pw2[j],
                                     x2d.reshape(b, s, d), norm_ffn[i], wpq)
            x2d, h, pq = x3.reshape(t, d), h3.reshape(t, d), pq3.reshape(t, -1)
        last = i == depth - 1
        x2d = _peer(h, pq, peer_sub_keys[i], peer_u[i].astype(BF16),
                    _expert_major_t(peer_v[i]), x2d, norm_final, final_norm=last)
    return x2d.reshape(b, s, d)
```

```python
import functools
import math

import numpy as np
import jax
import jax.numpy as jnp
from jax import lax
from jax.experimental import pallas as pl
from jax.experimental.pallas import tpu as pltpu

F32 = jnp.float32
BF16 = jnp.bfloat16
EPS = 1e-6
NEG = -1e30

N_HEADS = 16
HEAD_DIM = 64
MOBA_BLOCK = 256
MOBA_TOPK = 3
REL_BUCKETS = 32
REL_MAX_DIST = 128
CONV_WIDTH = 31
PEER_HEADS = 8
PEER_NKEYS = 128
PEER_HALF = 128
PEER_TOPK = 16
SC_TOP1, SC_TOP2, SC_TAU, SC_ZINV, SC_ROWS = 0, PEER_TOPK, 2 * PEER_TOPK, 2 * PEER_TOPK + 1, 2 * PEER_TOPK + 2

LANES = 128
SUBLANES = 8
VMEM_LIMIT = 60 * 1024 * 1024

MOBA_QLANES = 128
TM_PROJ = 512
TT_ROUTER = 1024
TT_PEER = 512
CE_STEP = 2048
CE_SUB = 512
TM_CONV = 512
CONV_HALO = 32
CONV_ROWS = 16

NT_DIMS = (((1,), (1,)), ((), ()))


def _cparams(sem):
    return pltpu.CompilerParams(dimension_semantics=sem, vmem_limit_bytes=VMEM_LIMIT)


def _rms(x, g):
    return x * lax.rsqrt(jnp.mean(x * x, axis=-1, keepdims=True) + EPS) * g


def _qkv_kernel(x_ref, g_ref, wqt_ref, wk_ref, wvt_ref, qt_ref, k_ref, vt_ref, km_ref):
    h = _rms(x_ref[...], g_ref[...]).astype(BF16)
    scale = HEAD_DIM ** -0.5
    qt = lax.dot_general(wqt_ref[...], h, NT_DIMS, preferred_element_type=F32)
    qt_ref[...] = (qt * scale).astype(BF16)
    k = jnp.dot(h, wk_ref[...], preferred_element_type=F32)
    k_ref[...] = k.astype(BF16)
    vt = lax.dot_general(wvt_ref[...], h, NT_DIMS, preferred_element_type=F32)
    vt_ref[...] = vt.astype(BF16)
    nblk = k.shape[0] // MOBA_BLOCK
    km_ref[...] = jnp.mean(k.reshape(nblk, MOBA_BLOCK, k.shape[1]), axis=1)


def _qkv_proj(x3, g, wqt_bf16, wk_bf16, wvt_bf16):
    b, s, d = x3.shape
    tm = TM_PROJ
    nblk = tm // MOBA_BLOCK
    nb = s // MOBA_BLOCK
    wspec = pl.BlockSpec((d, d), lambda bi, i: (0, 0))
    tspec = pl.BlockSpec((None, d, tm), lambda bi, i: (bi, 0, i))
    return pl.pallas_call(
        _qkv_kernel,
        grid=(b, s // tm),
        in_specs=[
            pl.BlockSpec((None, tm, d), lambda bi, i: (bi, i, 0)),
            pl.BlockSpec((1, d), lambda bi, i: (0, 0)),
            wspec, wspec, wspec,
        ],
        out_specs=[
            tspec,
            pl.BlockSpec((None, tm, d), lambda bi, i: (bi, i, 0)),
            tspec,
            pl.BlockSpec((None, None, nblk, d), lambda bi, i: (bi, i, 0, 0)),
        ],
        out_shape=[
            jax.ShapeDtypeStruct((b, d, s), BF16),
            jax.ShapeDtypeStruct((b, s, d), BF16),
            jax.ShapeDtypeStruct((b, d, s), BF16),
            jax.ShapeDtypeStruct((b, s // tm, nblk, d), F32),
        ],
        compiler_params=_cparams(("parallel", "parallel")),
        name="qkv_proj",
    )(x3, g.reshape(1, d), wqt_bf16, wk_bf16, wvt_bf16)


def _moba_kernel(qt_ref, k_ref, vt_ref, km_ref, tbl_ref, o_ref, sel_scr, ss_scr, p_scr, acc_scr, *, nb):
    blk = MOBA_BLOCK
    i = pl.program_id(2)
    jp = jnp.maximum(i - 1, 0)
    qt = qt_ref[...]
    frow = lax.broadcasted_iota(jnp.int32, qt.shape, 0)
    blk_id = lax.broadcasted_iota(jnp.int32, (nb, blk), 0)
    valid = blk_id < i
    km = km_ref[...]

    qh = []
    for hh in range(2):
        qh_t = jnp.where(frow // HEAD_DIM == hh, qt, jnp.zeros_like(qt))
        qh.append(qh_t)
        gate = jnp.dot(km, qh_t.astype(F32), precision=lax.Precision.HIGHEST,
                       preferred_element_type=F32)
        gate = jnp.where(valid, gate, NEG)
        g = gate
        for _ in range(MOBA_TOPK - 1):
            g = jnp.where(g >= jnp.max(g, axis=0, keepdims=True), NEG, g)
        tau = jnp.max(g, axis=0, keepdims=True)
        sel = jnp.where(valid, jnp.where(gate >= tau, 0.0, NEG), NEG)
        for qs in range(blk // MOBA_QLANES):
            sel_scr[hh, qs, 0:nb, :] = sel[:, qs * MOBA_QLANES:(qs + 1) * MOBA_QLANES]

    def k_block(j):
        return k_ref[pl.ds(pl.multiple_of(j * blk, blk), blk), :]

    chains = [(hh, qs) for hh in range(2) for qs in range(blk // MOBA_QLANES)]

    def qcols(qs):
        return slice(qs * MOBA_QLANES, (qs + 1) * MOBA_QLANES)

    def scores(kj, hh, qs):
        return jnp.dot(kj, qh[hh][:, qcols(qs)], preferred_element_type=F32)

    def sel_row(hh, qs, j):
        return sel_scr[hh, qs, pl.ds(j, 1), :]

    def pv(hh, p, j):
        vj = vt_ref[hh * HEAD_DIM:(hh + 1) * HEAD_DIM, pl.ds(pl.multiple_of(j * blk, blk), blk)]
        return jnp.dot(vj, p.astype(BF16), preferred_element_type=F32)

    def softmax_step(m_prev, l_prev, s):
        m_new = jnp.maximum(m_prev, jnp.max(s, axis=0, keepdims=True))
        alpha = jnp.exp(m_prev - m_new)
        p = jnp.exp(s - m_new)
        l_new = alpha * l_prev + jnp.sum(p, axis=0, keepdims=True)
        return m_new, l_new, alpha, p.astype(BF16)

    def far_scores(b, buf):
        kj = k_block(jnp.minimum(b, i))
        row = jnp.where(b < jp, b, nb)
        for c, (hh, qs) in enumerate(chains):
            ss_scr[buf, c] = scores(kj, hh, qs) + sel_row(hh, qs, row)

    k_own, k_prev = k_block(i), k_block(jp)
    ss = [scores(k_own, hh, qs) + tbl_ref[hh, 0, :, qcols(qs)] for hh, qs in chains]
    ss_prev = [scores(k_prev, hh, qs) + tbl_ref[hh, 1, :, qcols(qs)] + sel_row(hh, qs, jp)
               for hh, qs in chains]
    ms = [jnp.max(s, axis=0, keepdims=True) for s in ss]
    ps = [jnp.exp(s - m) for s, m in zip(ss, ms)]
    ls = [jnp.sum(p, axis=0, keepdims=True) for p in ps]
    accs = [pv(hh, p, i) for (hh, _), p in zip(chains, ps)]
    steps = [softmax_step(m, l, s) for m, l, s in zip(ms, ls, ss_prev)]
    accs = [alpha * acc + pv(hh, p, jp) for (hh, _), acc, (_, _, alpha, p) in zip(chains, accs, steps)]
    ms = [st[0] for st in steps]
    ls = [st[1] for st in steps]

    def v_block(b):
        return jnp.clip(b, 0, i)

    def half(b, cur, ms, ls):
        nxt = 1 - cur
        far_scores(b + 1, nxt)
        pvs = [pv(hh, p_scr[nxt, c], v_block(b - 1)) for c, (hh, _) in enumerate(chains)]
        new_ms, new_ls = [], []
        for c in range(len(chains)):
            m_new, l_new, alpha, p = softmax_step(ms[c], ls[c], ss_scr[cur, c])
            p_scr[cur, c] = p
            acc_scr[c] = alpha * (acc_scr[c] + pvs[c])
            new_ms.append(m_new)
            new_ls.append(l_new)
        return tuple(new_ms), tuple(new_ls)

    def two_blocks(t, state):
        ms, ls = half(2 * t, 0, *state)
        return half(2 * t + 1, 1, ms, ls)

    for c in range(len(chains)):
        acc_scr[c] = accs[c]
        p_scr[1, c] = jnp.zeros((blk, MOBA_QLANES), BF16)
        sel_scr[chains[c][0], chains[c][1], nb:nb + SUBLANES, :] = jnp.full((SUBLANES, MOBA_QLANES), NEG, F32)
    far_scores(0, 0)
    n_pairs = (jp + 1) // 2
    ms, ls = lax.fori_loop(0, n_pairs, two_blocks, (tuple(ms), tuple(ls)))
    accs = [acc_scr[c] + pv(hh, p_scr[1, c], v_block(2 * n_pairs - 1)) for c, (hh, _) in enumerate(chains)]
    outs = [acc / l for l, acc in zip(ls, accs)]
    nqs = blk // MOBA_QLANES
    o_t = jnp.concatenate([jnp.concatenate(outs[hh * nqs:(hh + 1) * nqs], axis=1) for hh in range(2)],
                          axis=0)
    o_ref[...] = o_t.T.astype(o_ref.dtype)


def _t5_bucket_np(dist):
    max_exact = REL_BUCKETS // 2
    d = np.maximum(dist, 0)
    df = np.maximum(d, 1).astype(np.float32)
    large = max_exact + (np.log(df / max_exact) / math.log(REL_MAX_DIST / max_exact)
                         * (REL_BUCKETS - max_exact)).astype(np.int32)
    large = np.minimum(large, REL_BUCKETS - 1)
    return np.where(d < max_exact, d, large)


def _moba_bias_tables(rel_bias):
    blk = MOBA_BLOCK
    assert int(_t5_bucket_np(np.array([blk + 1]))[0]) == REL_BUCKETS - 1
    dist = np.arange(-(blk - 1), blk)

    def by_distance(d):
        onehot = (_t5_bucket_np(d)[None, :] == np.arange(REL_BUCKETS)[:, None]).astype(np.float32)
        return jnp.dot(rel_bias.astype(F32), onehot, precision=lax.Precision.HIGHEST)

    def toeplitz(x):
        h = x.shape[0]
        y = jnp.concatenate([x, jnp.zeros((h, 1), x.dtype)], axis=1)
        skew = jnp.tile(y, (1, blk))[:, :blk * (2 * blk - 1)].reshape(h, blk, 2 * blk - 1)
        return skew[:, :, blk - 1:]

    far = rel_bias[:, REL_BUCKETS - 1].astype(F32)[:, None, None]
    causal = (np.arange(blk)[None, :] >= np.arange(blk)[:, None])[None]
    own = jnp.where(causal, toeplitz(by_distance(dist)) - far, NEG)
    prev = toeplitz(by_distance(dist + blk)) - far
    return jnp.stack([own, prev], axis=1)


def _moba_attention(qt, k, vt, kmean, tbl):
    b, s, d = k.shape
    blk = MOBA_BLOCK
    nb = s // blk
    nhp = d // LANES
    nqs = blk // MOBA_QLANES
    return pl.pallas_call(
        functools.partial(_moba_kernel, nb=nb),
        grid=(b, nhp, nb),
        in_specs=[
            pl.BlockSpec((None, LANES, blk), lambda bi, hp, i: (bi, hp, i)),
            pl.BlockSpec((None, s, LANES), lambda bi, hp, i: (bi, 0, hp)),
            pl.BlockSpec((None, LANES, s), lambda bi, hp, i: (bi, hp, 0)),
            pl.BlockSpec((None, nb, LANES), lambda bi, hp, i: (bi, 0, hp)),
            pl.BlockSpec((2, 2, blk, blk), lambda bi, hp, i: (hp, 0, 0, 0)),
        ],
        out_specs=pl.BlockSpec((None, blk, LANES), lambda bi, hp, i: (bi, i, hp)),
        out_shape=jax.ShapeDtypeStruct((b, s, d), BF16),
        scratch_shapes=[
            pltpu.VMEM((2, nqs, nb + SUBLANES, MOBA_QLANES), F32),
            pltpu.VMEM((2, 2 * nqs, blk, MOBA_QLANES), F32),
            pltpu.VMEM((2, 2 * nqs, blk, MOBA_QLANES), BF16),
            pltpu.VMEM((2 * nqs, HEAD_DIM, MOBA_QLANES), F32),
        ],
        compiler_params=_cparams(("parallel", "parallel", "arbitrary")),
        name="moba_attention",
    )(qt, k, vt, kmean.reshape(b, nb, d), tbl)


def _peerq_epilogue(x_new, gf_ref, wpq_ref, x_out_ref, h_ref, pq_ref):
    x_out_ref[...] = x_new
    h = _rms(x_new, gf_ref[...]).astype(BF16)
    h_ref[...] = h
    pq_ref[...] = jnp.dot(h, wpq_ref[...], preferred_element_type=F32)


def _attn_out_kernel(o_ref, wo_ref, x_ref, gf_ref, wpq_ref, x_out_ref, h_ref, pq_ref):
    x_new = x_ref[...] + jnp.dot(o_ref[...], wo_ref[...], preferred_element_type=F32)
    _peerq_epilogue(x_new, gf_ref, wpq_ref, x_out_ref, h_ref, pq_ref)


def _attn_out_proj(o2d, wo_bf16, x2d, gf, wpq_bf16):
    t, d = x2d.shape
    nq = wpq_bf16.shape[1]
    tm = TM_PROJ
    return pl.pallas_call(
        _attn_out_kernel,
        grid=(t // tm,),
        in_specs=[
            pl.BlockSpec((tm, d), lambda i: (i, 0)),
            pl.BlockSpec((d, d), lambda i: (0, 0)),
            pl.BlockSpec((tm, d), lambda i: (i, 0)),
            pl.BlockSpec((1, d), lambda i: (0, 0)),
            pl.BlockSpec((d, nq), lambda i: (0, 0)),
        ],
        out_specs=[
            pl.BlockSpec((tm, d), lambda i: (i, 0)),
            pl.BlockSpec((tm, d), lambda i: (i, 0)),
            pl.BlockSpec((tm, nq), lambda i: (i, 0)),
        ],
        out_shape=[
            jax.ShapeDtypeStruct((t, d), F32),
            jax.ShapeDtypeStruct((t, d), BF16),
            jax.ShapeDtypeStruct((t, nq), F32),
        ],
        compiler_params=_cparams(("parallel",)),
        name="attn_out_proj",
    )(o2d, wo_bf16, x2d, gf.reshape(1, d), wpq_bf16)


def _cmpx(vals, a, b):
    hi = jnp.maximum(vals[a], vals[b])
    lo = jnp.minimum(vals[a], vals[b])
    vals[a], vals[b] = hi, lo


def _bitonic_merge_desc(vals):
    n = len(vals)
    dist = n // 2
    while dist >= 1:
        for s in range(0, n, 2 * dist):
            for t in range(s, s + dist):
                _cmpx(vals, t, t + dist)
        dist //= 2
    return vals


def _sort_desc(vals):
    n = len(vals)
    if n == 1:
        return vals
    top = _sort_desc(vals[: n // 2])
    bot = _sort_desc(vals[n // 2:])
    return _bitonic_merge_desc(top + bot[::-1])


def _top_merge(a, b):
    n = len(a)
    return _bitonic_merge_desc([jnp.maximum(a[r], b[n - 1 - r]) for r in range(n)])


def _sorted_top16(vals):
    groups = [_sort_desc(vals[g:g + PEER_TOPK]) for g in range(0, len(vals), PEER_TOPK)]
    while len(groups) > 1:
        groups = [_top_merge(groups[g], groups[g + 1]) for g in range(0, len(groups), 2)]
    return groups[0]


def _router_kernel(pq_ref, keys_ref, st_ref, sc_ref, scr_ref):
    nch = TT_ROUTER // LANES
    tops = []
    for c in range(2):
        qc = pq_ref[:, c * PEER_HALF:(c + 1) * PEER_HALF]
        s_t = lax.dot_general(keys_ref[0, c], qc, NT_DIMS,
                              precision=lax.Precision.HIGHEST,
                              preferred_element_type=F32)
        st_ref[0, c] = s_t
        for ch in range(nch):
            scr_ref[c, pl.ds(ch, PEER_NKEYS, stride=nch), :] = s_t[:, ch * LANES:(ch + 1) * LANES]
        vals = [scr_ref[c, k * nch:(k + 1) * nch, :] for k in range(PEER_NKEYS)]
        tops.append(_sorted_top16(vals))
    a, b = tops
    cands = [a[r] + b[c] for r in range(PEER_TOPK) for c in range(PEER_TOPK)
             if (r + 1) * (c + 1) <= PEER_TOPK]
    npad = 1 << (len(cands) - 1).bit_length()
    cands = cands + [jnp.full_like(cands[0], NEG)] * (npad - len(cands))
    best = _sort_desc(cands)[:PEER_TOPK]
    z = jnp.ones_like(best[0])
    for r in range(1, PEER_TOPK):
        z = z + jnp.exp(best[r] - best[0])
    for r in range(PEER_TOPK):
        sc_ref[0, SC_TOP1 + r] = a[r]
        sc_ref[0, SC_TOP2 + r] = b[r]
    sc_ref[0, SC_TAU] = best[PEER_TOPK - 1]
    sc_ref[0, SC_ZINV] = 1.0 / z


def _peer_router(pq, sub_keys):
    t = pq.shape[0]
    tt = TT_ROUTER
    nch = tt // LANES
    hp = PEER_HEADS
    return pl.pallas_call(
        _router_kernel,
        grid=(t // tt, hp),
        in_specs=[
            pl.BlockSpec((tt, 2 * PEER_HALF), lambda i, h: (i, h)),
            pl.BlockSpec((1, 2, PEER_NKEYS, PEER_HALF), lambda i, h: (h, 0, 0, 0)),
        ],
        out_specs=[
            pl.BlockSpec((1, 2, PEER_NKEYS, tt), lambda i, h: (h, 0, 0, i)),
            pl.BlockSpec((1, SC_ROWS, nch, LANES), lambda i, h: (h, 0, i, 0)),
        ],
        out_shape=[
            jax.ShapeDtypeStruct((hp, 2, PEER_NKEYS, t), F32),
            jax.ShapeDtypeStruct((hp, SC_ROWS, t // LANES, LANES), F32),
        ],
        scratch_shapes=[pltpu.VMEM((2, PEER_NKEYS * nch, LANES), F32)],
        compiler_params=_cparams(("parallel", "parallel")),
        name="peer_router",
    )(pq, sub_keys)


def _gelu2(x):
    c0 = math.sqrt(2.0 / math.pi)
    z = x * (c0 + (c0 * 0.044715) * (x * x))
    return x + x * jnp.tanh(z)


def _dup_bf16_bits(v):
    bits = pltpu.bitcast(v.astype(BF16).astype(F32), jnp.uint32)
    return bits | (bits >> 16)


def _peer_kernel(h_ref, st_ref, sc_ref, u_ref, vt_ref, x_ref, gfin_ref, o_ref,
                 rank_scr, bw_scr, cnt_scr, aw_scr, act_scr, wa_scr, acc_scr, *, final_norm):
    e = pl.program_id(1)
    tt = h_ref.shape[0]
    nsub = CE_STEP // CE_SUB
    ipc = CE_SUB // PEER_NKEYS
    pack = 2 * SUBLANES

    @pl.when(e == 0)
    def _():
        acc_scr[...] = jnp.zeros_like(acc_scr)
        for hd in range(PEER_HEADS):
            s1 = st_ref[hd, 0]
            s2 = st_ref[hd, 1]
            tau = sc_ref[hd, SC_TAU:SC_TAU + 1, :]
            rank2 = jnp.zeros_like(s2)
            cnt = jnp.zeros_like(s1)
            for c in range(PEER_TOPK):
                b_c = sc_ref[hd, SC_TOP2 + c:SC_TOP2 + c + 1, :]
                rank2 = jnp.where(b_c > s2, c + 1.0, rank2)
                cnt = jnp.where(s1 + b_c >= tau, c + 1.0, cnt)
            a0 = sc_ref[hd, SC_TOP1:SC_TOP1 + 1, :]
            b0 = sc_ref[hd, SC_TOP2:SC_TOP2 + 1, :]
            zinv = sc_ref[hd, SC_ZINV:SC_ZINV + 1, :]
            rank_scr[hd] = rank2.astype(BF16)
            bw_scr[hd] = (jnp.exp(s2 - b0) * (0.5 * zinv)).astype(BF16)
            cnt_w = _dup_bf16_bits(cnt)
            a_w = _dup_bf16_bits(jnp.exp(s1 - a0))
            for lg in range(tt // LANES):
                cnt_scr[hd, lg] = cnt_w[:, lg * LANES:(lg + 1) * LANES]
                aw_scr[hd, lg] = a_w[:, lg * LANES:(lg + 1) * LANES]

    def row_bf16(tile, r):
        words = jnp.broadcast_to(tile[r:r + 1, :], (SUBLANES, LANES))
        return pltpu.bitcast(words, BF16)

    def up_proj(c):
        act_scr[c % 2] = lax.dot_general(u_ref[c * CE_SUB:(c + 1) * CE_SUB, :], h_ref[...], NT_DIMS,
                                         preferred_element_type=F32)

    def down_proj(c):
        acc_scr[...] += jnp.dot(vt_ref[:, c * CE_SUB:(c + 1) * CE_SUB], wa_scr[c % 2],
                                preferred_element_type=F32)

    def gate(c):
        njg = PEER_NKEYS // pack
        zero = jnp.zeros((pack, LANES), BF16)
        for lg in range(tt // LANES):
            lanes = slice(lg * LANES, (lg + 1) * LANES)
            w = [[None] * njg for _ in range(ipc)]
            for hd in range(PEER_HEADS):
                rank = [rank_scr[hd, jg * pack:(jg + 1) * pack, lanes] for jg in range(njg)]
                bw = [bw_scr[hd, jg * pack:(jg + 1) * pack, lanes] for jg in range(njg)]
                first = (e * nsub + c) * ipc
                base = pl.multiple_of((first // SUBLANES) * SUBLANES, SUBLANES)
                off = (c * ipc) % SUBLANES
                cnt_tile = cnt_scr[hd, lg, pl.ds(base, SUBLANES), :]
                a_tile = aw_scr[hd, lg, pl.ds(base, SUBLANES), :]
                for il in range(ipc):
                    cnt_row = row_bf16(cnt_tile, off + il)
                    a_row = row_bf16(a_tile, off + il)
                    for jg in range(njg):
                        term = jnp.where(rank[jg] < cnt_row, a_row * bw[jg], zero)
                        w[il][jg] = term if w[il][jg] is None else w[il][jg] + term
            for il in range(ipc):
                for jg in range(njg):
                    rows = slice(il * PEER_NKEYS + jg * pack, il * PEER_NKEYS + (jg + 1) * pack)
                    wa_scr[c % 2, rows, lanes] = w[il][jg] * _gelu2(act_scr[c % 2, rows, lanes]).astype(BF16)

    up_proj(0)
    for c in range(nsub):
        if c + 1 < nsub:
            up_proj(c + 1)
        if c >= 1:
            down_proj(c - 1)
        gate(c)
    down_proj(nsub - 1)

    @pl.when(e == pl.num_programs(1) - 1)
    def _():
        y = x_ref[...] + acc_scr[...].T
        if final_norm:
            y = _rms(y, gfin_ref[...])
        o_ref[...] = y


def _peer_experts(h_bf16, st, sc, u_bf16, vt_bf16, x2d, gfin, final_norm):
    t, d = x2d.shape
    ne = u_bf16.shape[0]
    tt, ce = TT_PEER, CE_STEP
    hp = PEER_HEADS
    assert (ce // PEER_NKEYS) % SUBLANES == 0 and SUBLANES % (CE_SUB // PEER_NKEYS) == 0
    return pl.pallas_call(
        functools.partial(_peer_kernel, final_norm=final_norm),
        grid=(t // tt, ne // ce),
        in_specs=[
            pl.BlockSpec((tt, d), lambda i, e: (i, 0)),
            pl.BlockSpec((hp, 2, PEER_NKEYS, tt), lambda i, e: (0, 0, 0, i)),
            pl.BlockSpec((hp, SC_ROWS, tt), lambda i, e: (0, 0, i)),
            pl.BlockSpec((ce, d), lambda i, e: (e, 0)),
            pl.BlockSpec((None, d, ce), lambda i, e: (e, 0, 0)),
            pl.BlockSpec((tt, d), lambda i, e: (i, 0)),
            pl.BlockSpec((1, d), lambda i, e: (0, 0)),
        ],
        out_specs=pl.BlockSpec((tt, d), lambda i, e: (i, 0)),
        out_shape=jax.ShapeDtypeStruct((t, d), F32),
        scratch_shapes=[
            pltpu.VMEM((hp, PEER_NKEYS, tt), BF16),
            pltpu.VMEM((hp, PEER_NKEYS, tt), BF16),
            pltpu.VMEM((hp, tt // LANES, PEER_NKEYS, LANES), jnp.uint32),
            pltpu.VMEM((hp, tt // LANES, PEER_NKEYS, LANES), jnp.uint32),
            pltpu.VMEM((2, CE_SUB, tt), F32),
            pltpu.VMEM((2, CE_SUB, tt), BF16),
            pltpu.VMEM((d, tt), F32),
        ],
        compiler_params=_cparams(("parallel", "arbitrary")),
        name="peer_experts",
    )(h_bf16, st, sc.reshape(hp, SC_ROWS, t), u_bf16, vt_bf16, x2d, gfin.reshape(1, d))


def _expert_major_t(v):
    ne, d = v.shape
    return v.astype(BF16).reshape(ne // CE_STEP, CE_STEP, d).transpose(0, 2, 1)


def _peer(h_bf16, pq, sub_keys, u_bf16, vt_bf16, x2d, gfin, final_norm):
    st, sc = _peer_router(pq, sub_keys)
    return _peer_experts(h_bf16, st, sc, u_bf16, vt_bf16, x2d, gfin, final_norm)


def _glu_kernel(x_ref, g_ref, w_ref, b_ref, u_ref, *, d):
    h = _rms(x_ref[...], g_ref[...]).astype(BF16)
    a = jnp.dot(h, w_ref[...], preferred_element_type=F32) + b_ref[...]
    u_ref[...] = a[:, :d] * jax.nn.sigmoid(a[:, d:])


def _glu_proj(x2d, g, w_bf16, bias):
    t, d = x2d.shape
    tm = TM_PROJ
    return pl.pallas_call(
        functools.partial(_glu_kernel, d=d),
        grid=(t // tm,),
        in_specs=[
            pl.BlockSpec((tm, d), lambda i: (i, 0)),
            pl.BlockSpec((1, d), lambda i: (0, 0)),
            pl.BlockSpec((d, 2 * d), lambda i: (0, 0)),
            pl.BlockSpec((1, 2 * d), lambda i: (0, 0)),
        ],
        out_specs=pl.BlockSpec((tm, d), lambda i: (i, 0)),
        out_shape=jax.ShapeDtypeStruct((t, d), F32),
        compiler_params=_cparams(("parallel",)),
        name="glu_proj",
    )(x2d, g.reshape(1, d), w_bf16, bias.reshape(1, 2 * d))


def _conv_kernel(ucur_ref, uprev_ref, wdw_ref, bdw_ref, lng_ref, lnb_ref, w2_ref, b2_ref,
                 x_ref, gf_ref, wpq_ref, x_out_ref, h_ref, pq_ref, ext_scr, y_scr):
    tm, d = ucur_ref.shape
    first = pl.program_id(1) == 0
    ext_scr[0:CONV_HALO, :] = jnp.where(first, 0.0, uprev_ref[...])
    ext_scr[CONV_HALO:, :] = ucur_ref[...]
    lead = CONV_HALO - (CONV_WIDTH - 1)

    def chunk(ci, carry):
        r0 = pl.multiple_of(ci * CONV_ROWS, CONV_ROWS)
        window = ext_scr[pl.ds(r0, CONV_ROWS + CONV_HALO), :]
        acc = jnp.broadcast_to(bdw_ref[...], (CONV_ROWS, d))
        for w in range(CONV_WIDTH):
            acc = acc + wdw_ref[w:w + 1, :] * window[lead + w:lead + w + CONV_ROWS, :]
        y_scr[pl.ds(r0, CONV_ROWS), :] = acc
        return carry

    lax.fori_loop(0, tm // CONV_ROWS, chunk, 0)
    y = y_scr[...]
    mu = jnp.mean(y, axis=-1, keepdims=True)
    var = jnp.mean(jnp.square(y - mu), axis=-1, keepdims=True)
    y = (y - mu) * lax.rsqrt(var + EPS) * lng_ref[...] + lnb_ref[...]
    y = y * jax.nn.sigmoid(y)
    x_new = x_ref[...] + jnp.dot(y.astype(BF16), w2_ref[...], preferred_element_type=F32) + b2_ref[...]
    _peerq_epilogue(x_new, gf_ref, wpq_ref, x_out_ref, h_ref, pq_ref)


def _conv_tail(u3, wdw, bdw, lng, lnb, w2_bf16, b2, x3, gf, wpq_bf16):
    b, s, d = x3.shape
    nq = wpq_bf16.shape[1]
    tm = TM_CONV
    hpb = tm // CONV_HALO
    row = lambda v: v.reshape(1, d)
    tok = pl.BlockSpec((None, tm, d), lambda bi, i: (bi, i, 0))
    vec = pl.BlockSpec((1, d), lambda bi, i: (0, 0))
    return pl.pallas_call(
        _conv_kernel,
        grid=(b, s // tm),
        in_specs=[
            tok,
            pl.BlockSpec((None, CONV_HALO, d), lambda bi, i: (bi, jnp.maximum(i * hpb - 1, 0), 0)),
            pl.BlockSpec((CONV_WIDTH, d), lambda bi, i: (0, 0)),
            vec, vec, vec,
            pl.BlockSpec((d, d), lambda bi, i: (0, 0)),
            vec,
            tok,
            vec,
            pl.BlockSpec((d, nq), lambda bi, i: (0, 0)),
        ],
        out_specs=[
            tok,
            tok,
            pl.BlockSpec((None, tm, nq), lambda bi, i: (bi, i, 0)),
        ],
        out_shape=[
            jax.ShapeDtypeStruct((b, s, d), F32),
            jax.ShapeDtypeStruct((b, s, d), BF16),
            jax.ShapeDtypeStruct((b, s, nq), F32),
        ],
        scratch_shapes=[
            pltpu.VMEM((tm + CONV_HALO, d), F32),
            pltpu.VMEM((tm, d), F32),
        ],
        compiler_params=_cparams(("parallel", "arbitrary")),
        name="conv_tail",
    )(u3, u3, wdw, row(bdw), row(lng), row(lnb), w2_bf16, row(b2), x3, row(gf), wpq_bf16)


def kernel(x, rel_bias, norm_mix, norm_ffn, attn_w_qkv, attn_w_o, conv_w_pw1, conv_b_pw1,
           conv_w_dw, conv_b_dw, conv_ln_g, conv_ln_b, conv_w_pw2, conv_b_pw2, peer_w_q,
           peer_sub_keys, peer_u, peer_v, norm_final):
    b, s, d = x.shape
    t = b * s
    depth = norm_mix.shape[0]
    assert d == N_HEADS * HEAD_DIM and s % MOBA_BLOCK == 0 and s // MOBA_BLOCK <= LANES
    assert t % TT_ROUTER == 0 and s % TM_CONV == 0 and t % TM_PROJ == 0
    x2d = x.reshape(t, d)
    for i in range(depth):
        j = i // 2
        wpq = peer_w_q[i].astype(BF16)
        if i % 2 == 0:
            wq, wk, wv = (attn_w_qkv[j][:, c * d:(c + 1) * d] for c in range(3))
            qt, k, vt, kmean = _qkv_proj(x2d.reshape(b, s, d), norm_mix[i], wq.T.astype(BF16),
                                         wk.astype(BF16), wv.T.astype(BF16))
            o = _moba_attention(qt, k, vt, kmean, _moba_bias_tables(rel_bias))
            x2d, h, pq = _attn_out_proj(o.reshape(t, d), attn_w_o[j].astype(BF16), x2d,
                                        norm_ffn[i], wpq)
        else:
            u = _glu_proj(x2d, norm_mix[i], conv_w_pw1[j].astype(BF16), conv_b_pw1[j])
            x3, h3, pq3 = _conv_tail(u.reshape(b, s, d), conv_w_dw[j], conv_b_dw[j], conv_ln_g[j],
                                     conv_ln_b[j], conv_w_pw2[j].astype(BF16), conv_b_pw2[j],
                                     x2d.reshape(b, s, d), norm_ffn[i], wpq)
            x2d, h, pq = x3.reshape(t, d), h3.reshape(t, d), pq3.reshape(t, -1)
        last = i == depth - 1
        x2d = _peer(h, pq, peer_sub_keys[i], peer_u[i].astype(BF16),
                    _expert_major_t(peer_v[i]), x2d, norm_final, final_norm=last)
    return x2d.reshape(b, s, d)
```

```python
import functools
import math

import numpy as np
import jax
import jax.numpy as jnp
from jax import lax
from jax.experimental import pallas as pl
from jax.experimental.pallas import tpu as pltpu

F32 = jnp.float32
BF16 = jnp.bfloat16
EPS = 1e-6
NEG = -1e30

N_HEADS = 16
HEAD_DIM = 64
MOBA_BLOCK = 256
MOBA_TOPK = 3
REL_BUCKETS = 32
REL_MAX_DIST = 128
CONV_WIDTH = 31
PEER_HEADS = 8
PEER_NKEYS = 128
PEER_HALF = 128
PEER_TOPK = 16
SC_TOP1, SC_TOP2, SC_TAU, SC_ZINV, SC_ROWS = 0, PEER_TOPK, 2 * PEER_TOPK, 2 * PEER_TOPK + 1, 2 * PEER_TOPK + 2

LANES = 128
SUBLANES = 8
VMEM_LIMIT = 60 * 1024 * 1024

MOBA_QLANES = 128
TM_PROJ = 512
TT_ROUTER = 1024
TT_PEER = 512
CE_STEP = 2048
CE_SUB = 512
TM_CONV = 512
CONV_HALO = 32
CONV_ROWS = 16

NT_DIMS = (((1,), (1,)), ((), ()))


def _cparams(sem):
    return pltpu.CompilerParams(dimension_semantics=sem, vmem_limit_bytes=VMEM_LIMIT)


def _rms(x, g):
    return x * lax.rsqrt(jnp.mean(x * x, axis=-1, keepdims=True) + EPS) * g


def _qkv_kernel(x_ref, g_ref, wqt_ref, wk_ref, wvt_ref, qt_ref, k_ref, vt_ref, km_ref):
    h = _rms(x_ref[...], g_ref[...]).astype(BF16)
    scale = HEAD_DIM ** -0.5 * math.log2(math.e)
    qt = lax.dot_general(wqt_ref[...], h, NT_DIMS, preferred_element_type=F32)
    qt_ref[...] = (qt * scale).astype(BF16)
    k = jnp.dot(h, wk_ref[...], preferred_element_type=F32)
    k_ref[...] = k.astype(BF16)
    vt = lax.dot_general(wvt_ref[...], h, NT_DIMS, preferred_element_type=F32)
    vt_ref[...] = vt.astype(BF16)
    nblk = k.shape[0] // MOBA_BLOCK
    km_ref[...] = jnp.mean(k.reshape(nblk, MOBA_BLOCK, k.shape[1]), axis=1)


def _qkv_proj(x3, g, wqt_bf16, wk_bf16, wvt_bf16):
    b, s, d = x3.shape
    tm = TM_PROJ
    nblk = tm // MOBA_BLOCK
    nb = s // MOBA_BLOCK
    wspec = pl.BlockSpec((d, d), lambda bi, i: (0, 0))
    tspec = pl.BlockSpec((None, d, tm), lambda bi, i: (bi, 0, i))
    return pl.pallas_call(
        _qkv_kernel,
        grid=(b, s // tm),
        in_specs=[
            pl.BlockSpec((None, tm, d), lambda bi, i: (bi, i, 0)),
            pl.BlockSpec((1, d), lambda bi, i: (0, 0)),
            wspec, wspec, wspec,
        ],
        out_specs=[
            tspec,
            pl.BlockSpec((None, tm, d), lambda bi, i: (bi, i, 0)),
            tspec,
            pl.BlockSpec((None, None, nblk, d), lambda bi, i: (bi, i, 0, 0)),
        ],
        out_shape=[
            jax.ShapeDtypeStruct((b, d, s), BF16),
            jax.ShapeDtypeStruct((b, s, d), BF16),
            jax.ShapeDtypeStruct((b, d, s), BF16),
            jax.ShapeDtypeStruct((b, s // tm, nblk, d), F32),
        ],
        compiler_params=_cparams(("parallel", "parallel")),
        name="qkv_proj",
    )(x3, g.reshape(1, d), wqt_bf16, wk_bf16, wvt_bf16)


def _moba_kernel(qt_ref, k_ref, vt_ref, km_ref, tbl_ref, o_ref, sel_scr, ss_scr, p_scr, acc_scr, *, nb):
    blk = MOBA_BLOCK
    i = pl.program_id(2)
    jp = jnp.maximum(i - 1, 0)
    qt = qt_ref[...]
    frow = lax.broadcasted_iota(jnp.int32, qt.shape, 0)
    blk_id = lax.broadcasted_iota(jnp.int32, (nb, blk), 0)
    valid = blk_id < i
    km = km_ref[...]

    qh = []
    for hh in range(2):
        qh_t = jnp.where(frow // HEAD_DIM == hh, qt, jnp.zeros_like(qt))
        qh.append(qh_t)
        gate = jnp.dot(km, qh_t.astype(F32), precision=lax.Precision.HIGHEST,
                       preferred_element_type=F32)
        gate = jnp.where(valid, gate, NEG)
        g = gate
        for _ in range(MOBA_TOPK - 1):
            g = jnp.where(g >= jnp.max(g, axis=0, keepdims=True), NEG, g)
        tau = jnp.max(g, axis=0, keepdims=True)
        sel = jnp.where(valid, jnp.where(gate >= tau, 0.0, NEG), NEG)
        for qs in range(blk // MOBA_QLANES):
            sel_scr[hh, qs, 0:nb, :] = sel[:, qs * MOBA_QLANES:(qs + 1) * MOBA_QLANES]

    def k_block(j):
        return k_ref[pl.ds(pl.multiple_of(j * blk, blk), blk), :]

    chains = [(hh, qs) for hh in range(2) for qs in range(blk // MOBA_QLANES)]

    def qcols(qs):
        return slice(qs * MOBA_QLANES, (qs + 1) * MOBA_QLANES)

    def scores(kj, hh, qs):
        return jnp.dot(kj, qh[hh][:, qcols(qs)], preferred_element_type=F32)

    def sel_row(hh, qs, j):
        return sel_scr[hh, qs, pl.ds(j, 1), :]

    ones_rows = jnp.ones((2 * SUBLANES, blk), BF16)

    def pv(hh, p, j):
        vj = vt_ref[hh * HEAD_DIM:(hh + 1) * HEAD_DIM, pl.ds(pl.multiple_of(j * blk, blk), blk)]
        return jnp.dot(jnp.concatenate([vj, ones_rows], axis=0), p, preferred_element_type=F32)

    def softmax_step(m_prev, s):
        m_new = jnp.maximum(m_prev, jnp.max(s, axis=0, keepdims=True))
        alpha = jnp.exp2(m_prev - m_new)
        return m_new, alpha, jnp.exp2(s - m_new).astype(BF16)

    def far_scores(b, buf):
        kj = k_block(jnp.minimum(b, i))
        row = jnp.where(b < jp, b, nb)
        for c, (hh, qs) in enumerate(chains):
            ss_scr[buf, c] = scores(kj, hh, qs) + sel_row(hh, qs, row)

    k_own, k_prev = k_block(i), k_block(jp)
    ss = [scores(k_own, hh, qs) + tbl_ref[hh, 0, :, qcols(qs)] for hh, qs in chains]
    ss_prev = [scores(k_prev, hh, qs) + tbl_ref[hh, 1, :, qcols(qs)] + sel_row(hh, qs, jp)
               for hh, qs in chains]
    ms = [jnp.max(s, axis=0, keepdims=True) for s in ss]
    accs = [pv(hh, jnp.exp2(s - m).astype(BF16), i) for (hh, _), s, m in zip(chains, ss, ms)]
    steps = [softmax_step(m, s) for m, s in zip(ms, ss_prev)]
    accs = [alpha * acc + pv(hh, p, jp) for (hh, _), acc, (_, alpha, p) in zip(chains, accs, steps)]
    ms = [st[0] for st in steps]

    def v_block(b):
        return jnp.clip(b, 0, i)

    def half(b, cur, ms):
        nxt = 1 - cur
        far_scores(b + 1, nxt)
        pvs = [pv(hh, p_scr[nxt, c], v_block(b - 1)) for c, (hh, _) in enumerate(chains)]
        new_ms = []
        for c in range(len(chains)):
            m_new, alpha, p = softmax_step(ms[c], ss_scr[cur, c])
            p_scr[cur, c] = p
            acc_scr[c] = alpha * (acc_scr[c] + pvs[c])
            new_ms.append(m_new)
        return tuple(new_ms)

    def two_blocks(t, ms):
        return half(2 * t + 1, 1, half(2 * t, 0, ms))

    for c in range(len(chains)):
        acc_scr[c] = accs[c]
        p_scr[1, c] = jnp.zeros((blk, MOBA_QLANES), BF16)
        sel_scr[chains[c][0], chains[c][1], nb:nb + SUBLANES, :] = jnp.full((SUBLANES, MOBA_QLANES), NEG, F32)
    far_scores(0, 0)
    n_pairs = (jp + 1) // 2
    lax.fori_loop(0, n_pairs, two_blocks, tuple(ms))
    accs = [acc_scr[c] + pv(hh, p_scr[1, c], v_block(2 * n_pairs - 1)) for c, (hh, _) in enumerate(chains)]
    outs = [acc[:HEAD_DIM] / acc[HEAD_DIM:HEAD_DIM + 1] for acc in accs]
    nqs = blk // MOBA_QLANES
    o_t = jnp.concatenate([jnp.concatenate(outs[hh * nqs:(hh + 1) * nqs], axis=1) for hh in range(2)],
                          axis=0)
    o_ref[...] = o_t.T.astype(o_ref.dtype)


def _t5_bucket_np(dist):
    max_exact = REL_BUCKETS // 2
    d = np.maximum(dist, 0)
    df = np.maximum(d, 1).astype(np.float32)
    large = max_exact + (np.log(df / max_exact) / math.log(REL_MAX_DIST / max_exact)
                         * (REL_BUCKETS - max_exact)).astype(np.int32)
    large = np.minimum(large, REL_BUCKETS - 1)
    return np.where(d < max_exact, d, large)


def _moba_bias_tables(rel_bias):
    blk = MOBA_BLOCK
    assert int(_t5_bucket_np(np.array([blk + 1]))[0]) == REL_BUCKETS - 1
    dist = np.arange(-(blk - 1), blk)

    def by_distance(d):
        onehot = (_t5_bucket_np(d)[None, :] == np.arange(REL_BUCKETS)[:, None]).astype(np.float32)
        return jnp.dot(rel_bias.astype(F32), onehot, precision=lax.Precision.HIGHEST)

    def toeplitz(x):
        h = x.shape[0]
        y = jnp.concatenate([x, jnp.zeros((h, 1), x.dtype)], axis=1)
        skew = jnp.tile(y, (1, blk))[:, :blk * (2 * blk - 1)].reshape(h, blk, 2 * blk - 1)
        return skew[:, :, blk - 1:]

    far = rel_bias[:, REL_BUCKETS - 1].astype(F32)[:, None, None]
    causal = (np.arange(blk)[None, :] >= np.arange(blk)[:, None])[None]
    log2e = math.log2(math.e)
    own = jnp.where(causal, (toeplitz(by_distance(dist)) - far) * log2e, NEG)
    prev = (toeplitz(by_distance(dist + blk)) - far) * log2e
    return jnp.stack([own, prev], axis=1)


def _moba_attention(qt, k, vt, kmean, tbl):
    b, s, d = k.shape
    blk = MOBA_BLOCK
    nb = s // blk
    nhp = d // LANES
    nqs = blk // MOBA_QLANES
    return pl.pallas_call(
        functools.partial(_moba_kernel, nb=nb),
        grid=(b, nhp, nb),
        in_specs=[
            pl.BlockSpec((None, LANES, blk), lambda bi, hp, i: (bi, hp, i)),
            pl.BlockSpec((None, s, LANES), lambda bi, hp, i: (bi, 0, hp)),
            pl.BlockSpec((None, LANES, s), lambda bi, hp, i: (bi, hp, 0)),
            pl.BlockSpec((None, nb, LANES), lambda bi, hp, i: (bi, 0, hp)),
            pl.BlockSpec((2, 2, blk, blk), lambda bi, hp, i: (hp, 0, 0, 0)),
        ],
        out_specs=pl.BlockSpec((None, blk, LANES), lambda bi, hp, i: (bi, i, hp)),
        out_shape=jax.ShapeDtypeStruct((b, s, d), BF16),
        scratch_shapes=[
            pltpu.VMEM((2, nqs, nb + SUBLANES, MOBA_QLANES), F32),
            pltpu.VMEM((2, 2 * nqs, blk, MOBA_QLANES), F32),
            pltpu.VMEM((2, 2 * nqs, blk, MOBA_QLANES), BF16),
            pltpu.VMEM((2 * nqs, HEAD_DIM + 2 * SUBLANES, MOBA_QLANES), F32),
        ],
        compiler_params=_cparams(("parallel", "parallel", "arbitrary")),
        name="moba_attention",
    )(qt, k, vt, kmean.reshape(b, nb, d), tbl)


def _peerq_epilogue(x_new, gf_ref, wpq_ref, x_out_ref, h_ref, pq_ref):
    x_out_ref[...] = x_new
    h = _rms(x_new, gf_ref[...]).astype(BF16)
    h_ref[...] = h
    pq_ref[...] = jnp.dot(h, wpq_ref[...], preferred_element_type=F32)


def _attn_out_kernel(o_ref, wo_ref, x_ref, gf_ref, wpq_ref, x_out_ref, h_ref, pq_ref):
    x_new = x_ref[...] + jnp.dot(o_ref[...], wo_ref[...], preferred_element_type=F32)
    _peerq_epilogue(x_new, gf_ref, wpq_ref, x_out_ref, h_ref, pq_ref)


def _attn_out_proj(o2d, wo_bf16, x2d, gf, wpq_bf16):
    t, d = x2d.shape
    nq = wpq_bf16.shape[1]
    tm = TM_PROJ
    return pl.pallas_call(
        _attn_out_kernel,
        grid=(t // tm,),
        in_specs=[
            pl.BlockSpec((tm, d), lambda i: (i, 0)),
            pl.BlockSpec((d, d), lambda i: (0, 0)),
            pl.BlockSpec((tm, d), lambda i: (i, 0)),
            pl.BlockSpec((1, d), lambda i: (0, 0)),
            pl.BlockSpec((d, nq), lambda i: (0, 0)),
        ],
        out_specs=[
            pl.BlockSpec((tm, d), lambda i: (i, 0)),
            pl.BlockSpec((tm, d), lambda i: (i, 0)),
            pl.BlockSpec((tm, nq), lambda i: (i, 0)),
        ],
        out_shape=[
            jax.ShapeDtypeStruct((t, d), F32),
            jax.ShapeDtypeStruct((t, d), BF16),
            jax.ShapeDtypeStruct((t, nq), F32),
        ],
        compiler_params=_cparams(("parallel",)),
        name="attn_out_proj",
    )(o2d, wo_bf16, x2d, gf.reshape(1, d), wpq_bf16)


def _cmpx(vals, a, b):
    hi = jnp.maximum(vals[a], vals[b])
    lo = jnp.minimum(vals[a], vals[b])
    vals[a], vals[b] = hi, lo


def _bitonic_merge_desc(vals):
    n = len(vals)
    dist = n // 2
    while dist >= 1:
        for s in range(0, n, 2 * dist):
            for t in range(s, s + dist):
                _cmpx(vals, t, t + dist)
        dist //= 2
    return vals


def _sort_desc(vals):
    n = len(vals)
    if n == 1:
        return vals
    top = _sort_desc(vals[: n // 2])
    bot = _sort_desc(vals[n // 2:])
    return _bitonic_merge_desc(top + bot[::-1])


def _top_merge(a, b):
    n = len(a)
    return _bitonic_merge_desc([jnp.maximum(a[r], b[n - 1 - r]) for r in range(n)])


def _sorted_top16(vals):
    groups = [_sort_desc(vals[g:g + PEER_TOPK]) for g in range(0, len(vals), PEER_TOPK)]
    while len(groups) > 1:
        groups = [_top_merge(groups[g], groups[g + 1]) for g in range(0, len(groups), 2)]
    return groups[0]


def _router_kernel(pq_ref, keys_ref, st_ref, sc_ref, scr_ref):
    nch = TT_ROUTER // LANES
    tops = []
    for c in range(2):
        qc = pq_ref[:, c * PEER_HALF:(c + 1) * PEER_HALF]
        s_t = lax.dot_general(keys_ref[0, c], qc, NT_DIMS,
                              precision=lax.Precision.HIGHEST,
                              preferred_element_type=F32)
        st_ref[0, c] = s_t
        for ch in range(nch):
            scr_ref[c, pl.ds(ch, PEER_NKEYS, stride=nch), :] = s_t[:, ch * LANES:(ch + 1) * LANES]
        vals = [scr_ref[c, k * nch:(k + 1) * nch, :] for k in range(PEER_NKEYS)]
        tops.append(_sorted_top16(vals))
    a, b = tops
    cands = [a[r] + b[c] for r in range(PEER_TOPK) for c in range(PEER_TOPK)
             if (r + 1) * (c + 1) <= PEER_TOPK]
    npad = 1 << (len(cands) - 1).bit_length()
    cands = cands + [jnp.full_like(cands[0], NEG)] * (npad - len(cands))
    best = _sort_desc(cands)[:PEER_TOPK]
    z = jnp.ones_like(best[0])
    for r in range(1, PEER_TOPK):
        z = z + jnp.exp(best[r] - best[0])
    for r in range(PEER_TOPK):
        sc_ref[0, SC_TOP1 + r] = a[r]
        sc_ref[0, SC_TOP2 + r] = b[r]
    sc_ref[0, SC_TAU] = best[PEER_TOPK - 1]
    sc_ref[0, SC_ZINV] = 1.0 / z


def _peer_router(pq, sub_keys):
    t = pq.shape[0]
    tt = TT_ROUTER
    nch = tt // LANES
    hp = PEER_HEADS
    return pl.pallas_call(
        _router_kernel,
        grid=(t // tt, hp),
        in_specs=[
            pl.BlockSpec((tt, 2 * PEER_HALF), lambda i, h: (i, h)),
            pl.BlockSpec((1, 2, PEER_NKEYS, PEER_HALF), lambda i, h: (h, 0, 0, 0)),
        ],
        out_specs=[
            pl.BlockSpec((1, 2, PEER_NKEYS, tt), lambda i, h: (h, 0, 0, i)),
            pl.BlockSpec((1, SC_ROWS, nch, LANES), lambda i, h: (h, 0, i, 0)),
        ],
        out_shape=[
            jax.ShapeDtypeStruct((hp, 2, PEER_NKEYS, t), F32),
            jax.ShapeDtypeStruct((hp, SC_ROWS, t // LANES, LANES), F32),
        ],
        scratch_shapes=[pltpu.VMEM((2, PEER_NKEYS * nch, LANES), F32)],
        compiler_params=_cparams(("parallel", "parallel")),
        name="peer_router",
    )(pq, sub_keys)


def _gelu2(x):
    c0 = math.sqrt(2.0 / math.pi)
    z = x * (c0 + (c0 * 0.044715) * (x * x))
    return x + x * jnp.tanh(z)


def _dup_bf16_bits(v):
    bits = pltpu.bitcast(v.astype(BF16).astype(F32), jnp.uint32)
    return bits | (bits >> 16)


def _peer_kernel(h_ref, st_ref, sc_ref, u_ref, vt_ref, x_ref, gfin_ref, o_ref,
                 rank_scr, bw_scr, cnt_scr, aw_scr, act_scr, wa_scr, acc_scr, *, final_norm):
    e = pl.program_id(1)
    tt = h_ref.shape[0]
    nsub = CE_STEP // CE_SUB
    ipc = CE_SUB // PEER_NKEYS
    pack = 2 * SUBLANES

    @pl.when(e == 0)
    def _():
        acc_scr[...] = jnp.zeros_like(acc_scr)
        for hd in range(PEER_HEADS):
            s1 = st_ref[hd, 0]
            s2 = st_ref[hd, 1]
            tau = sc_ref[hd, SC_TAU:SC_TAU + 1, :]
            rank2 = jnp.zeros_like(s2)
            cnt = jnp.zeros_like(s1)
            for c in range(PEER_TOPK):
                b_c = sc_ref[hd, SC_TOP2 + c:SC_TOP2 + c + 1, :]
                rank2 = jnp.where(b_c > s2, c + 1.0, rank2)
                cnt = jnp.where(s1 + b_c >= tau, c + 1.0, cnt)
            a0 = sc_ref[hd, SC_TOP1:SC_TOP1 + 1, :]
            b0 = sc_ref[hd, SC_TOP2:SC_TOP2 + 1, :]
            zinv = sc_ref[hd, SC_ZINV:SC_ZINV + 1, :]
            rank_scr[hd] = rank2.astype(BF16)
            bw_scr[hd] = (jnp.exp(s2 - b0) * (0.5 * zinv)).astype(BF16)
            cnt_w = _dup_bf16_bits(cnt)
            a_w = _dup_bf16_bits(jnp.exp(s1 - a0))
            for lg in range(tt // LANES):
                cnt_scr[hd, lg] = cnt_w[:, lg * LANES:(lg + 1) * LANES]
                aw_scr[hd, lg] = a_w[:, lg * LANES:(lg + 1) * LANES]

    def row_bf16(tile, r):
        words = jnp.broadcast_to(tile[r:r + 1, :], (SUBLANES, LANES))
        return pltpu.bitcast(words, BF16)

    def up_proj(c):
        act_scr[c % 2] = lax.dot_general(u_ref[c * CE_SUB:(c + 1) * CE_SUB, :], h_ref[...], NT_DIMS,
                                         preferred_element_type=F32)

    def down_proj(c):
        acc_scr[...] += jnp.dot(vt_ref[:, c * CE_SUB:(c + 1) * CE_SUB], wa_scr[c % 2],
                                preferred_element_type=F32)

    def gate(c):
        njg = PEER_NKEYS // pack
        zero = jnp.zeros((pack, LANES), BF16)
        for lg in range(tt // LANES):
            lanes = slice(lg * LANES, (lg + 1) * LANES)
            w = [[None] * njg for _ in range(ipc)]
            for hd in range(PEER_HEADS):
                rank = [rank_scr[hd, jg * pack:(jg + 1) * pack, lanes] for jg in range(njg)]
                bw = [bw_scr[hd, jg * pack:(jg + 1) * pack, lanes] for jg in range(njg)]
                first = (e * nsub + c) * ipc
                base = pl.multiple_of((first // SUBLANES) * SUBLANES, SUBLANES)
                off = (c * ipc) % SUBLANES
                cnt_tile = cnt_scr[hd, lg, pl.ds(base, SUBLANES), :]
                a_tile = aw_scr[hd, lg, pl.ds(base, SUBLANES), :]
                for il in range(ipc):
                    cnt_row = row_bf16(cnt_tile, off + il)
                    a_row = row_bf16(a_tile, off + il)
                    for jg in range(njg):
                        term = jnp.where(rank[jg] < cnt_row, a_row * bw[jg], zero)
                        w[il][jg] = term if w[il][jg] is None else w[il][jg] + term
            for il in range(ipc):
                for jg in range(njg):
                    rows = slice(il * PEER_NKEYS + jg * pack, il * PEER_NKEYS + (jg + 1) * pack)
                    wa_scr[c % 2, rows, lanes] = w[il][jg] * _gelu2(act_scr[c % 2, rows, lanes]).astype(BF16)

    up_proj(0)
    for c in range(nsub):
        if c + 1 < nsub:
            up_proj(c + 1)
        if c >= 1:
            down_proj(c - 1)
        gate(c)
    down_proj(nsub - 1)

    @pl.when(e == pl.num_programs(1) - 1)
    def _():
        y = x_ref[...] + acc_scr[...].T
        if final_norm:
            y = _rms(y, gfin_ref[...])
        o_ref[...] = y


def _peer_experts(h_bf16, st, sc, u_bf16, vt_bf16, x2d, gfin, final_norm):
    t, d = x2d.shape
    ne = u_bf16.shape[0]
    tt, ce = TT_PEER, CE_STEP
    hp = PEER_HEADS
    assert (ce // PEER_NKEYS) % SUBLANES == 0 and SUBLANES % (CE_SUB // PEER_NKEYS) == 0
    return pl.pallas_call(
        functools.partial(_peer_kernel, final_norm=final_norm),
        grid=(t // tt, ne // ce),
        in_specs=[
            pl.BlockSpec((tt, d), lambda i, e: (i, 0)),
            pl.BlockSpec((hp, 2, PEER_NKEYS, tt), lambda i, e: (0, 0, 0, i)),
            pl.BlockSpec((hp, SC_ROWS, tt), lambda i, e: (0, 0, i)),
            pl.BlockSpec((ce, d), lambda i, e: (e, 0)),
            pl.BlockSpec((None, d, ce), lambda i, e: (e, 0, 0)),
            pl.BlockSpec((tt, d), lambda i, e: (i, 0)),
            pl.BlockSpec((1, d), lambda i, e: (0, 0)),
        ],
        out_specs=pl.BlockSpec((tt, d), lambda i, e: (i, 0)),
        out_shape=jax.ShapeDtypeStruct((t, d), F32),
        scratch_shapes=[
            pltpu.VMEM((hp, PEER_NKEYS, tt), BF16),
            pltpu.VMEM((hp, PEER_NKEYS, tt), BF16),
            pltpu.VMEM((hp, tt // LANES, PEER_NKEYS, LANES), jnp.uint32),
            pltpu.VMEM((hp, tt // LANES, PEER_NKEYS, LANES), jnp.uint32),
            pltpu.VMEM((2, CE_SUB, tt), F32),
            pltpu.VMEM((2, CE_SUB, tt), BF16),
            pltpu.VMEM((d, tt), F32),
        ],
        compiler_params=_cparams(("parallel", "arbitrary")),
        name="peer_experts",
    )(h_bf16, st, sc.reshape(hp, SC_ROWS, t), u_bf16, vt_bf16, x2d, gfin.reshape(1, d))


def _expert_major_t(v):
    ne, d = v.shape
    return v.astype(BF16).reshape(ne // CE_STEP, CE_STEP, d).transpose(0, 2, 1)


def _peer(h_bf16, pq, sub_keys, u_bf16, vt_bf16, x2d, gfin, final_norm):
    st, sc = _peer_router(pq, sub_keys)
    return _peer_experts(h_bf16, st, sc, u_bf16, vt_bf16, x2d, gfin, final_norm)


def _glu_kernel(x_ref, g_ref, w_ref, b_ref, u_ref, *, d):
    h = _rms(x_ref[...], g_ref[...]).astype(BF16)
    a = jnp.dot(h, w_ref[...], preferred_element_type=F32) + b_ref[...]
    u_ref[...] = a[:, :d] * jax.nn.sigmoid(a[:, d:])


def _glu_proj(x2d, g, w_bf16, bias):
    t, d = x2d.shape
    tm = TM_PROJ
    return pl.pallas_call(
        functools.partial(_glu_kernel, d=d),
        grid=(t // tm,),
        in_specs=[
            pl.BlockSpec((tm, d), lambda i: (i, 0)),
            pl.BlockSpec((1, d), lambda i: (0, 0)),
            pl.BlockSpec((d, 2 * d), lambda i: (0, 0)),
            pl.BlockSpec((1, 2 * d), lambda i: (0, 0)),
        ],
        out_specs=pl.BlockSpec((tm, d), lambda i: (i, 0)),
        out_shape=jax.ShapeDtypeStruct((t, d), F32),
        compiler_params=_cparams(("parallel",)),
        name="glu_proj",
    )(x2d, g.reshape(1, d), w_bf16, bias.reshape(1, 2 * d))


def _conv_kernel(ucur_ref, uprev_ref, wdw_ref, bdw_ref, lng_ref, lnb_ref, w2_ref, b2_ref,
                 x_ref, gf_ref, wpq_ref, x_out_ref, h_ref, pq_ref, ext_scr, y_scr):
    tm, d = ucur_ref.shape
    first = pl.program_id(1) == 0
    ext_scr[0:CONV_HALO, :] = jnp.where(first, 0.0, uprev_ref[...])
    ext_scr[CONV_HALO:, :] = ucur_ref[...]
    lead = CONV_HALO - (CONV_WIDTH - 1)

    def chunk(ci, carry):
        r0 = pl.multiple_of(ci * CONV_ROWS, CONV_ROWS)
        rows = CONV_ROWS + CONV_HALO
        window = ext_scr[pl.ds(r0, rows), :]
        shifted = [window] + [pltpu.roll(window, rows - r, 0) for r in range(1, SUBLANES)]
        acc = jnp.broadcast_to(bdw_ref[...], (CONV_ROWS, d))
        for w in range(CONV_WIDTH):
            start = (lead + w) // SUBLANES * SUBLANES
            tap = shifted[(lead + w) % SUBLANES][start:start + CONV_ROWS, :]
            acc = acc + wdw_ref[w:w + 1, :] * tap
        y_scr[pl.ds(r0, CONV_ROWS), :] = acc
        return carry

    lax.fori_loop(0, tm // CONV_ROWS, chunk, 0)
    y = y_scr[...]
    mu = jnp.mean(y, axis=-1, keepdims=True)
    var = jnp.mean(jnp.square(y - mu), axis=-1, keepdims=True)
    y = (y - mu) * lax.rsqrt(var + EPS) * lng_ref[...] + lnb_ref[...]
    y = y * jax.nn.sigmoid(y)
    x_new = x_ref[...] + jnp.dot(y.astype(BF16), w2_ref[...], preferred_element_type=F32) + b2_ref[...]
    _peerq_epilogue(x_new, gf_ref, wpq_ref, x_out_ref, h_ref, pq_ref)


def _conv_tail(u3, wdw, bdw, lng, lnb, w2_bf16, b2, x3, gf, wpq_bf16):
    b, s, d = x3.shape
    nq = wpq_bf16.shape[1]
    tm = TM_CONV
    hpb = tm // CONV_HALO
    row = lambda v: v.reshape(1, d)
    tok = pl.BlockSpec((None, tm, d), lambda bi, i: (bi, i, 0))
    vec = pl.BlockSpec((1, d), lambda bi, i: (0, 0))
    return pl.pallas_call(
        _conv_kernel,
        grid=(b, s // tm),
        in_specs=[
            tok,
            pl.BlockSpec((None, CONV_HALO, d), lambda bi, i: (bi, jnp.maximum(i * hpb - 1, 0), 0)),
            pl.BlockSpec((CONV_WIDTH, d), lambda bi, i: (0, 0)),
            vec, vec, vec,
            pl.BlockSpec((d, d), lambda bi, i: (0, 0)),
            vec,
            tok,
            vec,
            pl.BlockSpec((d, nq), lambda bi, i: (0, 0)),
        ],
        out_specs=[
            tok,
            tok,
            pl.BlockSpec((None, tm, nq), lambda bi, i: (bi, i, 0)),
        ],
        out_shape=[
            jax.ShapeDtypeStruct((b, s, d), F32),
            jax.ShapeDtypeStruct((b, s, d), BF16),
            jax.ShapeDtypeStruct((b, s, nq), F32),
        ],
        scratch_shapes=[
            pltpu.VMEM((tm + CONV_HALO, d), F32),
            pltpu.VMEM((tm, d), F32),
        ],
        compiler_params=_cparams(("parallel", "arbitrary")),
        name="conv_tail",
    )(u3, u3, wdw, row(bdw), row(lng), row(lnb), w2_bf16, row(b2), x3, row(gf), wpq_bf16)


def kernel(x, rel_bias, norm_mix, norm_ffn, attn_w_qkv, attn_w_o, conv_w_pw1, conv_b_pw1,
           conv_w_dw, conv_b_dw, conv_ln_g, conv_ln_b, conv_w_pw2, conv_b_pw2, peer_w_q,
           peer_sub_keys, peer_u, peer_v, norm_final):
    b, s, d = x.shape
    t = b * s
    depth = norm_mix.shape[0]
    assert d == N_HEADS * HEAD_DIM and s % MOBA_BLOCK == 0 and s // MOBA_BLOCK <= LANES
    assert t % TT_ROUTER == 0 and s % TM_CONV == 0 and t % TM_PROJ == 0
    x2d = x.reshape(t, d)
    for i in range(depth):
        j = i // 2
        wpq = peer_w_q[i].astype(BF16)
        if i % 2 == 0:
            wq, wk, wv = (attn_w_qkv[j][:, c * d:(c + 1) * d] for c in range(3))
            qt, k, vt, kmean = _qkv_proj(x2d.reshape(b, s, d), norm_mix[i], wq.T.astype(BF16),
                                         wk.astype(BF16), wv.T.astype(BF16))
            o = _moba_attention(qt, k, vt, kmean, _moba_bias_tables(rel_bias))
            x2d, h, pq = _attn_out_proj(o.reshape(t, d), attn_w_o[j].astype(BF16), x2d,
                                        norm_ffn[i], wpq)
        else:
            u = _glu_proj(x2d, norm_mix[i], conv_w_pw1[j].astype(BF16), conv_b_pw1[j])
            x3, h3, pq3 = _conv_tail(u.reshape(b, s, d), conv_w_dw[j], conv_b_dw[j], conv_ln_g[j],
                                     conv_ln_b[j], conv_w_pw2[j].astype(BF16), conv_b_pw2[j],
                                     x2d.reshape(b, s, d), norm_ffn[i], wpq)
            x2d, h, pq = x3.reshape(t, d), h3.reshape(t, d), pq3.reshape(t, -1)
        last = i == depth - 1
        x2d = _peer(h, pq, peer_sub_keys[i], peer_u[i].astype(BF16),
                    _expert_major_t(peer_v[i]), x2d, norm_final, final_norm=last)
    return x2d.reshape(b, s, d)
```

```python
import functools
import math

import numpy as np
import jax
import jax.numpy as jnp
from jax import lax
from jax.experimental import pallas as pl
from jax.experimental.pallas import tpu as pltpu

F32 = jnp.float32
BF16 = jnp.bfloat16
EPS = 1e-6
NEG = -1e30

N_HEADS = 16
HEAD_DIM = 64
MOBA_BLOCK = 256
MOBA_TOPK = 3
REL_BUCKETS = 32
REL_MAX_DIST = 128
CONV_WIDTH = 31
PEER_HEADS = 8
PEER_NKEYS = 128
PEER_HALF = 128
PEER_TOPK = 16
SC_TOP1, SC_TOP2, SC_TAU, SC_ZINV, SC_ROWS = 0, PEER_TOPK, 2 * PEER_TOPK, 2 * PEER_TOPK + 1, 2 * PEER_TOPK + 2

LANES = 128
SUBLANES = 8
VMEM_LIMIT = 60 * 1024 * 1024

MOBA_QLANES = 128
TM_PROJ = 512
TT_ROUTER = 1024
TT_PEER = 512
CE_STEP = 2048
CE_SUB = 512
TM_CONV = 512
CONV_HALO = 32
CONV_ROWS = 16

NT_DIMS = (((1,), (1,)), ((), ()))


def _cparams(sem):
    return pltpu.CompilerParams(dimension_semantics=sem, vmem_limit_bytes=VMEM_LIMIT)


def _rms(x, g):
    return x * lax.rsqrt(jnp.mean(x * x, axis=-1, keepdims=True) + EPS) * g


def _qkv_kernel(x_ref, g_ref, wqt_ref, wk_ref, wvt_ref, qt_ref, k_ref, vt_ref, km_ref):
    h = _rms(x_ref[...], g_ref[...]).astype(BF16)
    scale = HEAD_DIM ** -0.5 * math.log2(math.e)
    qt = lax.dot_general(wqt_ref[...], h, NT_DIMS, preferred_element_type=F32)
    qt_ref[...] = (qt * scale).astype(BF16)
    k = jnp.dot(h, wk_ref[...], preferred_element_type=F32)
    k_ref[...] = k.astype(BF16)
    vt = lax.dot_general(wvt_ref[...], h, NT_DIMS, preferred_element_type=F32)
    vt_ref[...] = vt.astype(BF16)
    nblk = k.shape[0] // MOBA_BLOCK
    km_ref[...] = jnp.mean(k.reshape(nblk, MOBA_BLOCK, k.shape[1]), axis=1)


def _qkv_proj(x3, g, wqt_bf16, wk_bf16, wvt_bf16):
    b, s, d = x3.shape
    tm = TM_PROJ
    nblk = tm // MOBA_BLOCK
    nb = s // MOBA_BLOCK
    wspec = pl.BlockSpec((d, d), lambda bi, i: (0, 0))
    tspec = pl.BlockSpec((None, d, tm), lambda bi, i: (bi, 0, i))
    return pl.pallas_call(
        _qkv_kernel,
        grid=(b, s // tm),
        in_specs=[
            pl.BlockSpec((None, tm, d), lambda bi, i: (bi, i, 0)),
            pl.BlockSpec((1, d), lambda bi, i: (0, 0)),
            wspec, wspec, wspec,
        ],
        out_specs=[
            tspec,
            pl.BlockSpec((None, tm, d), lambda bi, i: (bi, i, 0)),
            tspec,
            pl.BlockSpec((None, None, nblk, d), lambda bi, i: (bi, i, 0, 0)),
        ],
        out_shape=[
            jax.ShapeDtypeStruct((b, d, s), BF16),
            jax.ShapeDtypeStruct((b, s, d), BF16),
            jax.ShapeDtypeStruct((b, d, s), BF16),
            jax.ShapeDtypeStruct((b, s // tm, nblk, d), F32),
        ],
        compiler_params=_cparams(("parallel", "parallel")),
        name="qkv_proj",
    )(x3, g.reshape(1, d), wqt_bf16, wk_bf16, wvt_bf16)


def _moba_kernel(qt_ref, k_ref, vt_ref, km_ref, tbl_ref, o_ref, sel_scr, ss_scr, p_scr, acc_scr, *, nb):
    blk = MOBA_BLOCK
    i = pl.program_id(2)
    jp = jnp.maximum(i - 1, 0)
    qt = qt_ref[...]
    frow = lax.broadcasted_iota(jnp.int32, qt.shape, 0)
    blk_id = lax.broadcasted_iota(jnp.int32, (nb, blk), 0)
    valid = blk_id < i
    km = km_ref[...]

    qh = []
    for hh in range(2):
        qh_t = jnp.where(frow // HEAD_DIM == hh, qt, jnp.zeros_like(qt))
        qh.append(qh_t)
        gate = jnp.dot(km, qh_t.astype(F32), precision=lax.Precision.HIGHEST,
                       preferred_element_type=F32)
        gate = jnp.where(valid, gate, NEG)
        g = gate
        for _ in range(MOBA_TOPK - 1):
            g = jnp.where(g >= jnp.max(g, axis=0, keepdims=True), NEG, g)
        tau = jnp.max(g, axis=0, keepdims=True)
        sel = jnp.where(valid, jnp.where(gate >= tau, 0.0, NEG), NEG)
        sel_scr[hh, 0:nb, :] = sel
        sel_scr[hh, nb:nb + SUBLANES, :] = jnp.full((SUBLANES, blk), NEG, F32)

    def k_block(j):
        return k_ref[pl.ds(pl.multiple_of(j * blk, blk), blk), :]

    nqs = blk // MOBA_QLANES
    chains = [(hh, qs) for hh in range(2) for qs in range(nqs)]

    def qcols(qs):
        return slice(qs * MOBA_QLANES, (qs + 1) * MOBA_QLANES)

    def lanes_of_head(vals, hh):
        return jnp.concatenate(vals[hh * nqs:(hh + 1) * nqs], axis=1)

    def scores(kj, hh):
        return jnp.dot(kj, qh[hh], preferred_element_type=F32)

    def sel_row(hh, j):
        return sel_scr[hh, pl.ds(j, 1), :]

    ones_rows = jnp.ones((2 * SUBLANES, blk), BF16)

    def pv(hh, p, j):
        vj = vt_ref[hh * HEAD_DIM:(hh + 1) * HEAD_DIM, pl.ds(pl.multiple_of(j * blk, blk), blk)]
        return jnp.dot(jnp.concatenate([vj, ones_rows], axis=0), p, preferred_element_type=F32)

    def softmax_step(m_prev, s):
        m_new = jnp.maximum(m_prev, jnp.max(s, axis=0, keepdims=True))
        alpha = jnp.exp2(m_prev - m_new)
        return m_new, alpha, jnp.exp2(s - m_new).astype(BF16)

    def far_scores(b, buf):
        kj = k_block(jnp.minimum(b, i))
        row = jnp.where(b < jp, b, nb)
        for hh in range(2):
            ss_scr[buf, hh] = scores(kj, hh) + sel_row(hh, row)

    k_own, k_prev = k_block(i), k_block(jp)
    ss = [scores(k_own, hh) + tbl_ref[hh, 0] for hh in range(2)]
    ss_prev = [scores(k_prev, hh) + tbl_ref[hh, 1] + sel_row(hh, jp) for hh in range(2)]
    ms = [jnp.max(ss[hh][:, qcols(qs)], axis=0, keepdims=True) for hh, qs in chains]
    ps = [jnp.exp2(ss[hh][:, qcols(qs)] - m).astype(BF16) for (hh, qs), m in zip(chains, ms)]
    accs = [pv(hh, lanes_of_head(ps, hh), i) for hh in range(2)]
    steps = [softmax_step(m, ss_prev[hh][:, qcols(qs)]) for (hh, qs), m in zip(chains, ms)]
    accs = [lanes_of_head([st[1] for st in steps], hh) * accs[hh]
            + pv(hh, lanes_of_head([st[2] for st in steps], hh), jp) for hh in range(2)]
    ms = [st[0] for st in steps]

    def v_block(b):
        return jnp.clip(b, 0, i)

    def half(b, cur, ms):
        nxt = 1 - cur
        far_scores(b + 1, nxt)
        pvs = [pv(hh, p_scr[nxt, hh], v_block(b - 1)) for hh in range(2)]
        new_ms = []
        for c, (hh, qs) in enumerate(chains):
            m_new, alpha, p = softmax_step(ms[c], ss_scr[cur, hh, :, qcols(qs)])
            p_scr[cur, hh, :, qcols(qs)] = p
            acc_scr[hh, :, qcols(qs)] = alpha * (acc_scr[hh, :, qcols(qs)] + pvs[hh][:, qcols(qs)])
            new_ms.append(m_new)
        return tuple(new_ms)

    def two_blocks(t, ms):
        return half(2 * t + 1, 1, half(2 * t, 0, ms))

    for hh in range(2):
        acc_scr[hh] = accs[hh]
        p_scr[1, hh] = jnp.zeros((blk, blk), BF16)
    far_scores(0, 0)
    n_pairs = (jp + 1) // 2
    lax.fori_loop(0, n_pairs, two_blocks, tuple(ms))
    accs = [acc_scr[hh] + pv(hh, p_scr[1, hh], v_block(2 * n_pairs - 1)) for hh in range(2)]
    o_t = jnp.concatenate([acc[:HEAD_DIM] / acc[HEAD_DIM:HEAD_DIM + 1] for acc in accs], axis=0)
    o_ref[...] = o_t.T.astype(o_ref.dtype)


def _t5_bucket_np(dist):
    max_exact = REL_BUCKETS // 2
    d = np.maximum(dist, 0)
    df = np.maximum(d, 1).astype(np.float32)
    large = max_exact + (np.log(df / max_exact) / math.log(REL_MAX_DIST / max_exact)
                         * (REL_BUCKETS - max_exact)).astype(np.int32)
    large = np.minimum(large, REL_BUCKETS - 1)
    return np.where(d < max_exact, d, large)


def _moba_bias_tables(rel_bias):
    blk = MOBA_BLOCK
    assert int(_t5_bucket_np(np.array([blk + 1]))[0]) == REL_BUCKETS - 1
    dist = np.arange(-(blk - 1), blk)

    def by_distance(d):
        onehot = (_t5_bucket_np(d)[None, :] == np.arange(REL_BUCKETS)[:, None]).astype(np.float32)
        return jnp.dot(rel_bias.astype(F32), onehot, precision=lax.Precision.HIGHEST)

    def toeplitz(x):
        h = x.shape[0]
        y = jnp.concatenate([x, jnp.zeros((h, 1), x.dtype)], axis=1)
        skew = jnp.tile(y, (1, blk))[:, :blk * (2 * blk - 1)].reshape(h, blk, 2 * blk - 1)
        return skew[:, :, blk - 1:]

    far = rel_bias[:, REL_BUCKETS - 1].astype(F32)[:, None, None]
    causal = (np.arange(blk)[None, :] >= np.arange(blk)[:, None])[None]
    log2e = math.log2(math.e)
    own = jnp.where(causal, (toeplitz(by_distance(dist)) - far) * log2e, NEG)
    prev = (toeplitz(by_distance(dist + blk)) - far) * log2e
    return jnp.stack([own, prev], axis=1)


def _moba_attention(qt, k, vt, kmean, tbl):
    b, s, d = k.shape
    blk = MOBA_BLOCK
    nb = s // blk
    nhp = d // LANES
    return pl.pallas_call(
        functools.partial(_moba_kernel, nb=nb),
        grid=(b, nhp, nb),
        in_specs=[
            pl.BlockSpec((None, LANES, blk), lambda bi, hp, i: (bi, hp, i)),
            pl.BlockSpec((None, s, LANES), lambda bi, hp, i: (bi, 0, hp)),
            pl.BlockSpec((None, LANES, s), lambda bi, hp, i: (bi, hp, 0)),
            pl.BlockSpec((None, nb, LANES), lambda bi, hp, i: (bi, 0, hp)),
            pl.BlockSpec((2, 2, blk, blk), lambda bi, hp, i: (hp, 0, 0, 0)),
        ],
        out_specs=pl.BlockSpec((None, blk, LANES), lambda bi, hp, i: (bi, i, hp)),
        out_shape=jax.ShapeDtypeStruct((b, s, d), BF16),
        scratch_shapes=[
            pltpu.VMEM((2, nb + SUBLANES, blk), F32),
            pltpu.VMEM((2, 2, blk, blk), F32),
            pltpu.VMEM((2, 2, blk, blk), BF16),
            pltpu.VMEM((2, HEAD_DIM + 2 * SUBLANES, blk), F32),
        ],
        compiler_params=_cparams(("parallel", "parallel", "arbitrary")),
        name="moba_attention",
    )(qt, k, vt, kmean.reshape(b, nb, d), tbl)


def _peerq_epilogue(x_new, gf_ref, wpq_ref, x_out_ref, h_ref, pq_ref):
    x_out_ref[...] = x_new
    h = _rms(x_new, gf_ref[...]).astype(BF16)
    h_ref[...] = h
    pq_ref[...] = jnp.dot(h, wpq_ref[...], preferred_element_type=F32)


def _attn_out_kernel(o_ref, wo_ref, x_ref, gf_ref, wpq_ref, x_out_ref, h_ref, pq_ref):
    x_new = x_ref[...] + jnp.dot(o_ref[...], wo_ref[...], preferred_element_type=F32)
    _peerq_epilogue(x_new, gf_ref, wpq_ref, x_out_ref, h_ref, pq_ref)


def _attn_out_proj(o2d, wo_bf16, x2d, gf, wpq_bf16):
    t, d = x2d.shape
    nq = wpq_bf16.shape[1]
    tm = TM_PROJ
    return pl.pallas_call(
        _attn_out_kernel,
        grid=(t // tm,),
        in_specs=[
            pl.BlockSpec((tm, d), lambda i: (i, 0)),
            pl.BlockSpec((d, d), lambda i: (0, 0)),
            pl.BlockSpec((tm, d), lambda i: (i, 0)),
            pl.BlockSpec((1, d), lambda i: (0, 0)),
            pl.BlockSpec((d, nq), lambda i: (0, 0)),
        ],
        out_specs=[
            pl.BlockSpec((tm, d), lambda i: (i, 0)),
            pl.BlockSpec((tm, d), lambda i: (i, 0)),
            pl.BlockSpec((tm, nq), lambda i: (i, 0)),
        ],
        out_shape=[
            jax.ShapeDtypeStruct((t, d), F32),
            jax.ShapeDtypeStruct((t, d), BF16),
            jax.ShapeDtypeStruct((t, nq), F32),
        ],
        compiler_params=_cparams(("parallel",)),
        name="attn_out_proj",
    )(o2d, wo_bf16, x2d, gf.reshape(1, d), wpq_bf16)


def _cmpx(vals, a, b):
    hi = jnp.maximum(vals[a], vals[b])
    lo = jnp.minimum(vals[a], vals[b])
    vals[a], vals[b] = hi, lo


def _bitonic_merge_desc(vals):
    n = len(vals)
    dist = n // 2
    while dist >= 1:
        for s in range(0, n, 2 * dist):
            for t in range(s, s + dist):
                _cmpx(vals, t, t + dist)
        dist //= 2
    return vals


def _sort_desc(vals):
    n = len(vals)
    if n == 1:
        return vals
    top = _sort_desc(vals[: n // 2])
    bot = _sort_desc(vals[n // 2:])
    return _bitonic_merge_desc(top + bot[::-1])


def _top_merge(a, b):
    n = len(a)
    return _bitonic_merge_desc([jnp.maximum(a[r], b[n - 1 - r]) for r in range(n)])


def _sorted_top16(vals):
    groups = [_sort_desc(vals[g:g + PEER_TOPK]) for g in range(0, len(vals), PEER_TOPK)]
    while len(groups) > 1:
        groups = [_top_merge(groups[g], groups[g + 1]) for g in range(0, len(groups), 2)]
    return groups[0]


def _router_kernel(pq_ref, keys_ref, st_ref, sc_ref, scr_ref):
    nch = TT_ROUTER // LANES
    tops = []
    for c in range(2):
        qc = pq_ref[:, c * PEER_HALF:(c + 1) * PEER_HALF]
        s_t = lax.dot_general(keys_ref[0, c], qc, NT_DIMS,
                              precision=lax.Precision.HIGHEST,
                              preferred_element_type=F32)
        st_ref[0, c] = s_t
        for ch in range(nch):
            scr_ref[c, pl.ds(ch, PEER_NKEYS, stride=nch), :] = s_t[:, ch * LANES:(ch + 1) * LANES]
        vals = [scr_ref[c, k * nch:(k + 1) * nch, :] for k in range(PEER_NKEYS)]
        tops.append(_sorted_top16(vals))
    a, b = tops
    cands = [a[r] + b[c] for r in range(PEER_TOPK) for c in range(PEER_TOPK)
             if (r + 1) * (c + 1) <= PEER_TOPK]
    npad = 1 << (len(cands) - 1).bit_length()
    cands = cands + [jnp.full_like(cands[0], NEG)] * (npad - len(cands))
    best = _sort_desc(cands)[:PEER_TOPK]
    z = jnp.ones_like(best[0])
    for r in range(1, PEER_TOPK):
        z = z + jnp.exp(best[r] - best[0])
    for r in range(PEER_TOPK):
        sc_ref[0, SC_TOP1 + r] = a[r]
        sc_ref[0, SC_TOP2 + r] = b[r]
    sc_ref[0, SC_TAU] = best[PEER_TOPK - 1]
    sc_ref[0, SC_ZINV] = 1.0 / z


def _peer_router(pq, sub_keys):
    t = pq.shape[0]
    tt = TT_ROUTER
    nch = tt // LANES
    hp = PEER_HEADS
    return pl.pallas_call(
        _router_kernel,
        grid=(t // tt, hp),
        in_specs=[
            pl.BlockSpec((tt, 2 * PEER_HALF), lambda i, h: (i, h)),
            pl.BlockSpec((1, 2, PEER_NKEYS, PEER_HALF), lambda i, h: (h, 0, 0, 0)),
        ],
        out_specs=[
            pl.BlockSpec((1, 2, PEER_NKEYS, tt), lambda i, h: (h, 0, 0, i)),
            pl.BlockSpec((1, SC_ROWS, nch, LANES), lambda i, h: (h, 0, i, 0)),
        ],
        out_shape=[
            jax.ShapeDtypeStruct((hp, 2, PEER_NKEYS, t), F32),
            jax.ShapeDtypeStruct((hp, SC_ROWS, t // LANES, LANES), F32),
        ],
        scratch_shapes=[pltpu.VMEM((2, PEER_NKEYS * nch, LANES), F32)],
        compiler_params=_cparams(("parallel", "parallel")),
        name="peer_router",
    )(pq, sub_keys)


def _gelu2(x):
    c0 = math.sqrt(2.0 / math.pi)
    z = x * (c0 + (c0 * 0.044715) * (x * x))
    return x + x * jnp.tanh(z)


def _dup_bf16_bits(v):
    bits = pltpu.bitcast(v.astype(BF16).astype(F32), jnp.uint32)
    return bits | (bits >> 16)


def _peer_kernel(h_ref, st_ref, sc_ref, u_ref, vt_ref, x_ref, gfin_ref, o_ref,
                 rank_scr, bw_scr, cnt_scr, aw_scr, act_scr, wa_scr, acc_scr, *, final_norm):
    e = pl.program_id(1)
    tt = h_ref.shape[0]
    nsub = CE_STEP // CE_SUB
    ipc = CE_SUB // PEER_NKEYS
    pack = 2 * SUBLANES

    @pl.when(e == 0)
    def _():
        acc_scr[...] = jnp.zeros_like(acc_scr)
        for hd in range(PEER_HEADS):
            s1 = st_ref[hd, 0]
            s2 = st_ref[hd, 1]
            tau = sc_ref[hd, SC_TAU:SC_TAU + 1, :]
            rank2 = jnp.zeros_like(s2)
            cnt = jnp.zeros_like(s1)
            for c in range(PEER_TOPK):
                b_c = sc_ref[hd, SC_TOP2 + c:SC_TOP2 + c + 1, :]
                rank2 = jnp.where(b_c > s2, c + 1.0, rank2)
                cnt = jnp.where(s1 + b_c >= tau, c + 1.0, cnt)
            a0 = sc_ref[hd, SC_TOP1:SC_TOP1 + 1, :]
            b0 = sc_ref[hd, SC_TOP2:SC_TOP2 + 1, :]
            zinv = sc_ref[hd, SC_ZINV:SC_ZINV + 1, :]
            rank_scr[hd] = rank2.astype(BF16)
            bw_scr[hd] = (jnp.exp(s2 - b0) * (0.5 * zinv)).astype(BF16)
            cnt_w = _dup_bf16_bits(cnt)
            a_w = _dup_bf16_bits(jnp.exp(s1 - a0))
            for lg in range(tt // LANES):
                cnt_scr[hd, lg] = cnt_w[:, lg * LANES:(lg + 1) * LANES]
                aw_scr[hd, lg] = a_w[:, lg * LANES:(lg + 1) * LANES]

    def row_bf16(tile, r):
        words = jnp.broadcast_to(tile[r:r + 1, :], (SUBLANES, LANES))
        return pltpu.bitcast(words, BF16)

    def up_proj(c):
        act_scr[c % 2] = lax.dot_general(u_ref[c * CE_SUB:(c + 1) * CE_SUB, :], h_ref[...], NT_DIMS,
                                         preferred_element_type=F32)

    def down_proj(c):
        acc_scr[...] += jnp.dot(vt_ref[:, c * CE_SUB:(c + 1) * CE_SUB], wa_scr[c % 2],
                                preferred_element_type=F32)

    def gate(c):
        njg = PEER_NKEYS // pack
        zero = jnp.zeros((pack, LANES), BF16)
        for lg in range(tt // LANES):
            lanes = slice(lg * LANES, (lg + 1) * LANES)
            w = [[None] * njg for _ in range(ipc)]
            for hd in range(PEER_HEADS):
                rank = [rank_scr[hd, jg * pack:(jg + 1) * pack, lanes] for jg in range(njg)]
                bw = [bw_scr[hd, jg * pack:(jg + 1) * pack, lanes] for jg in range(njg)]
                first = (e * nsub + c) * ipc
                base = pl.multiple_of((first // SUBLANES) * SUBLANES, SUBLANES)
                off = (c * ipc) % SUBLANES
                cnt_tile = cnt_scr[hd, lg, pl.ds(base, SUBLANES), :]
                a_tile = aw_scr[hd, lg, pl.ds(base, SUBLANES), :]
                for il in range(ipc):
                    cnt_row = row_bf16(cnt_tile, off + il)
                    a_row = row_bf16(a_tile, off + il)
                    for jg in range(njg):
                        term = jnp.where(rank[jg] < cnt_row, a_row * bw[jg], zero)
                        w[il][jg] = term if w[il][jg] is None else w[il][jg] + term
            for il in range(ipc):
                for jg in range(njg):
                    rows = slice(il * PEER_NKEYS + jg * pack, il * PEER_NKEYS + (jg + 1) * pack)
                    wa_scr[c % 2, rows, lanes] = w[il][jg] * _gelu2(act_scr[c % 2, rows, lanes]).astype(BF16)

    up_proj(0)
    for c in range(nsub):
        if c + 1 < nsub:
            up_proj(c + 1)
        if c >= 1:
            down_proj(c - 1)
        gate(c)
    down_proj(nsub - 1)

    @pl.when(e == pl.num_programs(1) - 1)
    def _():
        y = x_ref[...] + acc_scr[...].T
        if final_norm:
            y = _rms(y, gfin_ref[...])
        o_ref[...] = y


def _peer_experts(h_bf16, st, sc, u_bf16, vt_bf16, x2d, gfin, final_norm):
    t, d = x2d.shape
    ne = u_bf16.shape[0]
    tt, ce = TT_PEER, CE_STEP
    hp = PEER_HEADS
    assert (ce // PEER_NKEYS) % SUBLANES == 0 and SUBLANES % (CE_SUB // PEER_NKEYS) == 0
    return pl.pallas_call(
        functools.partial(_peer_kernel, final_norm=final_norm),
        grid=(t // tt, ne // ce),
        in_specs=[
            pl.BlockSpec((tt, d), lambda i, e: (i, 0)),
            pl.BlockSpec((hp, 2, PEER_NKEYS, tt), lambda i, e: (0, 0, 0, i)),
            pl.BlockSpec((hp, SC_ROWS, tt), lambda i, e: (0, 0, i)),
            pl.BlockSpec((ce, d), lambda i, e: (e, 0)),
            pl.BlockSpec((None, d, ce), lambda i, e: (e, 0, 0)),
            pl.BlockSpec((tt, d), lambda i, e: (i, 0)),
            pl.BlockSpec((1, d), lambda i, e: (0, 0)),
        ],
        out_specs=pl.BlockSpec((tt, d), lambda i, e: (i, 0)),
        out_shape=jax.ShapeDtypeStruct((t, d), F32),
        scratch_shapes=[
            pltpu.VMEM((hp, PEER_NKEYS, tt), BF16),
            pltpu.VMEM((hp, PEER_NKEYS, tt), BF16),
            pltpu.VMEM((hp, tt // LANES, PEER_NKEYS, LANES), jnp.uint32),
            pltpu.VMEM((hp, tt // LANES, PEER_NKEYS, LANES), jnp.uint32),
            pltpu.VMEM((2, CE_SUB, tt), F32),
            pltpu.VMEM((2, CE_SUB, tt), BF16),
            pltpu.VMEM((d, tt), F32),
        ],
        compiler_params=_cparams(("parallel", "arbitrary")),
        name="peer_experts",
    )(h_bf16, st, sc.reshape(hp, SC_ROWS, t), u_bf16, vt_bf16, x2d, gfin.reshape(1, d))


def _expert_major_t(v):
    ne, d = v.shape
    return v.astype(BF16).reshape(ne // CE_STEP, CE_STEP, d).transpose(0, 2, 1)


def _peer(h_bf16, pq, sub_keys, u_bf16, vt_bf16, x2d, gfin, final_norm):
    st, sc = _peer_router(pq, sub_keys)
    return _peer_experts(h_bf16, st, sc, u_bf16, vt_bf16, x2d, gfin, final_norm)


def _glu_kernel(x_ref, g_ref, w_ref, b_ref, u_ref, *, d):
    h = _rms(x_ref[...], g_ref[...]).astype(BF16)
    a = jnp.dot(h, w_ref[...], preferred_element_type=F32) + b_ref[...]
    u_ref[...] = a[:, :d] * jax.nn.sigmoid(a[:, d:])


def _glu_proj(x2d, g, w_bf16, bias):
    t, d = x2d.shape
    tm = TM_PROJ
    return pl.pallas_call(
        functools.partial(_glu_kernel, d=d),
        grid=(t // tm,),
        in_specs=[
            pl.BlockSpec((tm, d), lambda i: (i, 0)),
            pl.BlockSpec((1, d), lambda i: (0, 0)),
            pl.BlockSpec((d, 2 * d), lambda i: (0, 0)),
            pl.BlockSpec((1, 2 * d), lambda i: (0, 0)),
        ],
        out_specs=pl.BlockSpec((tm, d), lambda i: (i, 0)),
        out_shape=jax.ShapeDtypeStruct((t, d), F32),
        compiler_params=_cparams(("parallel",)),
        name="glu_proj",
    )(x2d, g.reshape(1, d), w_bf16, bias.reshape(1, 2 * d))


def _conv_kernel(ucur_ref, uprev_ref, wdw_ref, bdw_ref, lng_ref, lnb_ref, w2_ref, b2_ref,
                 x_ref, gf_ref, wpq_ref, x_out_ref, h_ref, pq_ref, ext_scr, y_scr):
    tm, d = ucur_ref.shape
    first = pl.program_id(1) == 0
    ext_scr[0:CONV_HALO, :] = jnp.where(first, 0.0, uprev_ref[...])
    ext_scr[CONV_HALO:, :] = ucur_ref[...]
    lead = CONV_HALO - (CONV_WIDTH - 1)

    def chunk(ci, carry):
        r0 = pl.multiple_of(ci * CONV_ROWS, CONV_ROWS)
        rows = CONV_ROWS + CONV_HALO
        window = ext_scr[pl.ds(r0, rows), :]
        shifted = [window] + [pltpu.roll(window, rows - r, 0) for r in range(1, SUBLANES)]
        acc = jnp.broadcast_to(bdw_ref[...], (CONV_ROWS, d))
        for w in range(CONV_WIDTH):
            start = (lead + w) // SUBLANES * SUBLANES
            tap = shifted[(lead + w) % SUBLANES][start:start + CONV_ROWS, :]
            acc = acc + wdw_ref[w:w + 1, :] * tap
        y_scr[pl.ds(r0, CONV_ROWS), :] = acc
        return carry

    lax.fori_loop(0, tm // CONV_ROWS, chunk, 0)
    y = y_scr[...]
    mu = jnp.mean(y, axis=-1, keepdims=True)
    var = jnp.mean(jnp.square(y - mu), axis=-1, keepdims=True)
    y = (y - mu) * lax.rsqrt(var + EPS) * lng_ref[...] + lnb_ref[...]
    y = y * jax.nn.sigmoid(y)
    x_new = x_ref[...] + jnp.dot(y.astype(BF16), w2_ref[...], preferred_element_type=F32) + b2_ref[...]
    _peerq_epilogue(x_new, gf_ref, wpq_ref, x_out_ref, h_ref, pq_ref)


def _conv_tail(u3, wdw, bdw, lng, lnb, w2_bf16, b2, x3, gf, wpq_bf16):
    b, s, d = x3.shape
    nq = wpq_bf16.shape[1]
    tm = TM_CONV
    hpb = tm // CONV_HALO
    row = lambda v: v.reshape(1, d)
    tok = pl.BlockSpec((None, tm, d), lambda bi, i: (bi, i, 0))
    vec = pl.BlockSpec((1, d), lambda bi, i: (0, 0))
    return pl.pallas_call(
        _conv_kernel,
        grid=(b, s // tm),
        in_specs=[
            tok,
            pl.BlockSpec((None, CONV_HALO, d), lambda bi, i: (bi, jnp.maximum(i * hpb - 1, 0), 0)),
            pl.BlockSpec((CONV_WIDTH, d), lambda bi, i: (0, 0)),
            vec, vec, vec,
            pl.BlockSpec((d, d), lambda bi, i: (0, 0)),
            vec,
            tok,
            vec,
            pl.BlockSpec((d, nq), lambda bi, i: (0, 0)),
        ],
        out_specs=[
            tok,
            tok,
            pl.BlockSpec((None, tm, nq), lambda bi, i: (bi, i, 0)),
        ],
        out_shape=[
            jax.ShapeDtypeStruct((b, s, d), F32),
            jax.ShapeDtypeStruct((b, s, d), BF16),
            jax.ShapeDtypeStruct((b, s, nq), F32),
        ],
        scratch_shapes=[
            pltpu.VMEM((tm + CONV_HALO, d), F32),
            pltpu.VMEM((tm, d), F32),
        ],
        compiler_params=_cparams(("parallel", "arbitrary")),
        name="conv_tail",
    )(u3, u3, wdw, row(bdw), row(lng), row(lnb), w2_bf16, row(b2), x3, row(gf), wpq_bf16)


def kernel(x, rel_bias, norm_mix, norm_ffn, attn_w_qkv, attn_w_o, conv_w_pw1, conv_b_pw1,
           conv_w_dw, conv_b_dw, conv_ln_g, conv_ln_b, conv_w_pw2, conv_b_pw2, peer_w_q,
           peer_sub_keys, peer_u, peer_v, norm_final):
    b, s, d = x.shape
    t = b * s
    depth = norm_mix.shape[0]
    assert d == N_HEADS * HEAD_DIM and s % MOBA_BLOCK == 0 and s // MOBA_BLOCK <= LANES
    assert t % TT_ROUTER == 0 and s % TM_CONV == 0 and t % TM_PROJ == 0
    x2d = x.reshape(t, d)
    for i in range(depth):
        j = i // 2
        wpq = peer_w_q[i].astype(BF16)
        if i % 2 == 0:
            wq, wk, wv = (attn_w_qkv[j][:, c * d:(c + 1) * d] for c in range(3))
            qt, k, vt, kmean = _qkv_proj(x2d.reshape(b, s, d), norm_mix[i], wq.T.astype(BF16),
                                         wk.astype(BF16), wv.T.astype(BF16))
            o = _moba_attention(qt, k, vt, kmean, _moba_bias_tables(rel_bias))
            x2d, h, pq = _attn_out_proj(o.reshape(t, d), attn_w_o[j].astype(BF16), x2d,
                                        norm_ffn[i], wpq)
        else:
            u = _glu_proj(x2d, norm_mix[i], conv_w_pw1[j].astype(BF16), conv_b_pw1[j])
            x3, h3, pq3 = _conv_tail(u.reshape(b, s, d), conv_w_dw[j], conv_b_dw[j], conv_ln_g[j],
                                     conv_ln_b[j], conv_w_pw2[j].astype(BF16), conv_b_pw2[j],
                                     x2d.reshape(b, s, d), norm_ffn[i], wpq)
            x2d, h, pq = x3.reshape(t, d), h3.reshape(t, d), pq3.reshape(t, -1)
        last = i == depth - 1
        x2d = _peer(h, pq, peer_sub_keys[i], peer_u[i].astype(BF16),
                    _expert_major_t(peer_v[i]), x2d, norm_final, final_norm=last)
    return x2d.reshape(b, s, d)
```

```python
import functools
import math

import numpy as np
import jax
import jax.numpy as jnp
from jax import lax
from jax.experimental import pallas as pl
from jax.experimental.pallas import tpu as pltpu

F32 = jnp.float32
BF16 = jnp.bfloat16
EPS = 1e-6
NEG = -1e30

N_HEADS = 16
HEAD_DIM = 64
MOBA_BLOCK = 256
MOBA_TOPK = 3
REL_BUCKETS = 32
REL_MAX_DIST = 128
CONV_WIDTH = 31
PEER_HEADS = 8
PEER_NKEYS = 128
PEER_HALF = 128
PEER_TOPK = 16
SC_TOP1, SC_TOP2, SC_TAU, SC_ZINV, SC_ROWS = 0, PEER_TOPK, 2 * PEER_TOPK, 2 * PEER_TOPK + 1, 2 * PEER_TOPK + 2

LANES = 128
SUBLANES = 8
VMEM_LIMIT = 60 * 1024 * 1024

MOBA_QLANES = 128
TM_PROJ = 512
TT_ROUTER = 1024
TT_PEER = 512
CE_STEP = 2048
CE_SUB = 512
TM_CONV = 512
CONV_HALO = 32
CONV_ROWS = 16

NT_DIMS = (((1,), (1,)), ((), ()))


def _cparams(sem):
    return pltpu.CompilerParams(dimension_semantics=sem, vmem_limit_bytes=VMEM_LIMIT)


def _rms(x, g):
    return x * lax.rsqrt(jnp.mean(x * x, axis=-1, keepdims=True) + EPS) * g


def _qkv_kernel(x_ref, g_ref, wqt_ref, wk_ref, wvt_ref, qt_ref, k_ref, vt_ref, km_ref):
    h = _rms(x_ref[...], g_ref[...]).astype(BF16)
    scale = HEAD_DIM ** -0.5 * math.log2(math.e)
    qt = lax.dot_general(wqt_ref[...], h, NT_DIMS, preferred_element_type=F32)
    qt_ref[...] = (qt * scale).astype(BF16)
    k = jnp.dot(h, wk_ref[...], preferred_element_type=F32)
    k_ref[...] = k.astype(BF16)
    vt = lax.dot_general(wvt_ref[...], h, NT_DIMS, preferred_element_type=F32)
    vt_ref[...] = vt.astype(BF16)
    nblk = k.shape[0] // MOBA_BLOCK
    km_ref[...] = jnp.mean(k.reshape(nblk, MOBA_BLOCK, k.shape[1]), axis=1)


def _qkv_proj(x3, g, wqt_bf16, wk_bf16, wvt_bf16):
    b, s, d = x3.shape
    tm = TM_PROJ
    nblk = tm // MOBA_BLOCK
    nb = s // MOBA_BLOCK
    wspec = pl.BlockSpec((d, d), lambda bi, i: (0, 0))
    tspec = pl.BlockSpec((None, d, tm), lambda bi, i: (bi, 0, i))
    return pl.pallas_call(
        _qkv_kernel,
        grid=(b, s // tm),
        in_specs=[
            pl.BlockSpec((None, tm, d), lambda bi, i: (bi, i, 0)),
            pl.BlockSpec((1, d), lambda bi, i: (0, 0)),
            wspec, wspec, wspec,
        ],
        out_specs=[
            tspec,
            pl.BlockSpec((None, tm, d), lambda bi, i: (bi, i, 0)),
            tspec,
            pl.BlockSpec((None, None, nblk, d), lambda bi, i: (bi, i, 0, 0)),
        ],
        out_shape=[
            jax.ShapeDtypeStruct((b, d, s), BF16),
            jax.ShapeDtypeStruct((b, s, d), BF16),
            jax.ShapeDtypeStruct((b, d, s), BF16),
            jax.ShapeDtypeStruct((b, s // tm, nblk, d), F32),
        ],
        compiler_params=_cparams(("parallel", "parallel")),
        name="qkv_proj",
    )(x3, g.reshape(1, d), wqt_bf16, wk_bf16, wvt_bf16)


def _moba_kernel(qt_ref, k_ref, vt_ref, km_ref, tbl_ref, o_ref, sel_scr, ss_scr, p_scr, acc_scr, *, nb):
    blk = MOBA_BLOCK
    i = pl.program_id(2)
    jp = jnp.maximum(i - 1, 0)
    qt = qt_ref[...]
    frow = lax.broadcasted_iota(jnp.int32, qt.shape, 0)
    blk_id = lax.broadcasted_iota(jnp.int32, (nb, blk), 0)
    valid = blk_id < i
    km = km_ref[...]

    qh = [jnp.where(frow // HEAD_DIM == hh, qt, jnp.zeros_like(qt)) for hh in range(2)]

    def k_block(j):
        return k_ref[pl.ds(pl.multiple_of(j * blk, blk), blk), :]

    def scores(kj, hh):
        return jnp.dot(kj, qh[hh], preferred_element_type=F32)

    k_own, k_prev = k_block(i), k_block(jp)
    raw_own = [scores(k_own, hh) for hh in range(2)]
    raw_prev = [scores(k_prev, hh) for hh in range(2)]

    for hh in range(2):
        gate = jnp.dot(km, qh[hh].astype(F32), precision=lax.Precision.HIGHEST,
                       preferred_element_type=F32)
        gate = jnp.where(valid, gate, NEG)
        g = gate
        for _ in range(MOBA_TOPK - 1):
            g = jnp.where(g >= jnp.max(g, axis=0, keepdims=True), NEG, g)
        tau = jnp.max(g, axis=0, keepdims=True)
        sel = jnp.where(valid, jnp.where(gate >= tau, 0.0, NEG), NEG)
        sel_scr[hh, 0:nb, :] = sel
        sel_scr[hh, nb:nb + SUBLANES, :] = jnp.full((SUBLANES, blk), NEG, F32)

    nqs = blk // MOBA_QLANES
    chains = [(hh, qs) for hh in range(2) for qs in range(nqs)]

    def qcols(qs):
        return slice(qs * MOBA_QLANES, (qs + 1) * MOBA_QLANES)

    def lanes_of_head(vals, hh):
        return jnp.concatenate(vals[hh * nqs:(hh + 1) * nqs], axis=1)

    def sel_row(hh, j):
        return sel_scr[hh, pl.ds(j, 1), :]

    ones_rows = jnp.ones((2 * SUBLANES, blk), BF16)

    def pv(hh, p, j):
        vj = vt_ref[hh * HEAD_DIM:(hh + 1) * HEAD_DIM, pl.ds(pl.multiple_of(j * blk, blk), blk)]
        return jnp.dot(jnp.concatenate([vj, ones_rows], axis=0), p, preferred_element_type=F32)

    def softmax_step(m_prev, s):
        m_new = jnp.maximum(m_prev, jnp.max(s, axis=0, keepdims=True))
        alpha = jnp.exp2(m_prev - m_new)
        return m_new, alpha, jnp.exp2(s - m_new).astype(BF16)

    def far_scores(b, buf):
        kj = k_block(jnp.minimum(b, i))
        row = jnp.where(b < jp, b, nb)
        for hh in range(2):
            ss_scr[buf, hh] = scores(kj, hh) + sel_row(hh, row)

    far_scores(0, 0)
    ss = [raw_own[hh] + tbl_ref[hh, 0] for hh in range(2)]
    ss_prev = [raw_prev[hh] + tbl_ref[hh, 1] + sel_row(hh, jp) for hh in range(2)]
    ms = [jnp.max(ss[hh][:, qcols(qs)], axis=0, keepdims=True) for hh, qs in chains]
    ps = [jnp.exp2(ss[hh][:, qcols(qs)] - m).astype(BF16) for (hh, qs), m in zip(chains, ms)]
    accs = [pv(hh, lanes_of_head(ps, hh), i) for hh in range(2)]
    steps = [softmax_step(m, ss_prev[hh][:, qcols(qs)]) for (hh, qs), m in zip(chains, ms)]
    accs = [lanes_of_head([st[1] for st in steps], hh) * accs[hh]
            + pv(hh, lanes_of_head([st[2] for st in steps], hh), jp) for hh in range(2)]
    ms = [st[0] for st in steps]

    def v_block(b):
        return jnp.clip(b, 0, i)

    def half(b, cur, ms):
        nxt = 1 - cur
        far_scores(b + 1, nxt)
        pvs = [pv(hh, p_scr[nxt, hh], v_block(b - 1)) for hh in range(2)]
        new_ms = []
        for c, (hh, qs) in enumerate(chains):
            m_new, alpha, p = softmax_step(ms[c], ss_scr[cur, hh, :, qcols(qs)])
            p_scr[cur, hh, :, qcols(qs)] = p
            acc_scr[hh, :, qcols(qs)] = alpha * (acc_scr[hh, :, qcols(qs)] + pvs[hh][:, qcols(qs)])
            new_ms.append(m_new)
        return tuple(new_ms)

    def two_blocks(t, ms):
        return half(2 * t + 1, 1, half(2 * t, 0, ms))

    for hh in range(2):
        acc_scr[hh] = accs[hh]
        p_scr[1, hh] = jnp.zeros((blk, blk), BF16)
    n_pairs = (jp + 1) // 2
    lax.fori_loop(0, n_pairs, two_blocks, tuple(ms))
    accs = [acc_scr[hh] + pv(hh, p_scr[1, hh], v_block(2 * n_pairs - 1)) for hh in range(2)]
    o_t = jnp.concatenate([acc[:HEAD_DIM] / acc[HEAD_DIM:HEAD_DIM + 1] for acc in accs], axis=0)
    o_ref[...] = o_t.T.astype(o_ref.dtype)


def _t5_bucket_np(dist):
    max_exact = REL_BUCKETS // 2
    d = np.maximum(dist, 0)
    df = np.maximum(d, 1).astype(np.float32)
    large = max_exact + (np.log(df / max_exact) / math.log(REL_MAX_DIST / max_exact)
                         * (REL_BUCKETS - max_exact)).astype(np.int32)
    large = np.minimum(large, REL_BUCKETS - 1)
    return np.where(d < max_exact, d, large)


def _moba_bias_tables(rel_bias):
    blk = MOBA_BLOCK
    assert int(_t5_bucket_np(np.array([blk + 1]))[0]) == REL_BUCKETS - 1
    dist = np.arange(-(blk - 1), blk)

    def by_distance(d):
        onehot = (_t5_bucket_np(d)[None, :] == np.arange(REL_BUCKETS)[:, None]).astype(np.float32)
        return jnp.dot(rel_bias.astype(F32), onehot, precision=lax.Precision.HIGHEST)

    def toeplitz(x):
        h = x.shape[0]
        y = jnp.concatenate([x, jnp.zeros((h, 1), x.dtype)], axis=1)
        skew = jnp.tile(y, (1, blk))[:, :blk * (2 * blk - 1)].reshape(h, blk, 2 * blk - 1)
        return skew[:, :, blk - 1:]

    far = rel_bias[:, REL_BUCKETS - 1].astype(F32)[:, None, None]
    causal = (np.arange(blk)[None, :] >= np.arange(blk)[:, None])[None]
    log2e = math.log2(math.e)
    own = jnp.where(causal, (toeplitz(by_distance(dist)) - far) * log2e, NEG)
    prev = (toeplitz(by_distance(dist + blk)) - far) * log2e
    return jnp.stack([own, prev], axis=1)


def _moba_attention(qt, k, vt, kmean, tbl):
    b, s, d = k.shape
    blk = MOBA_BLOCK
    nb = s // blk
    nhp = d // LANES
    return pl.pallas_call(
        functools.partial(_moba_kernel, nb=nb),
        grid=(b, nhp, nb),
        in_specs=[
            pl.BlockSpec((None, LANES, blk), lambda bi, hp, i: (bi, hp, i)),
            pl.BlockSpec((None, s, LANES), lambda bi, hp, i: (bi, 0, hp)),
            pl.BlockSpec((None, LANES, s), lambda bi, hp, i: (bi, hp, 0)),
            pl.BlockSpec((None, nb, LANES), lambda bi, hp, i: (bi, 0, hp)),
            pl.BlockSpec((2, 2, blk, blk), lambda bi, hp, i: (hp, 0, 0, 0)),
        ],
        out_specs=pl.BlockSpec((None, blk, LANES), lambda bi, hp, i: (bi, i, hp)),
        out_shape=jax.ShapeDtypeStruct((b, s, d), BF16),
        scratch_shapes=[
            pltpu.VMEM((2, nb + SUBLANES, blk), F32),
            pltpu.VMEM((2, 2, blk, blk), F32),
            pltpu.VMEM((2, 2, blk, blk), BF16),
            pltpu.VMEM((2, HEAD_DIM + 2 * SUBLANES, blk), F32),
        ],
        compiler_params=_cparams(("parallel", "parallel", "arbitrary")),
        name="moba_attention",
    )(qt, k, vt, kmean.reshape(b, nb, d), tbl)


def _peerq_epilogue(x_new, gf_ref, wpq_ref, x_out_ref, h_ref, pq_ref):
    x_out_ref[...] = x_new
    h = _rms(x_new, gf_ref[...]).astype(BF16)
    h_ref[...] = h
    pq_ref[...] = jnp.dot(h, wpq_ref[...], preferred_element_type=F32)


def _attn_out_kernel(o_ref, wo_ref, x_ref, gf_ref, wpq_ref, x_out_ref, h_ref, pq_ref):
    x_new = x_ref[...] + jnp.dot(o_ref[...], wo_ref[...], preferred_element_type=F32)
    _peerq_epilogue(x_new, gf_ref, wpq_ref, x_out_ref, h_ref, pq_ref)


def _attn_out_proj(o2d, wo_bf16, x2d, gf, wpq_bf16):
    t, d = x2d.shape
    nq = wpq_bf16.shape[1]
    tm = TM_PROJ
    return pl.pallas_call(
        _attn_out_kernel,
        grid=(t // tm,),
        in_specs=[
            pl.BlockSpec((tm, d), lambda i: (i, 0)),
            pl.BlockSpec((d, d), lambda i: (0, 0)),
            pl.BlockSpec((tm, d), lambda i: (i, 0)),
            pl.BlockSpec((1, d), lambda i: (0, 0)),
            pl.BlockSpec((d, nq), lambda i: (0, 0)),
        ],
        out_specs=[
            pl.BlockSpec((tm, d), lambda i: (i, 0)),
            pl.BlockSpec((tm, d), lambda i: (i, 0)),
            pl.BlockSpec((tm, nq), lambda i: (i, 0)),
        ],
        out_shape=[
            jax.ShapeDtypeStruct((t, d), F32),
            jax.ShapeDtypeStruct((t, d), BF16),
            jax.ShapeDtypeStruct((t, nq), F32),
        ],
        compiler_params=_cparams(("parallel",)),
        name="attn_out_proj",
    )(o2d, wo_bf16, x2d, gf.reshape(1, d), wpq_bf16)


def _cmpx(vals, a, b):
    hi = jnp.maximum(vals[a], vals[b])
    lo = jnp.minimum(vals[a], vals[b])
    vals[a], vals[b] = hi, lo


def _bitonic_merge_desc(vals):
    n = len(vals)
    dist = n // 2
    while dist >= 1:
        for s in range(0, n, 2 * dist):
            for t in range(s, s + dist):
                _cmpx(vals, t, t + dist)
        dist //= 2
    return vals


def _sort_desc(vals):
    n = len(vals)
    if n == 1:
        return vals
    top = _sort_desc(vals[: n // 2])
    bot = _sort_desc(vals[n // 2:])
    return _bitonic_merge_desc(top + bot[::-1])


def _top_merge(a, b):
    n = len(a)
    return _bitonic_merge_desc([jnp.maximum(a[r], b[n - 1 - r]) for r in range(n)])


def _sorted_top16(vals):
    groups = [_sort_desc(vals[g:g + PEER_TOPK]) for g in range(0, len(vals), PEER_TOPK)]
    while len(groups) > 1:
        groups = [_top_merge(groups[g], groups[g + 1]) for g in range(0, len(groups), 2)]
    return groups[0]


def _dot_nt_3pass(a, b):
    a_hi, b_hi = a.astype(BF16), b.astype(BF16)
    a_lo = (a - a_hi.astype(F32)).astype(BF16)
    b_lo = (b - b_hi.astype(F32)).astype(BF16)
    dot = functools.partial(lax.dot_general, dimension_numbers=NT_DIMS, preferred_element_type=F32)
    return dot(a_hi, b_hi) + (dot(a_hi, b_lo) + dot(a_lo, b_hi))


def _router_kernel(pq_ref, keys_ref, st_ref, sc_ref, scr_ref):
    nch = TT_ROUTER // LANES
    tops = []
    for c in range(2):
        qc = pq_ref[:, c * PEER_HALF:(c + 1) * PEER_HALF]
        s_t = _dot_nt_3pass(keys_ref[0, c], qc)
        st_ref[0, c] = s_t
        for ch in range(nch):
            scr_ref[c, pl.ds(ch, PEER_NKEYS, stride=nch), :] = s_t[:, ch * LANES:(ch + 1) * LANES]
        vals = [scr_ref[c, k * nch:(k + 1) * nch, :] for k in range(PEER_NKEYS)]
        tops.append(_sorted_top16(vals))
    a, b = tops
    cands = [a[r] + b[c] for r in range(PEER_TOPK) for c in range(PEER_TOPK)
             if (r + 1) * (c + 1) <= PEER_TOPK]
    npad = 1 << (len(cands) - 1).bit_length()
    cands = cands + [jnp.full_like(cands[0], NEG)] * (npad - len(cands))
    best = _sort_desc(cands)[:PEER_TOPK]
    z = jnp.ones_like(best[0])
    for r in range(1, PEER_TOPK):
        z = z + jnp.exp(best[r] - best[0])
    for r in range(PEER_TOPK):
        sc_ref[0, SC_TOP1 + r] = a[r]
        sc_ref[0, SC_TOP2 + r] = b[r]
    sc_ref[0, SC_TAU] = best[PEER_TOPK - 1]
    sc_ref[0, SC_ZINV] = 1.0 / z


def _peer_router(pq, sub_keys):
    t = pq.shape[0]
    tt = TT_ROUTER
    nch = tt // LANES
    hp = PEER_HEADS
    return pl.pallas_call(
        _router_kernel,
        grid=(t // tt, hp),
        in_specs=[
            pl.BlockSpec((tt, 2 * PEER_HALF), lambda i, h: (i, h)),
            pl.BlockSpec((1, 2, PEER_NKEYS, PEER_HALF), lambda i, h: (h, 0, 0, 0)),
        ],
        out_specs=[
            pl.BlockSpec((1, 2, PEER_NKEYS, tt), lambda i, h: (h, 0, 0, i)),
            pl.BlockSpec((1, SC_ROWS, nch, LANES), lambda i, h: (h, 0, i, 0)),
        ],
        out_shape=[
            jax.ShapeDtypeStruct((hp, 2, PEER_NKEYS, t), F32),
            jax.ShapeDtypeStruct((hp, SC_ROWS, t // LANES, LANES), F32),
        ],
        scratch_shapes=[pltpu.VMEM((2, PEER_NKEYS * nch, LANES), F32)],
        compiler_params=_cparams(("parallel", "parallel")),
        name="peer_router",
    )(pq, sub_keys)


def _gelu2(x):
    c0 = math.sqrt(2.0 / math.pi)
    z = x * (c0 + (c0 * 0.044715) * (x * x))
    return x + x * jnp.tanh(z)


def _dup_bf16_bits(v):
    bits = pltpu.bitcast(v.astype(BF16).astype(F32), jnp.uint32)
    return bits | (bits >> 16)


def _peer_kernel(h_ref, st_ref, sc_ref, u_ref, vt_ref, x_ref, gfin_ref, o_ref,
                 rank_scr, bw_scr, cnt_scr, aw_scr, act_scr, wa_scr, acc_scr, *, final_norm):
    e = pl.program_id(1)
    tt = h_ref.shape[0]
    nsub = CE_STEP // CE_SUB
    ipc = CE_SUB // PEER_NKEYS
    pack = 2 * SUBLANES

    @pl.when(e == 0)
    def _():
        acc_scr[...] = jnp.zeros_like(acc_scr)
        for hd in range(PEER_HEADS):
            s1 = st_ref[hd, 0]
            s2 = st_ref[hd, 1]
            tau = sc_ref[hd, SC_TAU:SC_TAU + 1, :]
            rank2 = jnp.zeros_like(s2)
            cnt = jnp.zeros_like(s1)
            for c in range(PEER_TOPK):
                b_c = sc_ref[hd, SC_TOP2 + c:SC_TOP2 + c + 1, :]
                rank2 = jnp.where(b_c > s2, c + 1.0, rank2)
                cnt = jnp.where(s1 + b_c >= tau, c + 1.0, cnt)
            a0 = sc_ref[hd, SC_TOP1:SC_TOP1 + 1, :]
            b0 = sc_ref[hd, SC_TOP2:SC_TOP2 + 1, :]
            zinv = sc_ref[hd, SC_ZINV:SC_ZINV + 1, :]
            rank_scr[hd] = rank2.astype(BF16)
            bw_scr[hd] = (jnp.exp(s2 - b0) * (0.5 * zinv)).astype(BF16)
            cnt_w = _dup_bf16_bits(cnt)
            a_w = _dup_bf16_bits(jnp.exp(s1 - a0))
            for lg in range(tt // LANES):
                cnt_scr[hd, lg] = cnt_w[:, lg * LANES:(lg + 1) * LANES]
                aw_scr[hd, lg] = a_w[:, lg * LANES:(lg + 1) * LANES]

    def row_bf16(tile, r):
        words = jnp.broadcast_to(tile[r:r + 1, :], (SUBLANES, LANES))
        return pltpu.bitcast(words, BF16)

    def up_proj(c):
        act_scr[c % 2] = lax.dot_general(u_ref[c * CE_SUB:(c + 1) * CE_SUB, :], h_ref[...], NT_DIMS,
                                         preferred_element_type=F32)

    def down_proj(c):
        acc_scr[...] += jnp.dot(vt_ref[:, c * CE_SUB:(c + 1) * CE_SUB], wa_scr[c % 2],
                                preferred_element_type=F32)

    def gate(c):
        njg = PEER_NKEYS // pack
        zero = jnp.zeros((pack, LANES), BF16)
        for lg in range(tt // LANES):
            lanes = slice(lg * LANES, (lg + 1) * LANES)
            w = [[None] * njg for _ in range(ipc)]
            for hd in range(PEER_HEADS):
                rank = [rank_scr[hd, jg * pack:(jg + 1) * pack, lanes] for jg in range(njg)]
                bw = [bw_scr[hd, jg * pack:(jg + 1) * pack, lanes] for jg in range(njg)]
                first = (e * nsub + c) * ipc
                base = pl.multiple_of((first // SUBLANES) * SUBLANES, SUBLANES)
                off = (c * ipc) % SUBLANES
                cnt_tile = cnt_scr[hd, lg, pl.ds(base, SUBLANES), :]
                a_tile = aw_scr[hd, lg, pl.ds(base, SUBLANES), :]
                for il in range(ipc):
                    cnt_row = row_bf16(cnt_tile, off + il)
                    a_row = row_bf16(a_tile, off + il)
                    for jg in range(njg):
                        term = jnp.where(rank[jg] < cnt_row, a_row * bw[jg], zero)
                        w[il][jg] = term if w[il][jg] is None else w[il][jg] + term
            for il in range(ipc):
                for jg in range(njg):
                    rows = slice(il * PEER_NKEYS + jg * pack, il * PEER_NKEYS + (jg + 1) * pack)
                    wa_scr[c % 2, rows, lanes] = w[il][jg] * _gelu2(act_scr[c % 2, rows, lanes].astype(BF16))

    up_proj(0)
    for c in range(nsub):
        if c + 1 < nsub:
            up_proj(c + 1)
        if c >= 1:
            down_proj(c - 1)
        gate(c)
    down_proj(nsub - 1)

    @pl.when(e == pl.num_programs(1) - 1)
    def _():
        y = x_ref[...] + acc_scr[...].T
        if final_norm:
            y = _rms(y, gfin_ref[...])
        o_ref[...] = y


def _peer_experts(h_bf16, st, sc, u_bf16, vt_bf16, x2d, gfin, final_norm):
    t, d = x2d.shape
    ne = u_bf16.shape[0]
    tt, ce = TT_PEER, CE_STEP
    hp = PEER_HEADS
    assert (ce // PEER_NKEYS) % SUBLANES == 0 and SUBLANES % (CE_SUB // PEER_NKEYS) == 0
    return pl.pallas_call(
        functools.partial(_peer_kernel, final_norm=final_norm),
        grid=(t // tt, ne // ce),
        in_specs=[
            pl.BlockSpec((tt, d), lambda i, e: (i, 0)),
            pl.BlockSpec((hp, 2, PEER_NKEYS, tt), lambda i, e: (0, 0, 0, i)),
            pl.BlockSpec((hp, SC_ROWS, tt), lambda i, e: (0, 0, i)),
            pl.BlockSpec((ce, d), lambda i, e: (e, 0)),
            pl.BlockSpec((None, d, ce), lambda i, e: (e, 0, 0)),
            pl.BlockSpec((tt, d), lambda i, e: (i, 0)),
            pl.BlockSpec((1, d), lambda i, e: (0, 0)),
        ],
        out_specs=pl.BlockSpec((tt, d), lambda i, e: (i, 0)),
        out_shape=jax.ShapeDtypeStruct((t, d), F32),
        scratch_shapes=[
            pltpu.VMEM((hp, PEER_NKEYS, tt), BF16),
            pltpu.VMEM((hp, PEER_NKEYS, tt), BF16),
            pltpu.VMEM((hp, tt // LANES, PEER_NKEYS, LANES), jnp.uint32),
            pltpu.VMEM((hp, tt // LANES, PEER_NKEYS, LANES), jnp.uint32),
            pltpu.VMEM((2, CE_SUB, tt), F32),
            pltpu.VMEM((2, CE_SUB, tt), BF16),
            pltpu.VMEM((d, tt), F32),
        ],
        compiler_params=_cparams(("parallel", "arbitrary")),
        name="peer_experts",
    )(h_bf16, st, sc.reshape(hp, SC_ROWS, t), u_bf16, vt_bf16, x2d, gfin.reshape(1, d))


def _expert_major_t(v):
    ne, d = v.shape
    return v.astype(BF16).reshape(ne // CE_STEP, CE_STEP, d).transpose(0, 2, 1)


def _peer(h_bf16, pq, sub_keys, u_bf16, vt_bf16, x2d, gfin, final_norm):
    st, sc = _peer_router(pq, sub_keys)
    return _peer_experts(h_bf16, st, sc, u_bf16, vt_bf16, x2d, gfin, final_norm)


def _glu_kernel(x_ref, g_ref, w_ref, b_ref, u_ref, *, d):
    h = _rms(x_ref[...], g_ref[...]).astype(BF16)
    a = jnp.dot(h, w_ref[...], preferred_element_type=F32) + b_ref[...]
    u_ref[...] = a[:, :d] * jax.nn.sigmoid(a[:, d:])


def _glu_proj(x2d, g, w_bf16, bias):
    t, d = x2d.shape
    tm = TM_PROJ
    return pl.pallas_call(
        functools.partial(_glu_kernel, d=d),
        grid=(t // tm,),
        in_specs=[
            pl.BlockSpec((tm, d), lambda i: (i, 0)),
            pl.BlockSpec((1, d), lambda i: (0, 0)),
            pl.BlockSpec((d, 2 * d), lambda i: (0, 0)),
            pl.BlockSpec((1, 2 * d), lambda i: (0, 0)),
        ],
        out_specs=pl.BlockSpec((tm, d), lambda i: (i, 0)),
        out_shape=jax.ShapeDtypeStruct((t, d), F32),
        compiler_params=_cparams(("parallel",)),
        name="glu_proj",
    )(x2d, g.reshape(1, d), w_bf16, bias.reshape(1, 2 * d))


def _conv_kernel(ucur_ref, uprev_ref, wdw_ref, bdw_ref, lng_ref, lnb_ref, w2_ref, b2_ref,
                 x_ref, gf_ref, wpq_ref, x_out_ref, h_ref, pq_ref, ext_scr, y_scr):
    tm, d = ucur_ref.shape
    first = pl.program_id(1) == 0
    ext_scr[0:CONV_HALO, :] = jnp.where(first, 0.0, uprev_ref[...])
    ext_scr[CONV_HALO:, :] = ucur_ref[...]
    lead = CONV_HALO - (CONV_WIDTH - 1)

    def chunk(ci, carry):
        r0 = pl.multiple_of(ci * CONV_ROWS, CONV_ROWS)
        rows = CONV_ROWS + CONV_HALO
        window = ext_scr[pl.ds(r0, rows), :]
        shifted = [window] + [pltpu.roll(window, rows - r, 0) for r in range(1, SUBLANES)]
        acc = jnp.broadcast_to(bdw_ref[...], (CONV_ROWS, d))
        for w in range(CONV_WIDTH):
            start = (lead + w) // SUBLANES * SUBLANES
            tap = shifted[(lead + w) % SUBLANES][start:start + CONV_ROWS, :]
            acc = acc + wdw_ref[w:w + 1, :] * tap
        y_scr[pl.ds(r0, CONV_ROWS), :] = acc
        return carry

    lax.fori_loop(0, tm // CONV_ROWS, chunk, 0)
    y = y_scr[...]
    mu = jnp.mean(y, axis=-1, keepdims=True)
    var = jnp.mean(jnp.square(y - mu), axis=-1, keepdims=True)
    y = (y - mu) * lax.rsqrt(var + EPS) * lng_ref[...] + lnb_ref[...]
    y = y * jax.nn.sigmoid(y)
    x_new = x_ref[...] + jnp.dot(y.astype(BF16), w2_ref[...], preferred_element_type=F32) + b2_ref[...]
    _peerq_epilogue(x_new, gf_ref, wpq_ref, x_out_ref, h_ref, pq_ref)


def _conv_tail(u3, wdw, bdw, lng, lnb, w2_bf16, b2, x3, gf, wpq_bf16):
    b, s, d = x3.shape
    nq = wpq_bf16.shape[1]
    tm = TM_CONV
    hpb = tm // CONV_HALO
    row = lambda v: v.reshape(1, d)
    tok = pl.BlockSpec((None, tm, d), lambda bi, i: (bi, i, 0))
    vec = pl.BlockSpec((1, d), lambda bi, i: (0, 0))
    return pl.pallas_call(
        _conv_kernel,
        grid=(b, s // tm),
        in_specs=[
            tok,
            pl.BlockSpec((None, CONV_HALO, d), lambda bi, i: (bi, jnp.maximum(i * hpb - 1, 0), 0)),
            pl.BlockSpec((CONV_WIDTH, d), lambda bi, i: (0, 0)),
            vec, vec, vec,
            pl.BlockSpec((d, d), lambda bi, i: (0, 0)),
            vec,
            tok,
            vec,
            pl.BlockSpec((d, nq), lambda bi, i: (0, 0)),
        ],
        out_specs=[
            tok,
            tok,
            pl.BlockSpec((None, tm, nq), lambda bi, i: (bi, i, 0)),
        ],
        out_shape=[
            jax.ShapeDtypeStruct((b, s, d), F32),
            jax.ShapeDtypeStruct((b, s, d), BF16),
            jax.ShapeDtypeStruct((b, s, nq), F32),
        ],
        scratch_shapes=[
            pltpu.VMEM((tm + CONV_HALO, d), F32),
            pltpu.VMEM((tm, d), F32),
        ],
        compiler_params=_cparams(("parallel", "arbitrary")),
        name="conv_tail",
    )(u3, u3, wdw, row(bdw), row(lng), row(lnb), w2_bf16, row(b2), x3, row(gf), wpq_bf16)


def kernel(x, rel_bias, norm_mix, norm_ffn, attn_w_qkv, attn_w_o, conv_w_pw1, conv_b_pw1,
           conv_w_dw, conv_b_dw, conv_ln_g, conv_ln_b, conv_w_pw2, conv_b_pw2, peer_w_q,
           peer_sub_keys, peer_u, peer_v, norm_final):
    b, s, d = x.shape
    t = b * s
    depth = norm_mix.shape[0]
    assert d == N_HEADS * HEAD_DIM and s % MOBA_BLOCK == 0 and s // MOBA_BLOCK <= LANES
    assert t % TT_ROUTER == 0 and s % TM_CONV == 0 and t % TM_PROJ == 0
    x2d = x.reshape(t, d)
    for i in range(depth):
        j = i // 2
        wpq = peer_w_q[i].astype(BF16)
        if i % 2 == 0:
            wq, wk, wv = (attn_w_qkv[j][:, c * d:(c + 1) * d] for c in range(3))
            qt, k, vt, kmean = _qkv_proj(x2d.reshape(b, s, d), norm_mix[i], wq.T.astype(BF16),
                                         wk.astype(BF16), wv.T.astype(BF16))
            o = _moba_attention(qt, k, vt, kmean, _moba_bias_tables(rel_bias))
            x2d, h, pq = _attn_out_proj(o.reshape(t, d), attn_w_o[j].astype(BF16), x2d,
                                        norm_ffn[i], wpq)
        else:
            u = _glu_proj(x2d, norm_mix[i], conv_w_pw1[j].astype(BF16), conv_b_pw1[j])
            x3, h3, pq3 = _conv_tail(u.reshape(b, s, d), conv_w_dw[j], conv_b_dw[j], conv_ln_g[j],
                                     conv_ln_b[j], conv_w_pw2[j].astype(BF16), conv_b_pw2[j],
                                     x2d.reshape(b, s, d), norm_ffn[i], wpq)
            x2d, h, pq = x3.reshape(t, d), h3.reshape(t, d), pq3.reshape(t, -1)
        last = i == depth - 1
        x2d = _peer(h, pq, peer_sub_keys[i], peer_u[i].astype(BF16),
                    _expert_major_t(peer_v[i]), x2d, norm_final, final_norm=last)
    return x2d.reshape(b, s, d)
```

```python
import functools
import math

import numpy as np
import jax
import jax.numpy as jnp
from jax import lax
from jax.experimental import pallas as pl
from jax.experimental.pallas import tpu as pltpu

F32 = jnp.float32
BF16 = jnp.bfloat16
EPS = 1e-6
NEG = -1e30

N_HEADS = 16
HEAD_DIM = 64
MOBA_BLOCK = 256
MOBA_TOPK = 3
REL_BUCKETS = 32
REL_MAX_DIST = 128
CONV_WIDTH = 31
PEER_HEADS = 8
PEER_NKEYS = 128
PEER_HALF = 128
PEER_TOPK = 16
SC_TOP1, SC_TOP2, SC_TAU, SC_ZINV, SC_ROWS = 0, PEER_TOPK, 2 * PEER_TOPK, 2 * PEER_TOPK + 1, 2 * PEER_TOPK + 2

LANES = 128
SUBLANES = 8
VMEM_LIMIT = 60 * 1024 * 1024

MOBA_QLANES = 128
TM_PROJ = 512
TT_ROUTER = 1024
TT_PEER = 512
CE_STEP = 2048
CE_SUB = 512
TM_CONV = 512
CONV_HALO = 32
CONV_ROWS = 16

NT_DIMS = (((1,), (1,)), ((), ()))


def _cparams(sem):
    return pltpu.CompilerParams(dimension_semantics=sem, vmem_limit_bytes=VMEM_LIMIT)


def _rms(x, g):
    return x * lax.rsqrt(jnp.mean(x * x, axis=-1, keepdims=True) + EPS) * g


def _qkv_kernel(x_ref, g_ref, wqt_ref, wk_ref, wvt_ref, qt_ref, k_ref, vt_ref, km_ref):
    h = _rms(x_ref[...], g_ref[...]).astype(BF16)
    scale = HEAD_DIM ** -0.5 * math.log2(math.e)
    qt = lax.dot_general(wqt_ref[...], h, NT_DIMS, preferred_element_type=F32)
    qt_ref[...] = (qt * scale).astype(BF16)
    k = jnp.dot(h, wk_ref[...], preferred_element_type=F32)
    k_ref[...] = k.astype(BF16)
    vt = lax.dot_general(wvt_ref[...], h, NT_DIMS, preferred_element_type=F32)
    vt_ref[...] = vt.astype(BF16)
    nblk = k.shape[0] // MOBA_BLOCK
    km_ref[...] = jnp.mean(k.reshape(nblk, MOBA_BLOCK, k.shape[1]), axis=1)


def _qkv_proj(x3, g, wqt_bf16, wk_bf16, wvt_bf16):
    b, s, d = x3.shape
    tm = TM_PROJ
    nblk = tm // MOBA_BLOCK
    nb = s // MOBA_BLOCK
    wspec = pl.BlockSpec((d, d), lambda bi, i: (0, 0))
    tspec = pl.BlockSpec((None, d, tm), lambda bi, i: (bi, 0, i))
    return pl.pallas_call(
        _qkv_kernel,
        grid=(b, s // tm),
        in_specs=[
            pl.BlockSpec((None, tm, d), lambda bi, i: (bi, i, 0)),
            pl.BlockSpec((1, d), lambda bi, i: (0, 0)),
            wspec, wspec, wspec,
        ],
        out_specs=[
            tspec,
            pl.BlockSpec((None, tm, d), lambda bi, i: (bi, i, 0)),
            tspec,
            pl.BlockSpec((None, None, nblk, d), lambda bi, i: (bi, i, 0, 0)),
        ],
        out_shape=[
            jax.ShapeDtypeStruct((b, d, s), BF16),
            jax.ShapeDtypeStruct((b, s, d), BF16),
            jax.ShapeDtypeStruct((b, d, s), BF16),
            jax.ShapeDtypeStruct((b, s // tm, nblk, d), F32),
        ],
        compiler_params=_cparams(("parallel", "parallel")),
        name="qkv_proj",
    )(x3, g.reshape(1, d), wqt_bf16, wk_bf16, wvt_bf16)


def _moba_kernel(qt_ref, k_ref, vt_ref, km_ref, tbl_ref, o_ref, sel_scr, ss_scr, p_scr, acc_scr, *, nb):
    blk = MOBA_BLOCK
    i = pl.program_id(2)
    jp = jnp.maximum(i - 1, 0)
    qt = qt_ref[...]
    frow = lax.broadcasted_iota(jnp.int32, qt.shape, 0)
    blk_id = lax.broadcasted_iota(jnp.int32, (nb, blk), 0)
    valid = blk_id < i
    km = km_ref[...]

    qh = [jnp.where(frow // HEAD_DIM == hh, qt, jnp.zeros_like(qt)) for hh in range(2)]

    def k_block(j):
        return k_ref[pl.ds(pl.multiple_of(j * blk, blk), blk), :]

    def scores(kj, hh):
        return jnp.dot(kj, qh[hh], preferred_element_type=F32)

    k_own, k_prev = k_block(i), k_block(jp)
    raw_own = [scores(k_own, hh) for hh in range(2)]
    raw_prev = [scores(k_prev, hh) for hh in range(2)]

    km_hi = km.astype(BF16)
    km_lo = (km - km_hi.astype(F32)).astype(BF16)
    for hh in range(2):
        gate = (jnp.dot(km_hi, qh[hh], preferred_element_type=F32)
                + jnp.dot(km_lo, qh[hh], preferred_element_type=F32))
        gate = jnp.where(valid, gate, NEG)
        g = gate
        for _ in range(MOBA_TOPK - 1):
            g = jnp.where(g >= jnp.max(g, axis=0, keepdims=True), NEG, g)
        tau = jnp.max(g, axis=0, keepdims=True)
        sel = jnp.where(valid, jnp.where(gate >= tau, 0.0, NEG), NEG)
        sel_scr[hh, 0:nb, :] = sel
        sel_scr[hh, nb:nb + SUBLANES, :] = jnp.full((SUBLANES, blk), NEG, F32)

    nqs = blk // MOBA_QLANES
    chains = [(hh, qs) for hh in range(2) for qs in range(nqs)]

    def qcols(qs):
        return slice(qs * MOBA_QLANES, (qs + 1) * MOBA_QLANES)

    def lanes_of_head(vals, hh):
        return jnp.concatenate(vals[hh * nqs:(hh + 1) * nqs], axis=1)

    def sel_row(hh, j):
        return sel_scr[hh, pl.ds(j, 1), :]

    ones_rows = jnp.ones((2 * SUBLANES, blk), BF16)

    def pv(hh, p, j):
        vj = vt_ref[hh * HEAD_DIM:(hh + 1) * HEAD_DIM, pl.ds(pl.multiple_of(j * blk, blk), blk)]
        return jnp.dot(jnp.concatenate([vj, ones_rows], axis=0), p, preferred_element_type=F32)

    def softmax_step(m_prev, s):
        m_new = jnp.maximum(m_prev, jnp.max(s, axis=0, keepdims=True))
        alpha = jnp.exp2(m_prev - m_new)
        return m_new, alpha, jnp.exp2(s - m_new).astype(BF16)

    def far_scores(b, buf):
        kj = k_block(jnp.minimum(b, i))
        row = jnp.where(b < jp, b, nb)
        for hh in range(2):
            ss_scr[buf, hh] = scores(kj, hh) + sel_row(hh, row)

    far_scores(0, 0)
    ss = [raw_own[hh] + tbl_ref[hh, 0] for hh in range(2)]
    ss_prev = [raw_prev[hh] + tbl_ref[hh, 1] + sel_row(hh, jp) for hh in range(2)]
    ms = [jnp.max(ss[hh][:, qcols(qs)], axis=0, keepdims=True) for hh, qs in chains]
    ps = [jnp.exp2(ss[hh][:, qcols(qs)] - m).astype(BF16) for (hh, qs), m in zip(chains, ms)]
    accs = [pv(hh, lanes_of_head(ps, hh), i) for hh in range(2)]
    steps = [softmax_step(m, ss_prev[hh][:, qcols(qs)]) for (hh, qs), m in zip(chains, ms)]
    accs = [lanes_of_head([st[1] for st in steps], hh) * accs[hh]
            + pv(hh, lanes_of_head([st[2] for st in steps], hh), jp) for hh in range(2)]
    ms = [st[0] for st in steps]

    def v_block(b):
        return jnp.clip(b, 0, i)

    def half(b, cur, ms):
        nxt = 1 - cur
        far_scores(b + 1, nxt)
        pvs = [pv(hh, p_scr[nxt, hh], v_block(b - 1)) for hh in range(2)]
        new_ms = []
        for c, (hh, qs) in enumerate(chains):
            m_new, alpha, p = softmax_step(ms[c], ss_scr[cur, hh, :, qcols(qs)])
            p_scr[cur, hh, :, qcols(qs)] = p
            acc_scr[hh, :, qcols(qs)] = alpha * (acc_scr[hh, :, qcols(qs)] + pvs[hh][:, qcols(qs)])
            new_ms.append(m_new)
        return tuple(new_ms)

    def two_blocks(t, ms):
        return half(2 * t + 1, 1, half(2 * t, 0, ms))

    for hh in range(2):
        acc_scr[hh] = accs[hh]
        p_scr[1, hh] = jnp.zeros((blk, blk), BF16)
    n_pairs = (jp + 1) // 2
    lax.fori_loop(0, n_pairs, two_blocks, tuple(ms))
    accs = [acc_scr[hh] + pv(hh, p_scr[1, hh], v_block(2 * n_pairs - 1)) for hh in range(2)]
    o_t = jnp.concatenate([acc[:HEAD_DIM] / acc[HEAD_DIM:HEAD_DIM + 1] for acc in accs], axis=0)
    o_ref[...] = o_t.T.astype(o_ref.dtype)


def _t5_bucket_np(dist):
    max_exact = REL_BUCKETS // 2
    d = np.maximum(dist, 0)
    df = np.maximum(d, 1).astype(np.float32)
    large = max_exact + (np.log(df / max_exact) / math.log(REL_MAX_DIST / max_exact)
                         * (REL_BUCKETS - max_exact)).astype(np.int32)
    large = np.minimum(large, REL_BUCKETS - 1)
    return np.where(d < max_exact, d, large)


def _moba_bias_tables(rel_bias):
    blk = MOBA_BLOCK
    assert int(_t5_bucket_np(np.array([blk + 1]))[0]) == REL_BUCKETS - 1
    dist = np.arange(-(blk - 1), blk)

    def by_distance(d):
        onehot = (_t5_bucket_np(d)[None, :] == np.arange(REL_BUCKETS)[:, None]).astype(np.float32)
        return jnp.dot(rel_bias.astype(F32), onehot, precision=lax.Precision.HIGHEST)

    def toeplitz(x):
        h = x.shape[0]
        y = jnp.concatenate([x, jnp.zeros((h, 1), x.dtype)], axis=1)
        skew = jnp.tile(y, (1, blk))[:, :blk * (2 * blk - 1)].reshape(h, blk, 2 * blk - 1)
        return skew[:, :, blk - 1:]

    far = rel_bias[:, REL_BUCKETS - 1].astype(F32)[:, None, None]
    causal = (np.arange(blk)[None, :] >= np.arange(blk)[:, None])[None]
    log2e = math.log2(math.e)
    own = jnp.where(causal, (toeplitz(by_distance(dist)) - far) * log2e, NEG)
    prev = (toeplitz(by_distance(dist + blk)) - far) * log2e
    return jnp.stack([own, prev], axis=1)


def _moba_attention(qt, k, vt, kmean, tbl):
    b, s, d = k.shape
    blk = MOBA_BLOCK
    nb = s // blk
    nhp = d // LANES
    return pl.pallas_call(
        functools.partial(_moba_kernel, nb=nb),
        grid=(b, nhp, nb),
        in_specs=[
            pl.BlockSpec((None, LANES, blk), lambda bi, hp, i: (bi, hp, i)),
            pl.BlockSpec((None, s, LANES), lambda bi, hp, i: (bi, 0, hp)),
            pl.BlockSpec((None, LANES, s), lambda bi, hp, i: (bi, hp, 0)),
            pl.BlockSpec((None, nb, LANES), lambda bi, hp, i: (bi, 0, hp)),
            pl.BlockSpec((2, 2, blk, blk), lambda bi, hp, i: (hp, 0, 0, 0)),
        ],
        out_specs=pl.BlockSpec((None, blk, LANES), lambda bi, hp, i: (bi, i, hp)),
        out_shape=jax.ShapeDtypeStruct((b, s, d), BF16),
        scratch_shapes=[
            pltpu.VMEM((2, nb + SUBLANES, blk), F32),
            pltpu.VMEM((2, 2, blk, blk), F32),
            pltpu.VMEM((2, 2, blk, blk), BF16),
            pltpu.VMEM((2, HEAD_DIM + 2 * SUBLANES, blk), F32),
        ],
        compiler_params=_cparams(("parallel", "parallel", "arbitrary")),
        name="moba_attention",
    )(qt, k, vt, kmean.reshape(b, nb, d), tbl)


def _peerq_epilogue(x_new, gf_ref, wpq_ref, x_out_ref, h_ref, pq_ref):
    x_out_ref[...] = x_new
    h = _rms(x_new, gf_ref[...]).astype(BF16)
    h_ref[...] = h
    pq_ref[...] = jnp.dot(h, wpq_ref[...], preferred_element_type=F32)


def _attn_out_kernel(o_ref, wo_ref, x_ref, gf_ref, wpq_ref, x_out_ref, h_ref, pq_ref):
    x_new = x_ref[...] + jnp.dot(o_ref[...], wo_ref[...], preferred_element_type=F32)
    _peerq_epilogue(x_new, gf_ref, wpq_ref, x_out_ref, h_ref, pq_ref)


def _attn_out_proj(o2d, wo_bf16, x2d, gf, wpq_bf16):
    t, d = x2d.shape
    nq = wpq_bf16.shape[1]
    tm = TM_PROJ
    return pl.pallas_call(
        _attn_out_kernel,
        grid=(t // tm,),
        in_specs=[
            pl.BlockSpec((tm, d), lambda i: (i, 0)),
            pl.BlockSpec((d, d), lambda i: (0, 0)),
            pl.BlockSpec((tm, d), lambda i: (i, 0)),
            pl.BlockSpec((1, d), lambda i: (0, 0)),
            pl.BlockSpec((d, nq), lambda i: (0, 0)),
        ],
        out_specs=[
            pl.BlockSpec((tm, d), lambda i: (i, 0)),
            pl.BlockSpec((tm, d), lambda i: (i, 0)),
            pl.BlockSpec((tm, nq), lambda i: (i, 0)),
        ],
        out_shape=[
            jax.ShapeDtypeStruct((t, d), F32),
            jax.ShapeDtypeStruct((t, d), BF16),
            jax.ShapeDtypeStruct((t, nq), F32),
        ],
        compiler_params=_cparams(("parallel",)),
        name="attn_out_proj",
    )(o2d, wo_bf16, x2d, gf.reshape(1, d), wpq_bf16)


def _cmpx(vals, a, b):
    hi = jnp.maximum(vals[a], vals[b])
    lo = jnp.minimum(vals[a], vals[b])
    vals[a], vals[b] = hi, lo


def _bitonic_merge_desc(vals):
    n = len(vals)
    dist = n // 2
    while dist >= 1:
        for s in range(0, n, 2 * dist):
            for t in range(s, s + dist):
                _cmpx(vals, t, t + dist)
        dist //= 2
    return vals


def _sort_desc(vals):
    n = len(vals)
    if n == 1:
        return vals
    top = _sort_desc(vals[: n // 2])
    bot = _sort_desc(vals[n // 2:])
    return _bitonic_merge_desc(top + bot[::-1])


def _top_merge(a, b):
    n = len(a)
    return _bitonic_merge_desc([jnp.maximum(a[r], b[n - 1 - r]) for r in range(n)])


def _sorted_top16(vals):
    groups = [_sort_desc(vals[g:g + PEER_TOPK]) for g in range(0, len(vals), PEER_TOPK)]
    while len(groups) > 1:
        groups = [_top_merge(groups[g], groups[g + 1]) for g in range(0, len(groups), 2)]
    return groups[0]


def _dot_nt_3pass(a, b):
    a_hi, b_hi = a.astype(BF16), b.astype(BF16)
    a_lo = (a - a_hi.astype(F32)).astype(BF16)
    b_lo = (b - b_hi.astype(F32)).astype(BF16)
    dot = functools.partial(lax.dot_general, dimension_numbers=NT_DIMS, preferred_element_type=F32)
    return dot(a_hi, b_hi) + (dot(a_hi, b_lo) + dot(a_lo, b_hi))


def _router_kernel(pq_ref, keys_ref, st_ref, sc_ref, scr_ref):
    nch = TT_ROUTER // LANES
    tops = []
    for c in range(2):
        qc = pq_ref[:, c * PEER_HALF:(c + 1) * PEER_HALF]
        s_t = _dot_nt_3pass(keys_ref[0, c], qc)
        st_ref[0, c] = s_t
        for ch in range(nch):
            scr_ref[c, pl.ds(ch, PEER_NKEYS, stride=nch), :] = s_t[:, ch * LANES:(ch + 1) * LANES]
        vals = [scr_ref[c, k * nch:(k + 1) * nch, :] for k in range(PEER_NKEYS)]
        tops.append(_sorted_top16(vals))
    a, b = tops
    cands = [a[r] + b[c] for r in range(PEER_TOPK) for c in range(PEER_TOPK)
             if (r + 1) * (c + 1) <= PEER_TOPK]
    npad = 1 << (len(cands) - 1).bit_length()
    cands = cands + [jnp.full_like(cands[0], NEG)] * (npad - len(cands))
    best = _sort_desc(cands)[:PEER_TOPK]
    z = jnp.ones_like(best[0])
    for r in range(1, PEER_TOPK):
        z = z + jnp.exp(best[r] - best[0])
    for r in range(PEER_TOPK):
        sc_ref[0, SC_TOP1 + r] = a[r]
        sc_ref[0, SC_TOP2 + r] = b[r]
    sc_ref[0, SC_TAU] = best[PEER_TOPK - 1]
    sc_ref[0, SC_ZINV] = 1.0 / z


def _peer_router(pq, sub_keys):
    t = pq.shape[0]
    tt = TT_ROUTER
    nch = tt // LANES
    hp = PEER_HEADS
    return pl.pallas_call(
        _router_kernel,
        grid=(t // tt, hp),
        in_specs=[
            pl.BlockSpec((tt, 2 * PEER_HALF), lambda i, h: (i, h)),
            pl.BlockSpec((1, 2, PEER_NKEYS, PEER_HALF), lambda i, h: (h, 0, 0, 0)),
        ],
        out_specs=[
            pl.BlockSpec((1, 2, PEER_NKEYS, tt), lambda i, h: (h, 0, 0, i)),
            pl.BlockSpec((1, SC_ROWS, nch, LANES), lambda i, h: (h, 0, i, 0)),
        ],
        out_shape=[
            jax.ShapeDtypeStruct((hp, 2, PEER_NKEYS, t), F32),
            jax.ShapeDtypeStruct((hp, SC_ROWS, t // LANES, LANES), F32),
        ],
        scratch_shapes=[pltpu.VMEM((2, PEER_NKEYS * nch, LANES), F32)],
        compiler_params=_cparams(("parallel", "parallel")),
        name="peer_router",
    )(pq, sub_keys)


def _gelu2(x):
    c0 = math.sqrt(2.0 / math.pi)
    z = x * (c0 + (c0 * 0.044715) * (x * x))
    return x + x * jnp.tanh(z)


def _dup_bf16_bits(v):
    bits = pltpu.bitcast(v.astype(BF16).astype(F32), jnp.uint32)
    return bits | (bits >> 16)


def _peer_kernel(h_ref, st_ref, sc_ref, u_ref, vt_ref, x_ref, gfin_ref, o_ref,
                 rank_scr, bw_scr, cnt_scr, aw_scr, act_scr, wa_scr, acc_scr, *, final_norm):
    e = pl.program_id(1)
    tt = h_ref.shape[0]
    nsub = CE_STEP // CE_SUB
    ipc = CE_SUB // PEER_NKEYS
    pack = 2 * SUBLANES

    @pl.when(e == 0)
    def _():
        acc_scr[...] = jnp.zeros_like(acc_scr)
        for hd in range(PEER_HEADS):
            s1 = st_ref[hd, 0]
            s2 = st_ref[hd, 1]
            tau = sc_ref[hd, SC_TAU:SC_TAU + 1, :]
            rank2 = jnp.zeros_like(s2)
            cnt = jnp.zeros_like(s1)
            for c in range(PEER_TOPK):
                b_c = sc_ref[hd, SC_TOP2 + c:SC_TOP2 + c + 1, :]
                rank2 = jnp.where(b_c > s2, c + 1.0, rank2)
                cnt = jnp.where(s1 + b_c >= tau, c + 1.0, cnt)
            a0 = sc_ref[hd, SC_TOP1:SC_TOP1 + 1, :]
            b0 = sc_ref[hd, SC_TOP2:SC_TOP2 + 1, :]
            zinv = sc_ref[hd, SC_ZINV:SC_ZINV + 1, :]
            rank_scr[hd] = rank2.astype(BF16)
            bw_scr[hd] = (jnp.exp(s2 - b0) * (0.5 * zinv)).astype(BF16)
            cnt_w = _dup_bf16_bits(cnt)
            a_w = _dup_bf16_bits(jnp.exp(s1 - a0))
            for lg in range(tt // LANES):
                cnt_scr[hd, lg] = cnt_w[:, lg * LANES:(lg + 1) * LANES]
                aw_scr[hd, lg] = a_w[:, lg * LANES:(lg + 1) * LANES]

    def row_bf16(tile, r):
        words = jnp.broadcast_to(tile[r:r + 1, :], (SUBLANES, LANES))
        return pltpu.bitcast(words, BF16)

    def up_proj(c):
        act_scr[c % 2] = lax.dot_general(u_ref[c * CE_SUB:(c + 1) * CE_SUB, :], h_ref[...], NT_DIMS,
                                         preferred_element_type=F32).astype(BF16)

    def down_proj(c):
        acc_scr[...] += jnp.dot(vt_ref[:, c * CE_SUB:(c + 1) * CE_SUB], wa_scr[c % 2],
                                preferred_element_type=F32)

    def gate(c):
        njg = PEER_NKEYS // pack
        zero = jnp.zeros((pack, LANES), BF16)
        for lg in range(tt // LANES):
            lanes = slice(lg * LANES, (lg + 1) * LANES)
            w = [[None] * njg for _ in range(ipc)]
            for hd in range(PEER_HEADS):
                rank = [rank_scr[hd, jg * pack:(jg + 1) * pack, lanes] for jg in range(njg)]
                bw = [bw_scr[hd, jg * pack:(jg + 1) * pack, lanes] for jg in range(njg)]
                first = (e * nsub + c) * ipc
                base = pl.multiple_of((first // SUBLANES) * SUBLANES, SUBLANES)
                off = (c * ipc) % SUBLANES
                cnt_tile = cnt_scr[hd, lg, pl.ds(base, SUBLANES), :]
                a_tile = aw_scr[hd, lg, pl.ds(base, SUBLANES), :]
                for il in range(ipc):
                    cnt_row = row_bf16(cnt_tile, off + il)
                    a_row = row_bf16(a_tile, off + il)
                    for jg in range(njg):
                        term = jnp.where(rank[jg] < cnt_row, a_row * bw[jg], zero)
                        w[il][jg] = term if w[il][jg] is None else w[il][jg] + term
            for il in range(ipc):
                for jg in range(njg):
                    rows = slice(il * PEER_NKEYS + jg * pack, il * PEER_NKEYS + (jg + 1) * pack)
                    wa_scr[c % 2, rows, lanes] = w[il][jg] * _gelu2(act_scr[c % 2, rows, lanes])

    up_proj(0)
    for c in range(nsub):
        if c + 1 < nsub:
            up_proj(c + 1)
        if c >= 1:
            down_proj(c - 1)
        gate(c)
    down_proj(nsub - 1)

    @pl.when(e == pl.num_programs(1) - 1)
    def _():
        y = x_ref[...] + acc_scr[...].T
        if final_norm:
            y = _rms(y, gfin_ref[...])
        o_ref[...] = y


def _peer_experts(h_bf16, st, sc, u_bf16, vt_bf16, x2d, gfin, final_norm):
    t, d = x2d.shape
    ne = u_bf16.shape[0]
    tt, ce = TT_PEER, CE_STEP
    hp = PEER_HEADS
    assert (ce // PEER_NKEYS) % SUBLANES == 0 and SUBLANES % (CE_SUB // PEER_NKEYS) == 0
    return pl.pallas_call(
        functools.partial(_peer_kernel, final_norm=final_norm),
        grid=(t // tt, ne // ce),
        in_specs=[
            pl.BlockSpec((tt, d), lambda i, e: (i, 0)),
            pl.BlockSpec((hp, 2, PEER_NKEYS, tt), lambda i, e: (0, 0, 0, i)),
            pl.BlockSpec((hp, SC_ROWS, tt), lambda i, e: (0, 0, i)),
            pl.BlockSpec((ce, d), lambda i, e: (e, 0)),
            pl.BlockSpec((None, d, ce), lambda i, e: (e, 0, 0)),
            pl.BlockSpec((tt, d), lambda i, e: (i, 0)),
            pl.BlockSpec((1, d), lambda i, e: (0, 0)),
        ],
        out_specs=pl.BlockSpec((tt, d), lambda i, e: (i, 0)),
        out_shape=jax.ShapeDtypeStruct((t, d), F32),
        scratch_shapes=[
            pltpu.VMEM((hp, PEER_NKEYS, tt), BF16),
            pltpu.VMEM((hp, PEER_NKEYS, tt), BF16),
            pltpu.VMEM((hp, tt // LANES, PEER_NKEYS, LANES), jnp.uint32),
            pltpu.VMEM((hp, tt // LANES, PEER_NKEYS, LANES), jnp.uint32),
            pltpu.VMEM((2, CE_SUB, tt), BF16),
            pltpu.VMEM((2, CE_SUB, tt), BF16),
            pltpu.VMEM((d, tt), F32),
        ],
        compiler_params=_cparams(("parallel", "arbitrary")),
        name="peer_experts",
    )(h_bf16, st, sc.reshape(hp, SC_ROWS, t), u_bf16, vt_bf16, x2d, gfin.reshape(1, d))


def _expert_major_t(v):
    ne, d = v.shape
    return v.astype(BF16).reshape(ne // CE_STEP, CE_STEP, d).transpose(0, 2, 1)


def _peer(h_bf16, pq, sub_keys, u_bf16, vt_bf16, x2d, gfin, final_norm):
    st, sc = _peer_router(pq, sub_keys)
    return _peer_experts(h_bf16, st, sc, u_bf16, vt_bf16, x2d, gfin, final_norm)


def _glu_kernel(x_ref, g_ref, w_ref, b_ref, u_ref, *, d):
    h = _rms(x_ref[...], g_ref[...]).astype(BF16)
    a = jnp.dot(h, w_ref[...], preferred_element_type=F32) + b_ref[...]
    u_ref[...] = a[:, :d] * jax.nn.sigmoid(a[:, d:])


def _glu_proj(x2d, g, w_bf16, bias):
    t, d = x2d.shape
    tm = TM_PROJ
    return pl.pallas_call(
        functools.partial(_glu_kernel, d=d),
        grid=(t // tm,),
        in_specs=[
            pl.BlockSpec((tm, d), lambda i: (i, 0)),
            pl.BlockSpec((1, d), lambda i: (0, 0)),
            pl.BlockSpec((d, 2 * d), lambda i: (0, 0)),
            pl.BlockSpec((1, 2 * d), lambda i: (0, 0)),
        ],
        out_specs=pl.BlockSpec((tm, d), lambda i: (i, 0)),
        out_shape=jax.ShapeDtypeStruct((t, d), F32),
        compiler_params=_cparams(("parallel",)),
        name="glu_proj",
    )(x2d, g.reshape(1, d), w_bf16, bias.reshape(1, 2 * d))


def _conv_kernel(ucur_ref, uprev_ref, wdw_ref, bdw_ref, lng_ref, lnb_ref, w2_ref, b2_ref,
                 x_ref, gf_ref, wpq_ref, x_out_ref, h_ref, pq_ref, ext_scr, y_scr):
    tm, d = ucur_ref.shape
    first = pl.program_id(1) == 0
    ext_scr[0:CONV_HALO, :] = jnp.where(first, 0.0, uprev_ref[...])
    ext_scr[CONV_HALO:, :] = ucur_ref[...]
    lead = CONV_HALO - (CONV_WIDTH - 1)

    def chunk(ci, carry):
        r0 = pl.multiple_of(ci * CONV_ROWS, CONV_ROWS)
        rows = CONV_ROWS + CONV_HALO
        window = ext_scr[pl.ds(r0, rows), :]
        shifted = [window] + [pltpu.roll(window, rows - r, 0) for r in range(1, SUBLANES)]
        acc = jnp.broadcast_to(bdw_ref[...], (CONV_ROWS, d))
        for w in range(CONV_WIDTH):
            start = (lead + w) // SUBLANES * SUBLANES
            tap = shifted[(lead + w) % SUBLANES][start:start + CONV_ROWS, :]
            acc = acc + wdw_ref[w:w + 1, :] * tap
        y_scr[pl.ds(r0, CONV_ROWS), :] = acc
        return carry

    lax.fori_loop(0, tm // CONV_ROWS, chunk, 0)
    y = y_scr[...]
    mu = jnp.mean(y, axis=-1, keepdims=True)
    var = jnp.mean(jnp.square(y - mu), axis=-1, keepdims=True)
    y = (y - mu) * lax.rsqrt(var + EPS) * lng_ref[...] + lnb_ref[...]
    y = y * jax.nn.sigmoid(y)
    x_new = x_ref[...] + jnp.dot(y.astype(BF16), w2_ref[...], preferred_element_type=F32) + b2_ref[...]
    _peerq_epilogue(x_new, gf_ref, wpq_ref, x_out_ref, h_ref, pq_ref)


def _conv_tail(u3, wdw, bdw, lng, lnb, w2_bf16, b2, x3, gf, wpq_bf16):
    b, s, d = x3.shape
    nq = wpq_bf16.shape[1]
    tm = TM_CONV
    hpb = tm // CONV_HALO
    row = lambda v: v.reshape(1, d)
    tok = pl.BlockSpec((None, tm, d), lambda bi, i: (bi, i, 0))
    vec = pl.BlockSpec((1, d), lambda bi, i: (0, 0))
    return pl.pallas_call(
        _conv_kernel,
        grid=(b, s // tm),
        in_specs=[
            tok,
            pl.BlockSpec((None, CONV_HALO, d), lambda bi, i: (bi, jnp.maximum(i * hpb - 1, 0), 0)),
            pl.BlockSpec((CONV_WIDTH, d), lambda bi, i: (0, 0)),
            vec, vec, vec,
            pl.BlockSpec((d, d), lambda bi, i: (0, 0)),
            vec,
            tok,
            vec,
            pl.BlockSpec((d, nq), lambda bi, i: (0, 0)),
        ],
        out_specs=[
            tok,
            tok,
            pl.BlockSpec((None, tm, nq), lambda bi, i: (bi, i, 0)),
        ],
        out_shape=[
            jax.ShapeDtypeStruct((b, s, d), F32),
            jax.ShapeDtypeStruct((b, s, d), BF16),
            jax.ShapeDtypeStruct((b, s, nq), F32),
        ],
        scratch_shapes=[
            pltpu.VMEM((tm + CONV_HALO, d), F32),
            pltpu.VMEM((tm, d), F32),
        ],
        compiler_params=_cparams(("parallel", "arbitrary")),
        name="conv_tail",
    )(u3, u3, wdw, row(bdw), row(lng), row(lnb), w2_bf16, row(b2), x3, row(gf), wpq_bf16)


def kernel(x, rel_bias, norm_mix, norm_ffn, attn_w_qkv, attn_w_o, conv_w_pw1, conv_b_pw1,
           conv_w_dw, conv_b_dw, conv_ln_g, conv_ln_b, conv_w_pw2, conv_b_pw2, peer_w_q,
           peer_sub_keys, peer_u, peer_v, norm_final):
    b, s, d = x.shape
    t = b * s
    depth = norm_mix.shape[0]
    assert d == N_HEADS * HEAD_DIM and s % MOBA_BLOCK == 0 and s // MOBA_BLOCK <= LANES
    assert t % TT_ROUTER == 0 and s % TM_CONV == 0 and t % TM_PROJ == 0
    x2d = x.reshape(t, d)
    for i in range(depth):
        j = i // 2
        wpq = peer_w_q[i].astype(BF16)
        if i % 2 == 0:
            wq, wk, wv = (attn_w_qkv[j][:, c * d:(c + 1) * d] for c in range(3))
            qt, k, vt, kmean = _qkv_proj(x2d.reshape(b, s, d), norm_mix[i], wq.T.astype(BF16),
                                         wk.astype(BF16), wv.T.astype(BF16))
            o = _moba_attention(qt, k, vt, kmean, _moba_bias_tables(rel_bias))
            x2d, h, pq = _attn_out_proj(o.reshape(t, d), attn_w_o[j].astype(BF16), x2d,
                                        norm_ffn[i], wpq)
        else:
            u = _glu_proj(x2d, norm_mix[i], conv_w_pw1[j].astype(BF16), conv_b_pw1[j])
            x3, h3, pq3 = _conv_tail(u.reshape(b, s, d), conv_w_dw[j], conv_b_dw[j], conv_ln_g[j],
                                     conv_ln_b[j], conv_w_pw2[j].astype(BF16), conv_b_pw2[j],
                                     x2d.reshape(b, s, d), norm_ffn[i], wpq)
            x2d, h, pq = x3.reshape(t, d), h3.reshape(t, d), pq3.reshape(t, -1)
        last = i == depth - 1
        x2d = _peer(h, pq, peer_sub_keys[i], peer_u[i].astype(BF16),
                    _expert_major_t(peer_v[i]), x2d, norm_final, final_norm=last)
    return x2d.reshape(b, s, d)
```

```python
import functools
import math

import numpy as np
import jax
import jax.numpy as jnp
from jax import lax
from jax.experimental import pallas as pl
from jax.experimental.pallas import tpu as pltpu

F32 = jnp.float32
BF16 = jnp.bfloat16
EPS = 1e-6
NEG = -1e30

N_HEADS = 16
HEAD_DIM = 64
MOBA_BLOCK = 256
MOBA_TOPK = 3
REL_BUCKETS = 32
REL_MAX_DIST = 128
CONV_WIDTH = 31
PEER_HEADS = 8
PEER_NKEYS = 128
PEER_HALF = 128
PEER_TOPK = 16
SC_TOP1, SC_TOP2, SC_TAU, SC_ZINV, SC_ROWS = 0, PEER_TOPK, 2 * PEER_TOPK, 2 * PEER_TOPK + 1, 2 * PEER_TOPK + 2

LANES = 128
SUBLANES = 8
VMEM_LIMIT = 60 * 1024 * 1024

MOBA_QLANES = 128
TM_PROJ = 512
TT_ROUTER = 1024
TT_PEER = 512
CE_STEP = 2048
CE_SUB = 512
TM_CONV = 512
CONV_HALO = 32
CONV_ROWS = 16

NT_DIMS = (((1,), (1,)), ((), ()))


def _cparams(sem):
    return pltpu.CompilerParams(dimension_semantics=sem, vmem_limit_bytes=VMEM_LIMIT)


def _rms(x, g):
    return x * lax.rsqrt(jnp.mean(x * x, axis=-1, keepdims=True) + EPS) * g


def _qkv_kernel(x_ref, g_ref, wqt_ref, wk_ref, wvt_ref, qt_ref, k_ref, vt_ref, km_ref):
    h = _rms(x_ref[...], g_ref[...]).astype(BF16)
    scale = HEAD_DIM ** -0.5 * math.log2(math.e)
    qt = lax.dot_general(wqt_ref[...], h, NT_DIMS, preferred_element_type=F32)
    qt_ref[...] = (qt * scale).astype(BF16)
    k = jnp.dot(h, wk_ref[...], preferred_element_type=F32)
    k_ref[...] = k.astype(BF16)
    vt = lax.dot_general(wvt_ref[...], h, NT_DIMS, preferred_element_type=F32)
    vt_ref[...] = vt.astype(BF16)
    nblk = k.shape[0] // MOBA_BLOCK
    km_ref[...] = jnp.mean(k.reshape(nblk, MOBA_BLOCK, k.shape[1]), axis=1)


def _qkv_proj(x3, g, wqt_bf16, wk_bf16, wvt_bf16):
    b, s, d = x3.shape
    tm = TM_PROJ
    nblk = tm // MOBA_BLOCK
    nb = s // MOBA_BLOCK
    wspec = pl.BlockSpec((d, d), lambda bi, i: (0, 0))
    tspec = pl.BlockSpec((None, d, tm), lambda bi, i: (bi, 0, i))
    return pl.pallas_call(
        _qkv_kernel,
        grid=(b, s // tm),
        in_specs=[
            pl.BlockSpec((None, tm, d), lambda bi, i: (bi, i, 0)),
            pl.BlockSpec((1, d), lambda bi, i: (0, 0)),
            wspec, wspec, wspec,
        ],
        out_specs=[
            tspec,
            pl.BlockSpec((None, tm, d), lambda bi, i: (bi, i, 0)),
            tspec,
            pl.BlockSpec((None, None, nblk, d), lambda bi, i: (bi, i, 0, 0)),
        ],
        out_shape=[
            jax.ShapeDtypeStruct((b, d, s), BF16),
            jax.ShapeDtypeStruct((b, s, d), BF16),
            jax.ShapeDtypeStruct((b, d, s), BF16),
            jax.ShapeDtypeStruct((b, s // tm, nblk, d), F32),
        ],
        compiler_params=_cparams(("parallel", "parallel")),
        name="qkv_proj",
    )(x3, g.reshape(1, d), wqt_bf16, wk_bf16, wvt_bf16)


def _moba_kernel(qt_ref, k_ref, vt_ref, km_ref, tbl_ref, o_ref, sel_scr, ss_scr, p_scr, acc_scr, *, nb):
    blk = MOBA_BLOCK
    i = pl.program_id(2)
    jp = jnp.maximum(i - 1, 0)
    qt = qt_ref[...]
    frow = lax.broadcasted_iota(jnp.int32, qt.shape, 0)
    blk_id = lax.broadcasted_iota(jnp.int32, (nb, blk), 0)
    valid = blk_id < i
    km = km_ref[...]

    qh = [jnp.where(frow // HEAD_DIM == hh, qt, jnp.zeros_like(qt)) for hh in range(2)]

    def k_block(j):
        return k_ref[pl.ds(pl.multiple_of(j * blk, blk), blk), :]

    def scores(kj, hh):
        return jnp.dot(kj, qh[hh], preferred_element_type=F32)

    k_own, k_prev = k_block(i), k_block(jp)
    raw_own = [scores(k_own, hh) for hh in range(2)]
    raw_prev = [scores(k_prev, hh) for hh in range(2)]

    km_hi = km.astype(BF16)
    km_lo = (km - km_hi.astype(F32)).astype(BF16)
    for hh in range(2):
        gate = (jnp.dot(km_hi, qh[hh], preferred_element_type=F32)
                + jnp.dot(km_lo, qh[hh], preferred_element_type=F32))
        gate = jnp.where(valid, gate, NEG)
        g = gate
        for _ in range(MOBA_TOPK - 1):
            g = jnp.where(g >= jnp.max(g, axis=0, keepdims=True), NEG, g)
        tau = jnp.max(g, axis=0, keepdims=True)
        sel = jnp.where(valid, jnp.where(gate >= tau, 0.0, NEG), NEG)
        sel_scr[hh, 0:nb, :] = sel
        sel_scr[hh, nb:nb + SUBLANES, :] = jnp.full((SUBLANES, blk), NEG, F32)

    nqs = blk // MOBA_QLANES
    chains = [(hh, qs) for hh in range(2) for qs in range(nqs)]

    def qcols(qs):
        return slice(qs * MOBA_QLANES, (qs + 1) * MOBA_QLANES)

    def lanes_of_head(vals, hh):
        return jnp.concatenate(vals[hh * nqs:(hh + 1) * nqs], axis=1)

    def sel_row(hh, j):
        return sel_scr[hh, pl.ds(j, 1), :]

    ones_rows = jnp.ones((2 * SUBLANES, blk), BF16)

    def pv(hh, p, j):
        vj = vt_ref[hh * HEAD_DIM:(hh + 1) * HEAD_DIM, pl.ds(pl.multiple_of(j * blk, blk), blk)]
        return jnp.dot(jnp.concatenate([vj, ones_rows], axis=0), p, preferred_element_type=F32)

    def softmax_step(m_prev, s):
        m_new = jnp.maximum(m_prev, jnp.max(s, axis=0, keepdims=True))
        alpha = jnp.exp2(m_prev - m_new)
        return m_new, alpha, jnp.exp2(s - m_new).astype(BF16)

    def far_scores(b, buf):
        kj = k_block(jnp.minimum(b, i))
        row = jnp.where(b < jp, b, nb)
        for hh in range(2):
            ss_scr[buf, hh] = scores(kj, hh) + sel_row(hh, row)

    far_scores(0, 0)
    ss = [raw_own[hh] + tbl_ref[hh, 0] for hh in range(2)]
    ss_prev = [raw_prev[hh] + tbl_ref[hh, 1] + sel_row(hh, jp) for hh in range(2)]
    ms = [jnp.max(ss[hh][:, qcols(qs)], axis=0, keepdims=True) for hh, qs in chains]
    ps = [jnp.exp2(ss[hh][:, qcols(qs)] - m).astype(BF16) for (hh, qs), m in zip(chains, ms)]
    accs = [pv(hh, lanes_of_head(ps, hh), i) for hh in range(2)]
    steps = [softmax_step(m, ss_prev[hh][:, qcols(qs)]) for (hh, qs), m in zip(chains, ms)]
    accs = [lanes_of_head([st[1] for st in steps], hh) * accs[hh]
            + pv(hh, lanes_of_head([st[2] for st in steps], hh), jp) for hh in range(2)]
    ms = [st[0] for st in steps]

    def v_block(b):
        return jnp.clip(b, 0, i)

    def half(b, cur, ms):
        nxt = 1 - cur
        far_scores(b + 1, nxt)
        pvs = [pv(hh, p_scr[nxt, hh], v_block(b - 1)) for hh in range(2)]
        new_ms = []
        for c, (hh, qs) in enumerate(chains):
            m_new, alpha, p = softmax_step(ms[c], ss_scr[cur, hh, :, qcols(qs)])
            p_scr[cur, hh, :, qcols(qs)] = p
            acc_scr[hh, :, qcols(qs)] = alpha * (acc_scr[hh, :, qcols(qs)] + pvs[hh][:, qcols(qs)])
            new_ms.append(m_new)
        return tuple(new_ms)

    def two_blocks(t, ms):
        return half(2 * t + 1, 1, half(2 * t, 0, ms))

    for hh in range(2):
        acc_scr[hh] = accs[hh]
        p_scr[1, hh] = jnp.zeros((blk, blk), BF16)
    n_pairs = (jp + 1) // 2
    lax.fori_loop(0, n_pairs, two_blocks, tuple(ms))
    accs = [acc_scr[hh] + pv(hh, p_scr[1, hh], v_block(2 * n_pairs - 1)) for hh in range(2)]
    o_t = jnp.concatenate([acc[:HEAD_DIM] / acc[HEAD_DIM:HEAD_DIM + 1] for acc in accs], axis=0)
    o_ref[...] = o_t.T.astype(o_ref.dtype)


def _t5_bucket_np(dist):
    max_exact = REL_BUCKETS // 2
    d = np.maximum(dist, 0)
    df = np.maximum(d, 1).astype(np.float32)
    large = max_exact + (np.log(df / max_exact) / math.log(REL_MAX_DIST / max_exact)
                         * (REL_BUCKETS - max_exact)).astype(np.int32)
    large = np.minimum(large, REL_BUCKETS - 1)
    return np.where(d < max_exact, d, large)


def _moba_bias_tables(rel_bias):
    blk = MOBA_BLOCK
    assert int(_t5_bucket_np(np.array([blk + 1]))[0]) == REL_BUCKETS - 1
    dist = np.arange(-(blk - 1), blk)

    def by_distance(d):
        onehot = (_t5_bucket_np(d)[None, :] == np.arange(REL_BUCKETS)[:, None]).astype(np.float32)
        return jnp.dot(rel_bias.astype(F32), onehot, precision=lax.Precision.HIGHEST)

    def toeplitz(x):
        h = x.shape[0]
        y = jnp.concatenate([x, jnp.zeros((h, 1), x.dtype)], axis=1)
        skew = jnp.tile(y, (1, blk))[:, :blk * (2 * blk - 1)].reshape(h, blk, 2 * blk - 1)
        return skew[:, :, blk - 1:]

    far = rel_bias[:, REL_BUCKETS - 1].astype(F32)[:, None, None]
    causal = (np.arange(blk)[None, :] >= np.arange(blk)[:, None])[None]
    log2e = math.log2(math.e)
    own = jnp.where(causal, (toeplitz(by_distance(dist)) - far) * log2e, NEG)
    prev = (toeplitz(by_distance(dist + blk)) - far) * log2e
    return jnp.stack([own, prev], axis=1)


def _moba_attention(qt, k, vt, kmean, tbl):
    b, s, d = k.shape
    blk = MOBA_BLOCK
    nb = s // blk
    nhp = d // LANES
    return pl.pallas_call(
        functools.partial(_moba_kernel, nb=nb),
        grid=(b, nhp, nb),
        in_specs=[
            pl.BlockSpec((None, LANES, blk), lambda bi, hp, i: (bi, hp, i)),
            pl.BlockSpec((None, s, LANES), lambda bi, hp, i: (bi, 0, hp)),
            pl.BlockSpec((None, LANES, s), lambda bi, hp, i: (bi, hp, 0)),
            pl.BlockSpec((None, nb, LANES), lambda bi, hp, i: (bi, 0, hp)),
            pl.BlockSpec((2, 2, blk, blk), lambda bi, hp, i: (hp, 0, 0, 0)),
        ],
        out_specs=pl.BlockSpec((None, blk, LANES), lambda bi, hp, i: (bi, i, hp)),
        out_shape=jax.ShapeDtypeStruct((b, s, d), BF16),
        scratch_shapes=[
            pltpu.VMEM((2, nb + SUBLANES, blk), F32),
            pltpu.VMEM((2, 2, blk, blk), F32),
            pltpu.VMEM((2, 2, blk, blk), BF16),
            pltpu.VMEM((2, HEAD_DIM + 2 * SUBLANES, blk), F32),
        ],
        compiler_params=_cparams(("parallel", "parallel", "arbitrary")),
        name="moba_attention",
    )(qt, k, vt, kmean.reshape(b, nb, d), tbl)


def _peerq_epilogue(x_new, gf_ref, wpq_ref, x_out_ref, h_ref, pq_ref):
    x_out_ref[...] = x_new
    h = _rms(x_new, gf_ref[...]).astype(BF16)
    h_ref[...] = h
    pq_ref[...] = jnp.dot(h, wpq_ref[...], preferred_element_type=F32)


def _attn_out_kernel(o_ref, wo_ref, x_ref, gf_ref, wpq_ref, x_out_ref, h_ref, pq_ref):
    x_new = x_ref[...] + jnp.dot(o_ref[...], wo_ref[...], preferred_element_type=F32)
    _peerq_epilogue(x_new, gf_ref, wpq_ref, x_out_ref, h_ref, pq_ref)


def _attn_out_proj(o2d, wo_bf16, x2d, gf, wpq_bf16):
    t, d = x2d.shape
    nq = wpq_bf16.shape[1]
    tm = TM_PROJ
    return pl.pallas_call(
        _attn_out_kernel,
        grid=(t // tm,),
        in_specs=[
            pl.BlockSpec((tm, d), lambda i: (i, 0)),
            pl.BlockSpec((d, d), lambda i: (0, 0)),
            pl.BlockSpec((tm, d), lambda i: (i, 0)),
            pl.BlockSpec((1, d), lambda i: (0, 0)),
            pl.BlockSpec((d, nq), lambda i: (0, 0)),
        ],
        out_specs=[
            pl.BlockSpec((tm, d), lambda i: (i, 0)),
            pl.BlockSpec((tm, d), lambda i: (i, 0)),
            pl.BlockSpec((tm, nq), lambda i: (i, 0)),
        ],
        out_shape=[
            jax.ShapeDtypeStruct((t, d), F32),
            jax.ShapeDtypeStruct((t, d), BF16),
            jax.ShapeDtypeStruct((t, nq), F32),
        ],
        compiler_params=_cparams(("parallel",)),
        name="attn_out_proj",
    )(o2d, wo_bf16, x2d, gf.reshape(1, d), wpq_bf16)


def _cmpx(vals, a, b):
    hi = jnp.maximum(vals[a], vals[b])
    lo = jnp.minimum(vals[a], vals[b])
    vals[a], vals[b] = hi, lo


def _bitonic_merge_desc(vals):
    n = len(vals)
    dist = n // 2
    while dist >= 1:
        for s in range(0, n, 2 * dist):
            for t in range(s, s + dist):
                _cmpx(vals, t, t + dist)
        dist //= 2
    return vals


def _sort_desc(vals):
    n = len(vals)
    if n == 1:
        return vals
    top = _sort_desc(vals[: n // 2])
    bot = _sort_desc(vals[n // 2:])
    return _bitonic_merge_desc(top + bot[::-1])


def _top_merge(a, b):
    n = len(a)
    return _bitonic_merge_desc([jnp.maximum(a[r], b[n - 1 - r]) for r in range(n)])


def _sorted_top16(vals):
    groups = [_sort_desc(vals[g:g + PEER_TOPK]) for g in range(0, len(vals), PEER_TOPK)]
    while len(groups) > 1:
        groups = [_top_merge(groups[g], groups[g + 1]) for g in range(0, len(groups), 2)]
    return groups[0]


def _dot_nt_3pass(a, b):
    a_hi, b_hi = a.astype(BF16), b.astype(BF16)
    a_lo = (a - a_hi.astype(F32)).astype(BF16)
    b_lo = (b - b_hi.astype(F32)).astype(BF16)
    dot = functools.partial(lax.dot_general, dimension_numbers=NT_DIMS, preferred_element_type=F32)
    return dot(a_hi, b_hi) + (dot(a_hi, b_lo) + dot(a_lo, b_hi))


def _router_kernel(pq_ref, keys_ref, st_ref, sc_ref, scr_ref):
    nch = TT_ROUTER // LANES
    tops = []
    for c in range(2):
        qc = pq_ref[:, c * PEER_HALF:(c + 1) * PEER_HALF]
        s_t = _dot_nt_3pass(keys_ref[0, c], qc)
        st_ref[0, c] = s_t
        for ch in range(nch):
            scr_ref[c, pl.ds(ch, PEER_NKEYS, stride=nch), :] = s_t[:, ch * LANES:(ch + 1) * LANES]
        vals = [scr_ref[c, k * nch:(k + 1) * nch, :] for k in range(PEER_NKEYS)]
        tops.append(_sorted_top16(vals))
    a, b = tops
    cands = [a[r] + b[c] for r in range(PEER_TOPK) for c in range(PEER_TOPK)
             if (r + 1) * (c + 1) <= PEER_TOPK]
    npad = 1 << (len(cands) - 1).bit_length()
    cands = cands + [jnp.full_like(cands[0], NEG)] * (npad - len(cands))
    best = _sort_desc(cands)[:PEER_TOPK]
    z = jnp.ones_like(best[0])
    for r in range(1, PEER_TOPK):
        z = z + jnp.exp(best[r] - best[0])
    for r in range(PEER_TOPK):
        sc_ref[0, SC_TOP1 + r] = a[r]
        sc_ref[0, SC_TOP2 + r] = b[r]
    sc_ref[0, SC_TAU] = best[PEER_TOPK - 1]
    sc_ref[0, SC_ZINV] = 1.0 / z


def _peer_router(pq, sub_keys):
    t = pq.shape[0]
    tt = TT_ROUTER
    nch = tt // LANES
    hp = PEER_HEADS
    return pl.pallas_call(
        _router_kernel,
        grid=(t // tt, hp),
        in_specs=[
            pl.BlockSpec((tt, 2 * PEER_HALF), lambda i, h: (i, h)),
            pl.BlockSpec((1, 2, PEER_NKEYS, PEER_HALF), lambda i, h: (h, 0, 0, 0)),
        ],
        out_specs=[
            pl.BlockSpec((1, 2, PEER_NKEYS, tt), lambda i, h: (h, 0, 0, i)),
            pl.BlockSpec((1, SC_ROWS, nch, LANES), lambda i, h: (h, 0, i, 0)),
        ],
        out_shape=[
            jax.ShapeDtypeStruct((hp, 2, PEER_NKEYS, t), F32),
            jax.ShapeDtypeStruct((hp, SC_ROWS, t // LANES, LANES), F32),
        ],
        scratch_shapes=[pltpu.VMEM((2, PEER_NKEYS * nch, LANES), F32)],
        compiler_params=_cparams(("parallel", "parallel")),
        name="peer_router",
    )(pq, sub_keys)


def _gelu2(x):
    c0 = math.sqrt(2.0 / math.pi)
    z = x * (c0 + (c0 * 0.044715) * (x * x))
    return x + x * jnp.tanh(z)


def _dup_bf16_bits(v):
    bits = pltpu.bitcast(v.astype(BF16).astype(F32), jnp.uint32)
    return bits | (bits >> 16)


def _peer_kernel(h_ref, st_ref, sc_ref, u_ref, vt_ref, x_ref, gfin_ref, o_ref,
                 rank_scr, bw_scr, cnt_scr, aw_scr, act_scr, wa_scr, acc_scr, *, final_norm):
    e = pl.program_id(1)
    tt = h_ref.shape[0]
    nsub = CE_STEP // CE_SUB
    ipc = CE_SUB // PEER_NKEYS
    pack = 2 * SUBLANES

    @pl.when(e == 0)
    def _():
        acc_scr[...] = jnp.zeros_like(acc_scr)
        for hd in range(PEER_HEADS):
            s1 = st_ref[hd, 0]
            s2 = st_ref[hd, 1]
            tau = sc_ref[hd, SC_TAU:SC_TAU + 1, :]
            rank2 = jnp.zeros_like(s2)
            cnt = jnp.zeros_like(s1)
            for c in range(PEER_TOPK):
                b_c = sc_ref[hd, SC_TOP2 + c:SC_TOP2 + c + 1, :]
                rank2 = jnp.where(b_c > s2, c + 1.0, rank2)
                cnt = jnp.where(s1 + b_c >= tau, c + 1.0, cnt)
            a0 = sc_ref[hd, SC_TOP1:SC_TOP1 + 1, :]
            b0 = sc_ref[hd, SC_TOP2:SC_TOP2 + 1, :]
            zinv = sc_ref[hd, SC_ZINV:SC_ZINV + 1, :]
            rank_scr[hd] = rank2.astype(BF16)
            bw_scr[hd] = (jnp.exp(s2 - b0) * (0.5 * zinv)).astype(BF16)
            cnt_w = _dup_bf16_bits(cnt)
            a_w = _dup_bf16_bits(jnp.exp(s1 - a0))
            for lg in range(tt // LANES):
                cnt_scr[hd, lg] = cnt_w[:, lg * LANES:(lg + 1) * LANES]
                aw_scr[hd, lg] = a_w[:, lg * LANES:(lg + 1) * LANES]

    def row_bf16(tile, r):
        words = jnp.broadcast_to(tile[r:r + 1, :], (SUBLANES, LANES))
        return pltpu.bitcast(words, BF16)

    def up_proj(c):
        act_scr[c % 2] = lax.dot_general(u_ref[c * CE_SUB:(c + 1) * CE_SUB, :], h_ref[...], NT_DIMS,
                                         preferred_element_type=F32)

    def down_proj(c):
        acc_scr[...] += jnp.dot(vt_ref[:, c * CE_SUB:(c + 1) * CE_SUB], wa_scr[c % 2],
                                preferred_element_type=F32)

    def gate(c):
        njg = PEER_NKEYS // pack
        zero = jnp.zeros((pack, LANES), BF16)
        for lg in range(tt // LANES):
            lanes = slice(lg * LANES, (lg + 1) * LANES)
            w = [[None] * njg for _ in range(ipc)]
            for hd in range(PEER_HEADS):
                rank = [rank_scr[hd, jg * pack:(jg + 1) * pack, lanes] for jg in range(njg)]
                bw = [bw_scr[hd, jg * pack:(jg + 1) * pack, lanes] for jg in range(njg)]
                first = (e * nsub + c) * ipc
                base = pl.multiple_of((first // SUBLANES) * SUBLANES, SUBLANES)
                off = (c * ipc) % SUBLANES
                cnt_tile = cnt_scr[hd, lg, pl.ds(base, SUBLANES), :]
                a_tile = aw_scr[hd, lg, pl.ds(base, SUBLANES), :]
                for il in range(ipc):
                    cnt_row = row_bf16(cnt_tile, off + il)
                    a_row = row_bf16(a_tile, off + il)
                    for jg in range(njg):
                        term = jnp.where(rank[jg] < cnt_row, a_row * bw[jg], zero)
                        w[il][jg] = term if w[il][jg] is None else w[il][jg] + term
            for il in range(ipc):
                for jg in range(njg):
                    rows = slice(il * PEER_NKEYS + jg * pack, il * PEER_NKEYS + (jg + 1) * pack)
                    wa_scr[c % 2, rows, lanes] = w[il][jg] * _gelu2(act_scr[c % 2, rows, lanes].astype(BF16))

    up_proj(0)
    for c in range(nsub):
        if c + 1 < nsub:
            up_proj(c + 1)
        if c >= 1:
            down_proj(c - 1)
        gate(c)
    down_proj(nsub - 1)

    @pl.when(e == pl.num_programs(1) - 1)
    def _():
        y = x_ref[...] + acc_scr[...].T
        if final_norm:
            y = _rms(y, gfin_ref[...])
        o_ref[...] = y


def _peer_experts(h_bf16, st, sc, u_bf16, vt_bf16, x2d, gfin, final_norm):
    t, d = x2d.shape
    ne = u_bf16.shape[0]
    tt, ce = TT_PEER, CE_STEP
    hp = PEER_HEADS
    assert (ce // PEER_NKEYS) % SUBLANES == 0 and SUBLANES % (CE_SUB // PEER_NKEYS) == 0
    return pl.pallas_call(
        functools.partial(_peer_kernel, final_norm=final_norm),
        grid=(t // tt, ne // ce),
        in_specs=[
            pl.BlockSpec((tt, d), lambda i, e: (i, 0)),
            pl.BlockSpec((hp, 2, PEER_NKEYS, tt), lambda i, e: (0, 0, 0, i)),
            pl.BlockSpec((hp, SC_ROWS, tt), lambda i, e: (0, 0, i)),
            pl.BlockSpec((ce, d), lambda i, e: (e, 0)),
            pl.BlockSpec((None, d, ce), lambda i, e: (e, 0, 0)),
            pl.BlockSpec((tt, d), lambda i, e: (i, 0)),
            pl.BlockSpec((1, d), lambda i, e: (0, 0)),
        ],
        out_specs=pl.BlockSpec((tt, d), lambda i, e: (i, 0)),
        out_shape=jax.ShapeDtypeStruct((t, d), F32),
        scratch_shapes=[
            pltpu.VMEM((hp, PEER_NKEYS, tt), BF16),
            pltpu.VMEM((hp, PEER_NKEYS, tt), BF16),
            pltpu.VMEM((hp, tt // LANES, PEER_NKEYS, LANES), jnp.uint32),
            pltpu.VMEM((hp, tt // LANES, PEER_NKEYS, LANES), jnp.uint32),
            pltpu.VMEM((2, CE_SUB, tt), F32),
            pltpu.VMEM((2, CE_SUB, tt), BF16),
            pltpu.VMEM((d, tt), F32),
        ],
        compiler_params=_cparams(("parallel", "arbitrary")),
        name="peer_experts",
    )(h_bf16, st, sc.reshape(hp, SC_ROWS, t), u_bf16, vt_bf16, x2d, gfin.reshape(1, d))


def _expert_major_t(v):
    ne, d = v.shape
    return v.astype(BF16).reshape(ne // CE_STEP, CE_STEP, d).transpose(0, 2, 1)


def _peer(h_bf16, pq, sub_keys, u_bf16, vt_bf16, x2d, gfin, final_norm):
    st, sc = _peer_router(pq, sub_keys)
    return _peer_experts(h_bf16, st, sc, u_bf16, vt_bf16, x2d, gfin, final_norm)


def _glu_kernel(x_ref, g_ref, w_ref, b_ref, u_ref, *, d):
    h = _rms(x_ref[...], g_ref[...]).astype(BF16)
    a = jnp.dot(h, w_ref[...], preferred_element_type=F32) + b_ref[...]
    u_ref[...] = a[:, :d] * jax.nn.sigmoid(a[:, d:])


def _glu_proj(x2d, g, w_bf16, bias):
    t, d = x2d.shape
    tm = TM_PROJ
    return pl.pallas_call(
        functools.partial(_glu_kernel, d=d),
        grid=(t // tm,),
        in_specs=[
            pl.BlockSpec((tm, d), lambda i: (i, 0)),
            pl.BlockSpec((1, d), lambda i: (0, 0)),
            pl.BlockSpec((d, 2 * d), lambda i: (0, 0)),
            pl.BlockSpec((1, 2 * d), lambda i: (0, 0)),
        ],
        out_specs=pl.BlockSpec((tm, d), lambda i: (i, 0)),
        out_shape=jax.ShapeDtypeStruct((t, d), F32),
        compiler_params=_cparams(("parallel",)),
        name="glu_proj",
    )(x2d, g.reshape(1, d), w_bf16, bias.reshape(1, 2 * d))


def _conv_kernel(ucur_ref, uprev_ref, wdw_ref, bdw_ref, lng_ref, lnb_ref, w2_ref, b2_ref,
                 x_ref, gf_ref, wpq_ref, x_out_ref, h_ref, pq_ref, ext_scr, y_scr):
    tm, d = ucur_ref.shape
    first = pl.program_id(1) == 0
    ext_scr[0:CONV_HALO, :] = jnp.where(first, 0.0, uprev_ref[...])
    ext_scr[CONV_HALO:, :] = ucur_ref[...]
    lead = CONV_HALO - (CONV_WIDTH - 1)

    def chunk(ci, carry):
        r0 = pl.multiple_of(ci * CONV_ROWS, CONV_ROWS)
        rows = CONV_ROWS + CONV_HALO
        window = ext_scr[pl.ds(r0, rows), :]
        shifted = [window] + [pltpu.roll(window, rows - r, 0) for r in range(1, SUBLANES)]
        acc = jnp.broadcast_to(bdw_ref[...], (CONV_ROWS, d))
        for w in range(CONV_WIDTH):
            start = (lead + w) // SUBLANES * SUBLANES
            tap = shifted[(lead + w) % SUBLANES][start:start + CONV_ROWS, :]
            acc = acc + wdw_ref[w:w + 1, :] * tap
        y_scr[pl.ds(r0, CONV_ROWS), :] = acc
        return carry

    lax.fori_loop(0, tm // CONV_ROWS, chunk, 0)
    y = y_scr[...]
    mu = jnp.mean(y, axis=-1, keepdims=True)
    var = jnp.mean(jnp.square(y - mu), axis=-1, keepdims=True)
    y = (y - mu) * lax.rsqrt(var + EPS) * lng_ref[...] + lnb_ref[...]
    y = y * jax.nn.sigmoid(y)
    x_new = x_ref[...] + jnp.dot(y.astype(BF16), w2_ref[...], preferred_element_type=F32) + b2_ref[...]
    _peerq_epilogue(x_new, gf_ref, wpq_ref, x_out_ref, h_ref, pq_ref)


def _conv_tail(u3, wdw, bdw, lng, lnb, w2_bf16, b2, x3, gf, wpq_bf16):
    b, s, d = x3.shape
    nq = wpq_bf16.shape[1]
    tm = TM_CONV
    hpb = tm // CONV_HALO
    row = lambda v: v.reshape(1, d)
    tok = pl.BlockSpec((None, tm, d), lambda bi, i: (bi, i, 0))
    vec = pl.BlockSpec((1, d), lambda bi, i: (0, 0))
    return pl.pallas_call(
        _conv_kernel,
        grid=(b, s // tm),
        in_specs=[
            tok,
            pl.BlockSpec((None, CONV_HALO, d), lambda bi, i: (bi, jnp.maximum(i * hpb - 1, 0), 0)),
            pl.BlockSpec((CONV_WIDTH, d), lambda bi, i: (0, 0)),
            vec, vec, vec,
            pl.BlockSpec((d, d), lambda bi, i: (0, 0)),
            vec,
            tok,
            vec,
            pl.BlockSpec((d, nq), lambda bi, i: (0, 0)),
        ],
        out_specs=[
            tok,
            tok,
            pl.BlockSpec((None, tm, nq), lambda bi, i: (bi, i, 0)),
        ],
        out_shape=[
            jax.ShapeDtypeStruct((b, s, d), F32),
            jax.ShapeDtypeStruct((b, s, d), BF16),
            jax.ShapeDtypeStruct((b, s, nq), F32),
        ],
        scratch_shapes=[
            pltpu.VMEM((tm + CONV_HALO, d), F32),
            pltpu.VMEM((tm, d), F32),
        ],
        compiler_params=_cparams(("parallel", "arbitrary")),
        name="conv_tail",
    )(u3, u3, wdw, row(bdw), row(lng), row(lnb), w2_bf16, row(b2), x3, row(gf), wpq_bf16)


def kernel(x, rel_bias, norm_mix, norm_ffn, attn_w_qkv, attn_w_o, conv_w_pw1, conv_b_pw1,
           conv_w_dw, conv_b_dw, conv_ln_g, conv_ln_b, conv_w_pw2, conv_b_pw2, peer_w_q,
           peer_sub_keys, peer_u, peer_v, norm_final):
    b, s, d = x.shape
    t = b * s
    depth = norm_mix.shape[0]
    assert d == N_HEADS * HEAD_DIM and s % MOBA_BLOCK == 0 and s // MOBA_BLOCK <= LANES
    assert t % TT_ROUTER == 0 and s % TM_CONV == 0 and t % TM_PROJ == 0
    x2d = x.reshape(t, d)
    for i in range(depth):
        j = i // 2
        wpq = peer_w_q[i].astype(BF16)
        if i % 2 == 0:
            wq, wk, wv = (attn_w_qkv[j][:, c * d:(c + 1) * d] for c in range(3))
            qt, k, vt, kmean = _qkv_proj(x2d.reshape(b, s, d), norm_mix[i], wq.T.astype(BF16),
                                         wk.astype(BF16), wv.T.astype(BF16))
            o = _moba_attention(qt, k, vt, kmean, _moba_bias_tables(rel_bias))
            x2d, h, pq = _attn_out_proj(o.reshape(t, d), attn_w_o[j].astype(BF16), x2d,
                                        norm_ffn[i], wpq)
        else:
            u = _glu_proj(x2d, norm_mix[i], conv_w_pw1[j].astype(BF16), conv_b_pw1[j])
            x3, h3, pq3 = _conv_tail(u.reshape(b, s, d), conv_w_dw[j], conv_b_dw[j], conv_ln_g[j],
                                     conv_ln_b[j], conv_w_pw2[j].astype(BF16), conv_b_pw2[j],
                                     x2d.reshape(b, s, d), norm_ffn[i], wpq)
            x2d, h, pq = x3.reshape(t, d), h3.reshape(t, d), pq3.reshape(t, -1)
        last = i == depth - 1
        x2d = _peer(h, pq, peer_sub_keys[i], peer_u[i].astype(BF16),
                    _expert_major_t(peer_v[i]), x2d, norm_final, final_norm=last)
    return x2d.reshape(b, s, d)
```

```python
import functools
import math

import numpy as np
import jax
import jax.numpy as jnp
from jax import lax
from jax.experimental import pallas as pl
from jax.experimental.pallas import tpu as pltpu

F32 = jnp.float32
BF16 = jnp.bfloat16
EPS = 1e-6
NEG = -1e30

N_HEADS = 16
HEAD_DIM = 64
MOBA_BLOCK = 256
MOBA_TOPK = 3
REL_BUCKETS = 32
REL_MAX_DIST = 128
CONV_WIDTH = 31
PEER_HEADS = 8
PEER_NKEYS = 128
PEER_HALF = 128
PEER_TOPK = 16
SC_TOP1, SC_TOP2, SC_TAU, SC_ZINV, SC_ROWS = 0, PEER_TOPK, 2 * PEER_TOPK, 2 * PEER_TOPK + 1, 2 * PEER_TOPK + 2

LANES = 128
SUBLANES = 8
VMEM_LIMIT = 60 * 1024 * 1024

MOBA_QLANES = 128
TM_PROJ = 512
TT_ROUTER = 1024
TT_PEER = 512
CE_STEP = 2048
CE_SUB = 512
TM_CONV = 512
CONV_HALO = 32
CONV_ROWS = 16

NT_DIMS = (((1,), (1,)), ((), ()))


def _cparams(sem):
    return pltpu.CompilerParams(dimension_semantics=sem, vmem_limit_bytes=VMEM_LIMIT)


def _rms(x, g):
    return x * lax.rsqrt(jnp.mean(x * x, axis=-1, keepdims=True) + EPS) * g


def _qkv_kernel(x_ref, g_ref, wqt_ref, wk_ref, wvt_ref, qt_ref, k_ref, vt_ref, km_ref):
    h = _rms(x_ref[...], g_ref[...]).astype(BF16)
    scale = HEAD_DIM ** -0.5 * math.log2(math.e)
    qt = lax.dot_general(wqt_ref[...], h, NT_DIMS, preferred_element_type=F32)
    qt_ref[...] = (qt * scale).astype(BF16)
    k = jnp.dot(h, wk_ref[...], preferred_element_type=F32)
    k_ref[...] = k.astype(BF16)
    vt = lax.dot_general(wvt_ref[...], h, NT_DIMS, preferred_element_type=F32)
    vt_ref[...] = vt.astype(BF16)
    nblk = k.shape[0] // MOBA_BLOCK
    km_ref[...] = jnp.mean(k.reshape(nblk, MOBA_BLOCK, k.shape[1]), axis=1)


def _qkv_proj(x3, g, wqt_bf16, wk_bf16, wvt_bf16):
    b, s, d = x3.shape
    tm = TM_PROJ
    nblk = tm // MOBA_BLOCK
    nb = s // MOBA_BLOCK
    wspec = pl.BlockSpec((d, d), lambda bi, i: (0, 0))
    tspec = pl.BlockSpec((None, d, tm), lambda bi, i: (bi, 0, i))
    return pl.pallas_call(
        _qkv_kernel,
        grid=(b, s // tm),
        in_specs=[
            pl.BlockSpec((None, tm, d), lambda bi, i: (bi, i, 0)),
            pl.BlockSpec((1, d), lambda bi, i: (0, 0)),
            wspec, wspec, wspec,
        ],
        out_specs=[
            tspec,
            pl.BlockSpec((None, tm, d), lambda bi, i: (bi, i, 0)),
            tspec,
            pl.BlockSpec((None, None, nblk, d), lambda bi, i: (bi, i, 0, 0)),
        ],
        out_shape=[
            jax.ShapeDtypeStruct((b, d, s), BF16),
            jax.ShapeDtypeStruct((b, s, d), BF16),
            jax.ShapeDtypeStruct((b, d, s), BF16),
            jax.ShapeDtypeStruct((b, s // tm, nblk, d), F32),
        ],
        compiler_params=_cparams(("parallel", "parallel")),
        name="qkv_proj",
    )(x3, g.reshape(1, d), wqt_bf16, wk_bf16, wvt_bf16)


def _moba_kernel(qt_ref, k_ref, vt_ref, km_ref, tbl_ref, o_ref, sel_scr, ss_scr, p_scr, acc_scr, *, nb):
    blk = MOBA_BLOCK
    i = pl.program_id(2)
    jp = jnp.maximum(i - 1, 0)
    qt = qt_ref[...]
    frow = lax.broadcasted_iota(jnp.int32, qt.shape, 0)
    blk_id = lax.broadcasted_iota(jnp.int32, (nb, blk), 0)
    valid = blk_id < i
    km = km_ref[...]

    qh = [jnp.where(frow // HEAD_DIM == hh, qt, jnp.zeros_like(qt)) for hh in range(2)]

    def k_block(j):
        return k_ref[pl.ds(pl.multiple_of(j * blk, blk), blk), :]

    def scores(kj, hh):
        return jnp.dot(kj, qh[hh], preferred_element_type=F32)

    k_own, k_prev = k_block(i), k_block(jp)
    raw_own = [scores(k_own, hh) for hh in range(2)]
    raw_prev = [scores(k_prev, hh) for hh in range(2)]

    km_hi = km.astype(BF16)
    km_lo = (km - km_hi.astype(F32)).astype(BF16)
    for hh in range(2):
        gate = (jnp.dot(km_hi, qh[hh], preferred_element_type=F32)
                + jnp.dot(km_lo, qh[hh], preferred_element_type=F32))
        gate = jnp.where(valid, gate, NEG)
        g = gate
        for _ in range(MOBA_TOPK - 1):
            g = jnp.where(g >= jnp.max(g, axis=0, keepdims=True), NEG, g)
        tau = jnp.max(g, axis=0, keepdims=True)
        sel = jnp.where(valid, jnp.where(gate >= tau, 0.0, NEG), NEG)
        sel_scr[hh, 0:nb, :] = sel
        sel_scr[hh, nb:nb + SUBLANES, :] = jnp.full((SUBLANES, blk), NEG, F32)

    nqs = blk // MOBA_QLANES
    chains = [(hh, qs) for hh in range(2) for qs in range(nqs)]

    def qcols(qs):
        return slice(qs * MOBA_QLANES, (qs + 1) * MOBA_QLANES)

    def lanes_of_head(vals, hh):
        return jnp.concatenate(vals[hh * nqs:(hh + 1) * nqs], axis=1)

    def sel_row(hh, j):
        return sel_scr[hh, pl.ds(j, 1), :]

    ones_rows = jnp.ones((2 * SUBLANES, blk), BF16)

    def pv(hh, p, j):
        vj = vt_ref[hh * HEAD_DIM:(hh + 1) * HEAD_DIM, pl.ds(pl.multiple_of(j * blk, blk), blk)]
        return jnp.dot(jnp.concatenate([vj, ones_rows], axis=0), p, preferred_element_type=F32)

    def softmax_step(m_prev, s):
        m_new = jnp.maximum(m_prev, jnp.max(s, axis=0, keepdims=True))
        alpha = jnp.exp2(m_prev - m_new)
        return m_new, alpha, jnp.exp2((s - m_new).astype(BF16))

    def far_scores(b, buf):
        kj = k_block(jnp.minimum(b, i))
        row = jnp.where(b < jp, b, nb)
        for hh in range(2):
            ss_scr[buf, hh] = scores(kj, hh) + sel_row(hh, row)

    far_scores(0, 0)
    ss = [raw_own[hh] + tbl_ref[hh, 0] for hh in range(2)]
    ss_prev = [raw_prev[hh] + tbl_ref[hh, 1] + sel_row(hh, jp) for hh in range(2)]
    ms = [jnp.max(ss[hh][:, qcols(qs)], axis=0, keepdims=True) for hh, qs in chains]
    ps = [jnp.exp2((ss[hh][:, qcols(qs)] - m).astype(BF16)) for (hh, qs), m in zip(chains, ms)]
    accs = [pv(hh, lanes_of_head(ps, hh), i) for hh in range(2)]
    steps = [softmax_step(m, ss_prev[hh][:, qcols(qs)]) for (hh, qs), m in zip(chains, ms)]
    accs = [lanes_of_head([st[1] for st in steps], hh) * accs[hh]
            + pv(hh, lanes_of_head([st[2] for st in steps], hh), jp) for hh in range(2)]
    ms = [st[0] for st in steps]

    def v_block(b):
        return jnp.clip(b, 0, i)

    def half(b, cur, ms):
        nxt = 1 - cur
        far_scores(b + 1, nxt)
        pvs = [pv(hh, p_scr[nxt, hh], v_block(b - 1)) for hh in range(2)]
        new_ms = []
        for c, (hh, qs) in enumerate(chains):
            m_new, alpha, p = softmax_step(ms[c], ss_scr[cur, hh, :, qcols(qs)])
            p_scr[cur, hh, :, qcols(qs)] = p
            acc_scr[hh, :, qcols(qs)] = alpha * (acc_scr[hh, :, qcols(qs)] + pvs[hh][:, qcols(qs)])
            new_ms.append(m_new)
        return tuple(new_ms)

    def two_blocks(t, ms):
        return half(2 * t + 1, 1, half(2 * t, 0, ms))

    for hh in range(2):
        acc_scr[hh] = accs[hh]
        p_scr[1, hh] = jnp.zeros((blk, blk), BF16)
    n_pairs = (jp + 1) // 2
    lax.fori_loop(0, n_pairs, two_blocks, tuple(ms))
    accs = [acc_scr[hh] + pv(hh, p_scr[1, hh], v_block(2 * n_pairs - 1)) for hh in range(2)]
    o_t = jnp.concatenate([acc[:HEAD_DIM] / acc[HEAD_DIM:HEAD_DIM + 1] for acc in accs], axis=0)
    o_ref[...] = o_t.T.astype(o_ref.dtype)


def _t5_bucket_np(dist):
    max_exact = REL_BUCKETS // 2
    d = np.maximum(dist, 0)
    df = np.maximum(d, 1).astype(np.float32)
    large = max_exact + (np.log(df / max_exact) / math.log(REL_MAX_DIST / max_exact)
                         * (REL_BUCKETS - max_exact)).astype(np.int32)
    large = np.minimum(large, REL_BUCKETS - 1)
    return np.where(d < max_exact, d, large)


def _moba_bias_tables(rel_bias):
    blk = MOBA_BLOCK
    assert int(_t5_bucket_np(np.array([blk + 1]))[0]) == REL_BUCKETS - 1
    dist = np.arange(-(blk - 1), blk)

    def by_distance(d):
        onehot = (_t5_bucket_np(d)[None, :] == np.arange(REL_BUCKETS)[:, None]).astype(np.float32)
        return jnp.dot(rel_bias.astype(F32), onehot, precision=lax.Precision.HIGHEST)

    def toeplitz(x):
        h = x.shape[0]
        y = jnp.concatenate([x, jnp.zeros((h, 1), x.dtype)], axis=1)
        skew = jnp.tile(y, (1, blk))[:, :blk * (2 * blk - 1)].reshape(h, blk, 2 * blk - 1)
        return skew[:, :, blk - 1:]

    far = rel_bias[:, REL_BUCKETS - 1].astype(F32)[:, None, None]
    causal = (np.arange(blk)[None, :] >= np.arange(blk)[:, None])[None]
    log2e = math.log2(math.e)
    own = jnp.where(causal, (toeplitz(by_distance(dist)) - far) * log2e, NEG)
    prev = (toeplitz(by_distance(dist + blk)) - far) * log2e
    return jnp.stack([own, prev], axis=1)


def _moba_attention(qt, k, vt, kmean, tbl):
    b, s, d = k.shape
    blk = MOBA_BLOCK
    nb = s // blk
    nhp = d // LANES
    return pl.pallas_call(
        functools.partial(_moba_kernel, nb=nb),
        grid=(b, nhp, nb),
        in_specs=[
            pl.BlockSpec((None, LANES, blk), lambda bi, hp, i: (bi, hp, i)),
            pl.BlockSpec((None, s, LANES), lambda bi, hp, i: (bi, 0, hp)),
            pl.BlockSpec((None, LANES, s), lambda bi, hp, i: (bi, hp, 0)),
            pl.BlockSpec((None, nb, LANES), lambda bi, hp, i: (bi, 0, hp)),
            pl.BlockSpec((2, 2, blk, blk), lambda bi, hp, i: (hp, 0, 0, 0)),
        ],
        out_specs=pl.BlockSpec((None, blk, LANES), lambda bi, hp, i: (bi, i, hp)),
        out_shape=jax.ShapeDtypeStruct((b, s, d), BF16),
        scratch_shapes=[
            pltpu.VMEM((2, nb + SUBLANES, blk), F32),
            pltpu.VMEM((2, 2, blk, blk), F32),
            pltpu.VMEM((2, 2, blk, blk), BF16),
            pltpu.VMEM((2, HEAD_DIM + 2 * SUBLANES, blk), F32),
        ],
        compiler_params=_cparams(("parallel", "parallel", "arbitrary")),
        name="moba_attention",
    )(qt, k, vt, kmean.reshape(b, nb, d), tbl)


def _peerq_epilogue(x_new, gf_ref, wpq_ref, x_out_ref, h_ref, pq_ref):
    x_out_ref[...] = x_new
    h = _rms(x_new, gf_ref[...]).astype(BF16)
    h_ref[...] = h
    pq_ref[...] = jnp.dot(h, wpq_ref[...], preferred_element_type=F32)


def _attn_out_kernel(o_ref, wo_ref, x_ref, gf_ref, wpq_ref, x_out_ref, h_ref, pq_ref):
    x_new = x_ref[...] + jnp.dot(o_ref[...], wo_ref[...], preferred_element_type=F32)
    _peerq_epilogue(x_new, gf_ref, wpq_ref, x_out_ref, h_ref, pq_ref)


def _attn_out_proj(o2d, wo_bf16, x2d, gf, wpq_bf16):
    t, d = x2d.shape
    nq = wpq_bf16.shape[1]
    tm = TM_PROJ
    return pl.pallas_call(
        _attn_out_kernel,
        grid=(t // tm,),
        in_specs=[
            pl.BlockSpec((tm, d), lambda i: (i, 0)),
            pl.BlockSpec((d, d), lambda i: (0, 0)),
            pl.BlockSpec((tm, d), lambda i: (i, 0)),
            pl.BlockSpec((1, d), lambda i: (0, 0)),
            pl.BlockSpec((d, nq), lambda i: (0, 0)),
        ],
        out_specs=[
            pl.BlockSpec((tm, d), lambda i: (i, 0)),
            pl.BlockSpec((tm, d), lambda i: (i, 0)),
            pl.BlockSpec((tm, nq), lambda i: (i, 0)),
        ],
        out_shape=[
            jax.ShapeDtypeStruct((t, d), F32),
            jax.ShapeDtypeStruct((t, d), BF16),
            jax.ShapeDtypeStruct((t, nq), F32),
        ],
        compiler_params=_cparams(("parallel",)),
        name="attn_out_proj",
    )(o2d, wo_bf16, x2d, gf.reshape(1, d), wpq_bf16)


def _cmpx(vals, a, b):
    hi = jnp.maximum(vals[a], vals[b])
    lo = jnp.minimum(vals[a], vals[b])
    vals[a], vals[b] = hi, lo


def _bitonic_merge_desc(vals):
    n = len(vals)
    dist = n // 2
    while dist >= 1:
        for s in range(0, n, 2 * dist):
            for t in range(s, s + dist):
                _cmpx(vals, t, t + dist)
        dist //= 2
    return vals


def _sort_desc(vals):
    n = len(vals)
    if n == 1:
        return vals
    top = _sort_desc(vals[: n // 2])
    bot = _sort_desc(vals[n // 2:])
    return _bitonic_merge_desc(top + bot[::-1])


def _top_merge(a, b):
    n = len(a)
    return _bitonic_merge_desc([jnp.maximum(a[r], b[n - 1 - r]) for r in range(n)])


def _sorted_top16(vals):
    groups = [_sort_desc(vals[g:g + PEER_TOPK]) for g in range(0, len(vals), PEER_TOPK)]
    while len(groups) > 1:
        groups = [_top_merge(groups[g], groups[g + 1]) for g in range(0, len(groups), 2)]
    return groups[0]


def _dot_nt_3pass(a, b):
    a_hi, b_hi = a.astype(BF16), b.astype(BF16)
    a_lo = (a - a_hi.astype(F32)).astype(BF16)
    b_lo = (b - b_hi.astype(F32)).astype(BF16)
    dot = functools.partial(lax.dot_general, dimension_numbers=NT_DIMS, preferred_element_type=F32)
    return dot(a_hi, b_hi) + (dot(a_hi, b_lo) + dot(a_lo, b_hi))


def _router_kernel(pq_ref, keys_ref, st_ref, sc_ref, scr_ref):
    nch = TT_ROUTER // LANES
    tops = []
    for c in range(2):
        qc = pq_ref[:, c * PEER_HALF:(c + 1) * PEER_HALF]
        s_t = _dot_nt_3pass(keys_ref[0, c], qc)
        st_ref[0, c] = s_t
        for ch in range(nch):
            scr_ref[c, pl.ds(ch, PEER_NKEYS, stride=nch), :] = s_t[:, ch * LANES:(ch + 1) * LANES]
        vals = [scr_ref[c, k * nch:(k + 1) * nch, :] for k in range(PEER_NKEYS)]
        tops.append(_sorted_top16(vals))
    a, b = tops
    cands = [a[r] + b[c] for r in range(PEER_TOPK) for c in range(PEER_TOPK)
             if (r + 1) * (c + 1) <= PEER_TOPK]
    npad = 1 << (len(cands) - 1).bit_length()
    cands = cands + [jnp.full_like(cands[0], NEG)] * (npad - len(cands))
    best = _sort_desc(cands)[:PEER_TOPK]
    z = jnp.ones_like(best[0])
    for r in range(1, PEER_TOPK):
        z = z + jnp.exp(best[r] - best[0])
    for r in range(PEER_TOPK):
        sc_ref[0, SC_TOP1 + r] = a[r]
        sc_ref[0, SC_TOP2 + r] = b[r]
    sc_ref[0, SC_TAU] = best[PEER_TOPK - 1]
    sc_ref[0, SC_ZINV] = 1.0 / z


def _peer_router(pq, sub_keys):
    t = pq.shape[0]
    tt = TT_ROUTER
    nch = tt // LANES
    hp = PEER_HEADS
    return pl.pallas_call(
        _router_kernel,
        grid=(t // tt, hp),
        in_specs=[
            pl.BlockSpec((tt, 2 * PEER_HALF), lambda i, h: (i, h)),
            pl.BlockSpec((1, 2, PEER_NKEYS, PEER_HALF), lambda i, h: (h, 0, 0, 0)),
        ],
        out_specs=[
            pl.BlockSpec((1, 2, PEER_NKEYS, tt), lambda i, h: (h, 0, 0, i)),
            pl.BlockSpec((1, SC_ROWS, nch, LANES), lambda i, h: (h, 0, i, 0)),
        ],
        out_shape=[
            jax.ShapeDtypeStruct((hp, 2, PEER_NKEYS, t), F32),
            jax.ShapeDtypeStruct((hp, SC_ROWS, t // LANES, LANES), F32),
        ],
        scratch_shapes=[pltpu.VMEM((2, PEER_NKEYS * nch, LANES), F32)],
        compiler_params=_cparams(("parallel", "parallel")),
        name="peer_router",
    )(pq, sub_keys)


def _gelu2(x):
    c0 = math.sqrt(2.0 / math.pi)
    z = x * (c0 + (c0 * 0.044715) * (x * x))
    return x + x * jnp.tanh(z)


def _dup_bf16_bits(v):
    bits = pltpu.bitcast(v.astype(BF16).astype(F32), jnp.uint32)
    return bits | (bits >> 16)


def _peer_kernel(h_ref, st_ref, sc_ref, u_ref, vt_ref, x_ref, gfin_ref, o_ref,
                 rank_scr, bw_scr, cnt_scr, aw_scr, act_scr, wa_scr, acc_scr, *, final_norm):
    e = pl.program_id(1)
    tt = h_ref.shape[0]
    nsub = CE_STEP // CE_SUB
    ipc = CE_SUB // PEER_NKEYS
    pack = 2 * SUBLANES

    @pl.when(e == 0)
    def _():
        acc_scr[...] = jnp.zeros_like(acc_scr)
        for hd in range(PEER_HEADS):
            s1 = st_ref[hd, 0]
            s2 = st_ref[hd, 1]
            tau = sc_ref[hd, SC_TAU:SC_TAU + 1, :]
            rank2 = jnp.zeros_like(s2)
            cnt = jnp.zeros_like(s1)
            for c in range(PEER_TOPK):
                b_c = sc_ref[hd, SC_TOP2 + c:SC_TOP2 + c + 1, :]
                rank2 = jnp.where(b_c > s2, c + 1.0, rank2)
                cnt = jnp.where(s1 + b_c >= tau, c + 1.0, cnt)
            a0 = sc_ref[hd, SC_TOP1:SC_TOP1 + 1, :]
            b0 = sc_ref[hd, SC_TOP2:SC_TOP2 + 1, :]
            zinv = sc_ref[hd, SC_ZINV:SC_ZINV + 1, :]
            rank_scr[hd] = rank2.astype(BF16)
            bw_scr[hd] = (jnp.exp(s2 - b0) * (0.5 * zinv)).astype(BF16)
            cnt_w = _dup_bf16_bits(cnt)
            a_w = _dup_bf16_bits(jnp.exp(s1 - a0))
            for lg in range(tt // LANES):
                cnt_scr[hd, lg] = cnt_w[:, lg * LANES:(lg + 1) * LANES]
                aw_scr[hd, lg] = a_w[:, lg * LANES:(lg + 1) * LANES]

    def row_bf16(tile, r):
        words = jnp.broadcast_to(tile[r:r + 1, :], (SUBLANES, LANES))
        return pltpu.bitcast(words, BF16)

    def up_proj(c):
        act_scr[c % 2] = lax.dot_general(u_ref[c * CE_SUB:(c + 1) * CE_SUB, :], h_ref[...], NT_DIMS,
                                         preferred_element_type=F32)

    def down_proj(c):
        acc_scr[...] += jnp.dot(vt_ref[:, c * CE_SUB:(c + 1) * CE_SUB], wa_scr[c % 2],
                                preferred_element_type=F32)

    def gate(c):
        njg = PEER_NKEYS // pack
        zero = jnp.zeros((pack, LANES), BF16)
        for lg in range(tt // LANES):
            lanes = slice(lg * LANES, (lg + 1) * LANES)
            w = [[None] * njg for _ in range(ipc)]
            for hd in range(PEER_HEADS):
                rank = [rank_scr[hd, jg * pack:(jg + 1) * pack, lanes] for jg in range(njg)]
                bw = [bw_scr[hd, jg * pack:(jg + 1) * pack, lanes] for jg in range(njg)]
                first = (e * nsub + c) * ipc
                base = pl.multiple_of((first // SUBLANES) * SUBLANES, SUBLANES)
                off = (c * ipc) % SUBLANES
                cnt_tile = cnt_scr[hd, lg, pl.ds(base, SUBLANES), :]
                a_tile = aw_scr[hd, lg, pl.ds(base, SUBLANES), :]
                for il in range(ipc):
                    cnt_row = row_bf16(cnt_tile, off + il)
                    a_row = row_bf16(a_tile, off + il)
                    for jg in range(njg):
                        term = jnp.where(rank[jg] < cnt_row, a_row * bw[jg], zero)
                        w[il][jg] = term if w[il][jg] is None else w[il][jg] + term
            for il in range(ipc):
                for jg in range(njg):
                    rows = slice(il * PEER_NKEYS + jg * pack, il * PEER_NKEYS + (jg + 1) * pack)
                    wa_scr[c % 2, rows, lanes] = w[il][jg] * _gelu2(act_scr[c % 2, rows, lanes].astype(BF16))

    up_proj(0)
    for c in range(nsub):
        if c + 1 < nsub:
            up_proj(c + 1)
        if c >= 1:
            down_proj(c - 1)
        gate(c)
    down_proj(nsub - 1)

    @pl.when(e == pl.num_programs(1) - 1)
    def _():
        y = x_ref[...] + acc_scr[...].T
        if final_norm:
            y = _rms(y, gfin_ref[...])
        o_ref[...] = y


def _peer_experts(h_bf16, st, sc, u_bf16, vt_bf16, x2d, gfin, final_norm):
    t, d = x2d.shape
    ne = u_bf16.shape[0]
    tt, ce = TT_PEER, CE_STEP
    hp = PEER_HEADS
    assert (ce // PEER_NKEYS) % SUBLANES == 0 and SUBLANES % (CE_SUB // PEER_NKEYS) == 0
    return pl.pallas_call(
        functools.partial(_peer_kernel, final_norm=final_norm),
        grid=(t // tt, ne // ce),
        in_specs=[
            pl.BlockSpec((tt, d), lambda i, e: (i, 0)),
            pl.BlockSpec((hp, 2, PEER_NKEYS, tt), lambda i, e: (0, 0, 0, i)),
            pl.BlockSpec((hp, SC_ROWS, tt), lambda i, e: (0, 0, i)),
            pl.BlockSpec((ce, d), lambda i, e: (e, 0)),
            pl.BlockSpec((None, d, ce), lambda i, e: (e, 0, 0)),
            pl.BlockSpec((tt, d), lambda i, e: (i, 0)),
            pl.BlockSpec((1, d), lambda i, e: (0, 0)),
        ],
        out_specs=pl.BlockSpec((tt, d), lambda i, e: (i, 0)),
        out_shape=jax.ShapeDtypeStruct((t, d), F32),
        scratch_shapes=[
            pltpu.VMEM((hp, PEER_NKEYS, tt), BF16),
            pltpu.VMEM((hp, PEER_NKEYS, tt), BF16),
            pltpu.VMEM((hp, tt // LANES, PEER_NKEYS, LANES), jnp.uint32),
            pltpu.VMEM((hp, tt // LANES, PEER_NKEYS, LANES), jnp.uint32),
            pltpu.VMEM((2, CE_SUB, tt), F32),
            pltpu.VMEM((2, CE_SUB, tt), BF16),
            pltpu.VMEM((d, tt), F32),
        ],
        compiler_params=_cparams(("parallel", "arbitrary")),
        name="peer_experts",
    )(h_bf16, st, sc.reshape(hp, SC_ROWS, t), u_bf16, vt_bf16, x2d, gfin.reshape(1, d))


def _expert_major_t(v):
    ne, d = v.shape
    return v.astype(BF16).reshape(ne // CE_STEP, CE_STEP, d).transpose(0, 2, 1)


def _peer(h_bf16, pq, sub_keys, u_bf16, vt_bf16, x2d, gfin, final_norm):
    st, sc = _peer_router(pq, sub_keys)
    return _peer_experts(h_bf16, st, sc, u_bf16, vt_bf16, x2d, gfin, final_norm)


def _glu_kernel(x_ref, g_ref, w_ref, b_ref, u_ref, *, d):
    h = _rms(x_ref[...], g_ref[...]).astype(BF16)
    a = jnp.dot(h, w_ref[...], preferred_element_type=F32) + b_ref[...]
    u_ref[...] = a[:, :d] * jax.nn.sigmoid(a[:, d:])


def _glu_proj(x2d, g, w_bf16, bias):
    t, d = x2d.shape
    tm = TM_PROJ
    return pl.pallas_call(
        functools.partial(_glu_kernel, d=d),
        grid=(t // tm,),
        in_specs=[
            pl.BlockSpec((tm, d), lambda i: (i, 0)),
            pl.BlockSpec((1, d), lambda i: (0, 0)),
            pl.BlockSpec((d, 2 * d), lambda i: (0, 0)),
            pl.BlockSpec((1, 2 * d), lambda i: (0, 0)),
        ],
        out_specs=pl.BlockSpec((tm, d), lambda i: (i, 0)),
        out_shape=jax.ShapeDtypeStruct((t, d), F32),
        compiler_params=_cparams(("parallel",)),
        name="glu_proj",
    )(x2d, g.reshape(1, d), w_bf16, bias.reshape(1, 2 * d))


def _conv_kernel(ucur_ref, uprev_ref, wdw_ref, bdw_ref, lng_ref, lnb_ref, w2_ref, b2_ref,
                 x_ref, gf_ref, wpq_ref, x_out_ref, h_ref, pq_ref, ext_scr, y_scr):
    tm, d = ucur_ref.shape
    first = pl.program_id(1) == 0
    ext_scr[0:CONV_HALO, :] = jnp.where(first, 0.0, uprev_ref[...])
    ext_scr[CONV_HALO:, :] = ucur_ref[...]
    lead = CONV_HALO - (CONV_WIDTH - 1)

    def chunk(ci, carry):
        r0 = pl.multiple_of(ci * CONV_ROWS, CONV_ROWS)
        rows = CONV_ROWS + CONV_HALO
        window = ext_scr[pl.ds(r0, rows), :]
        shifted = [window] + [pltpu.roll(window, rows - r, 0) for r in range(1, SUBLANES)]
        acc = jnp.broadcast_to(bdw_ref[...], (CONV_ROWS, d))
        for w in range(CONV_WIDTH):
            start = (lead + w) // SUBLANES * SUBLANES
            tap = shifted[(lead + w) % SUBLANES][start:start + CONV_ROWS, :]
            acc = acc + wdw_ref[w:w + 1, :] * tap
        y_scr[pl.ds(r0, CONV_ROWS), :] = acc
        return carry

    lax.fori_loop(0, tm // CONV_ROWS, chunk, 0)
    y = y_scr[...]
    mu = jnp.mean(y, axis=-1, keepdims=True)
    var = jnp.mean(jnp.square(y - mu), axis=-1, keepdims=True)
    y = (y - mu) * lax.rsqrt(var + EPS) * lng_ref[...] + lnb_ref[...]
    y = y * jax.nn.sigmoid(y)
    x_new = x_ref[...] + jnp.dot(y.astype(BF16), w2_ref[...], preferred_element_type=F32) + b2_ref[...]
    _peerq_epilogue(x_new, gf_ref, wpq_ref, x_out_ref, h_ref, pq_ref)


def _conv_tail(u3, wdw, bdw, lng, lnb, w2_bf16, b2, x3, gf, wpq_bf16):
    b, s, d = x3.shape
    nq = wpq_bf16.shape[1]
    tm = TM_CONV
    hpb = tm // CONV_HALO
    row = lambda v: v.reshape(1, d)
    tok = pl.BlockSpec((None, tm, d), lambda bi, i: (bi, i, 0))
    vec = pl.BlockSpec((1, d), lambda bi, i: (0, 0))
    return pl.pallas_call(
        _conv_kernel,
        grid=(b, s // tm),
        in_specs=[
            tok,
            pl.BlockSpec((None, CONV_HALO, d), lambda bi, i: (bi, jnp.maximum(i * hpb - 1, 0), 0)),
            pl.BlockSpec((CONV_WIDTH, d), lambda bi, i: (0, 0)),
            vec, vec, vec,
            pl.BlockSpec((d, d), lambda bi, i: (0, 0)),
            vec,
            tok,
            vec,
            pl.BlockSpec((d, nq), lambda bi, i: (0, 0)),
        ],
        out_specs=[
            tok,
            tok,
            pl.BlockSpec((None, tm, nq), lambda bi, i: (bi, i, 0)),
        ],
        out_shape=[
            jax.ShapeDtypeStruct((b, s, d), F32),
            jax.ShapeDtypeStruct((b, s, d), BF16),
            jax.ShapeDtypeStruct((b, s, nq), F32),
        ],
        scratch_shapes=[
            pltpu.VMEM((tm + CONV_HALO, d), F32),
            pltpu.VMEM((tm, d), F32),
        ],
        compiler_params=_cparams(("parallel", "arbitrary")),
        name="conv_tail",
    )(u3, u3, wdw, row(bdw), row(lng), row(lnb), w2_bf16, row(b2), x3, row(gf), wpq_bf16)


def kernel(x, rel_bias, norm_mix, norm_ffn, attn_w_qkv, attn_w_o, conv_w_pw1, conv_b_pw1,
           conv_w_dw, conv_b_dw, conv_ln_g, conv_ln_b, conv_w_pw2, conv_b_pw2, peer_w_q,
           peer_sub_keys, peer_u, peer_v, norm_final):
    b, s, d = x.shape
    t = b * s
    depth = norm_mix.shape[0]
    assert d == N_HEADS * HEAD_DIM and s % MOBA_BLOCK == 0 and s // MOBA_BLOCK <= LANES
    assert t % TT_ROUTER == 0 and s % TM_CONV == 0 and t % TM_PROJ == 0
    x2d = x.reshape(t, d)
    for i in range(depth):
        j = i // 2
        wpq = peer_w_q[i].astype(BF16)
        if i % 2 == 0:
            wq, wk, wv = (attn_w_qkv[j][:, c * d:(c + 1) * d] for c in range(3))
            qt, k, vt, kmean = _qkv_proj(x2d.reshape(b, s, d), norm_mix[i], wq.T.astype(BF16),
                                         wk.astype(BF16), wv.T.astype(BF16))
            o = _moba_attention(qt, k, vt, kmean, _moba_bias_tables(rel_bias))
            x2d, h, pq = _attn_out_proj(o.reshape(t, d), attn_w_o[j].astype(BF16), x2d,
                                        norm_ffn[i], wpq)
        else:
            u = _glu_proj(x2d, norm_mix[i], conv_w_pw1[j].astype(BF16), conv_b_pw1[j])
            x3, h3, pq3 = _conv_tail(u.reshape(b, s, d), conv_w_dw[j], conv_b_dw[j], conv_ln_g[j],
                                     conv_ln_b[j], conv_w_pw2[j].astype(BF16), conv_b_pw2[j],
                                     x2d.reshape(b, s, d), norm_ffn[i], wpq)
            x2d, h, pq = x3.reshape(t, d), h3.reshape(t, d), pq3.reshape(t, -1)
        last = i == depth - 1
        x2d = _peer(h, pq, peer_sub_keys[i], peer_u[i].astype(BF16),
                    _expert_major_t(peer_v[i]), x2d, norm_final, final_norm=last)
    return x2d.reshape(b, s, d)
```

```python
import functools
import math

import numpy as np
import jax
import jax.numpy as jnp
from jax import lax
from jax.experimental import pallas as pl
from jax.experimental.pallas import tpu as pltpu

F32 = jnp.float32
BF16 = jnp.bfloat16
EPS = 1e-6
NEG = -1e30

N_HEADS = 16
HEAD_DIM = 64
MOBA_BLOCK = 256
MOBA_TOPK = 3
REL_BUCKETS = 32
REL_MAX_DIST = 128
CONV_WIDTH = 31
PEER_HEADS = 8
PEER_NKEYS = 128
PEER_HALF = 128
PEER_TOPK = 16
SC_TOP1, SC_TOP2, SC_TAU, SC_ZINV, SC_ROWS = 0, PEER_TOPK, 2 * PEER_TOPK, 2 * PEER_TOPK + 1, 2 * PEER_TOPK + 2

LANES = 128
SUBLANES = 8
VMEM_LIMIT = 60 * 1024 * 1024

MOBA_QLANES = 128
TM_PROJ = 512
TT_ROUTER = 1024
TT_PEER = 512
CE_STEP = 2048
CE_SUB = 512
TM_CONV = 512
CONV_HALO = 32
CONV_ROWS = 16

NT_DIMS = (((1,), (1,)), ((), ()))


def _cparams(sem):
    return pltpu.CompilerParams(dimension_semantics=sem, vmem_limit_bytes=VMEM_LIMIT)


def _rms(x, g):
    return x * lax.rsqrt(jnp.mean(x * x, axis=-1, keepdims=True) + EPS) * g


def _qkv_kernel(x_ref, g_ref, wqt_ref, wk_ref, wvt_ref, qt_ref, k_ref, vt_ref, km_ref):
    h = _rms(x_ref[...], g_ref[...]).astype(BF16)
    scale = HEAD_DIM ** -0.5 * math.log2(math.e)
    qt = lax.dot_general(wqt_ref[...], h, NT_DIMS, preferred_element_type=F32)
    qt_ref[...] = (qt * scale).astype(BF16)
    k = jnp.dot(h, wk_ref[...], preferred_element_type=F32)
    k_ref[...] = k.astype(BF16)
    vt = lax.dot_general(wvt_ref[...], h, NT_DIMS, preferred_element_type=F32)
    vt_ref[...] = vt.astype(BF16)
    nblk = k.shape[0] // MOBA_BLOCK
    km_ref[...] = jnp.mean(k.reshape(nblk, MOBA_BLOCK, k.shape[1]), axis=1)


def _qkv_proj(x3, g, wqt_bf16, wk_bf16, wvt_bf16):
    b, s, d = x3.shape
    tm = TM_PROJ
    nblk = tm // MOBA_BLOCK
    nb = s // MOBA_BLOCK
    wspec = pl.BlockSpec((d, d), lambda bi, i: (0, 0))
    tspec = pl.BlockSpec((None, d, tm), lambda bi, i: (bi, 0, i))
    return pl.pallas_call(
        _qkv_kernel,
        grid=(b, s // tm),
        in_specs=[
            pl.BlockSpec((None, tm, d), lambda bi, i: (bi, i, 0)),
            pl.BlockSpec((1, d), lambda bi, i: (0, 0)),
            wspec, wspec, wspec,
        ],
        out_specs=[
            tspec,
            pl.BlockSpec((None, tm, d), lambda bi, i: (bi, i, 0)),
            tspec,
            pl.BlockSpec((None, None, nblk, d), lambda bi, i: (bi, i, 0, 0)),
        ],
        out_shape=[
            jax.ShapeDtypeStruct((b, d, s), BF16),
            jax.ShapeDtypeStruct((b, s, d), BF16),
            jax.ShapeDtypeStruct((b, d, s), BF16),
            jax.ShapeDtypeStruct((b, s // tm, nblk, d), F32),
        ],
        compiler_params=_cparams(("parallel", "parallel")),
        name="qkv_proj",
    )(x3, g.reshape(1, d), wqt_bf16, wk_bf16, wvt_bf16)


def _moba_kernel(qt_ref, k_ref, vt_ref, km_ref, tbl_ref, o_ref, sel_scr, ss_scr, p_scr, acc_scr, *, nb):
    blk = MOBA_BLOCK
    i = pl.program_id(2)
    jp = jnp.maximum(i - 1, 0)
    qt = qt_ref[...]
    frow = lax.broadcasted_iota(jnp.int32, qt.shape, 0)
    blk_id = lax.broadcasted_iota(jnp.int32, (nb, blk), 0)
    valid = blk_id < i
    km = km_ref[...]

    qh = [jnp.where(frow // HEAD_DIM == hh, qt, jnp.zeros_like(qt)) for hh in range(2)]

    def k_block(j):
        return k_ref[pl.ds(pl.multiple_of(j * blk, blk), blk), :]

    def scores(kj, hh):
        return jnp.dot(kj, qh[hh], preferred_element_type=F32)

    k_own, k_prev = k_block(i), k_block(jp)
    raw_own = [scores(k_own, hh) for hh in range(2)]
    raw_prev = [scores(k_prev, hh) for hh in range(2)]

    km_hi = km.astype(BF16)
    km_lo = (km - km_hi.astype(F32)).astype(BF16)
    for hh in range(2):
        gate = (jnp.dot(km_hi, qh[hh], preferred_element_type=F32)
                + jnp.dot(km_lo, qh[hh], preferred_element_type=F32))
        gate = jnp.where(valid, gate, NEG)
        g = gate
        for _ in range(MOBA_TOPK - 1):
            g = jnp.where(g >= jnp.max(g, axis=0, keepdims=True), NEG, g)
        tau = jnp.max(g, axis=0, keepdims=True)
        sel = jnp.where(valid, jnp.where(gate >= tau, 0.0, NEG), NEG)
        sel_scr[hh, 0:nb, :] = sel
        sel_scr[hh, nb:nb + SUBLANES, :] = jnp.full((SUBLANES, blk), NEG, F32)

    nqs = blk // MOBA_QLANES
    chains = [(hh, qs) for hh in range(2) for qs in range(nqs)]

    def qcols(qs):
        return slice(qs * MOBA_QLANES, (qs + 1) * MOBA_QLANES)

    def lanes_of_head(vals, hh):
        return jnp.concatenate(vals[hh * nqs:(hh + 1) * nqs], axis=1)

    def sel_row(hh, j):
        return sel_scr[hh, pl.ds(j, 1), :]

    ones_rows = jnp.ones((2 * SUBLANES, blk), BF16)

    def pv(hh, p, j):
        vj = vt_ref[hh * HEAD_DIM:(hh + 1) * HEAD_DIM, pl.ds(pl.multiple_of(j * blk, blk), blk)]
        return jnp.dot(jnp.concatenate([vj, ones_rows], axis=0), p, preferred_element_type=F32)

    def softmax_step(m_prev, s):
        m_new = jnp.maximum(m_prev, jnp.max(s, axis=0, keepdims=True))
        alpha = jnp.exp2(m_prev - m_new)
        return m_new, alpha, jnp.exp2(s - m_new).astype(BF16)

    def far_scores(b, buf):
        kj = k_block(jnp.minimum(b, i))
        row = jnp.where(b < jp, b, nb)
        for hh in range(2):
            ss_scr[buf, hh] = scores(kj, hh) + sel_row(hh, row)

    far_scores(0, 0)
    ss = [raw_own[hh] + tbl_ref[hh, 0] for hh in range(2)]
    ss_prev = [raw_prev[hh] + tbl_ref[hh, 1] + sel_row(hh, jp) for hh in range(2)]
    ms = [jnp.max(ss[hh][:, qcols(qs)], axis=0, keepdims=True) for hh, qs in chains]
    ps = [jnp.exp2(ss[hh][:, qcols(qs)] - m).astype(BF16) for (hh, qs), m in zip(chains, ms)]
    accs = [pv(hh, lanes_of_head(ps, hh), i) for hh in range(2)]
    steps = [softmax_step(m, ss_prev[hh][:, qcols(qs)]) for (hh, qs), m in zip(chains, ms)]
    accs = [lanes_of_head([st[1] for st in steps], hh) * accs[hh]
            + pv(hh, lanes_of_head([st[2] for st in steps], hh), jp) for hh in range(2)]
    ms = [st[0] for st in steps]

    def v_block(b):
        return jnp.clip(b, 0, i)

    def half(b, cur, ms):
        nxt = 1 - cur
        far_scores(b + 1, nxt)
        pvs = [pv(hh, p_scr[nxt, hh], v_block(b - 1)) for hh in range(2)]
        new_ms = []
        for c, (hh, qs) in enumerate(chains):
            m_new, alpha, p = softmax_step(ms[c], ss_scr[cur, hh, :, qcols(qs)])
            p_scr[cur, hh, :, qcols(qs)] = p
            acc_scr[hh, :, qcols(qs)] = alpha * (acc_scr[hh, :, qcols(qs)] + pvs[hh][:, qcols(qs)])
            new_ms.append(m_new)
        return tuple(new_ms)

    def two_blocks(t, ms):
        return half(2 * t + 1, 1, half(2 * t, 0, ms))

    for hh in range(2):
        acc_scr[hh] = accs[hh]
        p_scr[1, hh] = jnp.zeros((blk, blk), BF16)
    n_pairs = (jp + 1) // 2
    lax.fori_loop(0, n_pairs, two_blocks, tuple(ms))
    accs = [acc_scr[hh] + pv(hh, p_scr[1, hh], v_block(2 * n_pairs - 1)) for hh in range(2)]
    o_t = jnp.concatenate([acc[:HEAD_DIM] / acc[HEAD_DIM:HEAD_DIM + 1] for acc in accs], axis=0)
    o_ref[...] = o_t.astype(o_ref.dtype)


def _t5_bucket_np(dist):
    max_exact = REL_BUCKETS // 2
    d = np.maximum(dist, 0)
    df = np.maximum(d, 1).astype(np.float32)
    large = max_exact + (np.log(df / max_exact) / math.log(REL_MAX_DIST / max_exact)
                         * (REL_BUCKETS - max_exact)).astype(np.int32)
    large = np.minimum(large, REL_BUCKETS - 1)
    return np.where(d < max_exact, d, large)


def _moba_bias_tables(rel_bias):
    blk = MOBA_BLOCK
    assert int(_t5_bucket_np(np.array([blk + 1]))[0]) == REL_BUCKETS - 1
    dist = np.arange(-(blk - 1), blk)

    def by_distance(d):
        onehot = (_t5_bucket_np(d)[None, :] == np.arange(REL_BUCKETS)[:, None]).astype(np.float32)
        return jnp.dot(rel_bias.astype(F32), onehot, precision=lax.Precision.HIGHEST)

    def toeplitz(x):
        h = x.shape[0]
        y = jnp.concatenate([x, jnp.zeros((h, 1), x.dtype)], axis=1)
        skew = jnp.tile(y, (1, blk))[:, :blk * (2 * blk - 1)].reshape(h, blk, 2 * blk - 1)
        return skew[:, :, blk - 1:]

    far = rel_bias[:, REL_BUCKETS - 1].astype(F32)[:, None, None]
    causal = (np.arange(blk)[None, :] >= np.arange(blk)[:, None])[None]
    log2e = math.log2(math.e)
    own = jnp.where(causal, (toeplitz(by_distance(dist)) - far) * log2e, NEG)
    prev = (toeplitz(by_distance(dist + blk)) - far) * log2e
    return jnp.stack([own, prev], axis=1)


def _moba_attention(qt, k, vt, kmean, tbl):
    b, s, d = k.shape
    blk = MOBA_BLOCK
    nb = s // blk
    nhp = d // LANES
    return pl.pallas_call(
        functools.partial(_moba_kernel, nb=nb),
        grid=(b, nhp, nb),
        in_specs=[
            pl.BlockSpec((None, LANES, blk), lambda bi, hp, i: (bi, hp, i)),
            pl.BlockSpec((None, s, LANES), lambda bi, hp, i: (bi, 0, hp)),
            pl.BlockSpec((None, LANES, s), lambda bi, hp, i: (bi, hp, 0)),
            pl.BlockSpec((None, nb, LANES), lambda bi, hp, i: (bi, 0, hp)),
            pl.BlockSpec((2, 2, blk, blk), lambda bi, hp, i: (hp, 0, 0, 0)),
        ],
        out_specs=pl.BlockSpec((None, LANES, blk), lambda bi, hp, i: (bi, hp, i)),
        out_shape=jax.ShapeDtypeStruct((b, d, s), BF16),
        scratch_shapes=[
            pltpu.VMEM((2, nb + SUBLANES, blk), F32),
            pltpu.VMEM((2, 2, blk, blk), F32),
            pltpu.VMEM((2, 2, blk, blk), BF16),
            pltpu.VMEM((2, HEAD_DIM + 2 * SUBLANES, blk), F32),
        ],
        compiler_params=_cparams(("parallel", "parallel", "arbitrary")),
        name="moba_attention",
    )(qt, k, vt, kmean.reshape(b, nb, d), tbl)


def _peerq_epilogue(x_new, gf_ref, wpq_ref, x_out_ref, h_ref, pq_ref):
    x_out_ref[...] = x_new
    h = _rms(x_new, gf_ref[...]).astype(BF16)
    h_ref[...] = h
    pq_ref[...] = jnp.dot(h, wpq_ref[...], preferred_element_type=F32)


TN_DIMS = (((0,), (0,)), ((), ()))


def _attn_out_kernel(ot_ref, wo_ref, x_ref, gf_ref, wpq_ref, x_out_ref, h_ref, pq_ref):
    x_new = x_ref[...] + lax.dot_general(ot_ref[...], wo_ref[...], TN_DIMS, preferred_element_type=F32)
    _peerq_epilogue(x_new, gf_ref, wpq_ref, x_out_ref, h_ref, pq_ref)


def _attn_out_proj(o_t, wo_bf16, x3, gf, wpq_bf16):
    b, s, d = x3.shape
    nq = wpq_bf16.shape[1]
    tm = TM_PROJ
    tok = lambda width: pl.BlockSpec((None, tm, width), lambda bi, i: (bi, i, 0))
    return pl.pallas_call(
        _attn_out_kernel,
        grid=(b, s // tm),
        in_specs=[
            pl.BlockSpec((None, d, tm), lambda bi, i: (bi, 0, i)),
            pl.BlockSpec((d, d), lambda bi, i: (0, 0)),
            tok(d),
            pl.BlockSpec((1, d), lambda bi, i: (0, 0)),
            pl.BlockSpec((d, nq), lambda bi, i: (0, 0)),
        ],
        out_specs=[tok(d), tok(d), tok(nq)],
        out_shape=[
            jax.ShapeDtypeStruct((b, s, d), F32),
            jax.ShapeDtypeStruct((b, s, d), BF16),
            jax.ShapeDtypeStruct((b, s, nq), F32),
        ],
        compiler_params=_cparams(("parallel", "parallel")),
        name="attn_out_proj",
    )(o_t, wo_bf16, x3, gf.reshape(1, d), wpq_bf16)


def _cmpx(vals, a, b):
    hi = jnp.maximum(vals[a], vals[b])
    lo = jnp.minimum(vals[a], vals[b])
    vals[a], vals[b] = hi, lo


def _bitonic_merge_desc(vals):
    n = len(vals)
    dist = n // 2
    while dist >= 1:
        for s in range(0, n, 2 * dist):
            for t in range(s, s + dist):
                _cmpx(vals, t, t + dist)
        dist //= 2
    return vals


def _sort_desc(vals):
    n = len(vals)
    if n == 1:
        return vals
    top = _sort_desc(vals[: n // 2])
    bot = _sort_desc(vals[n // 2:])
    return _bitonic_merge_desc(top + bot[::-1])


def _top_merge(a, b):
    n = len(a)
    return _bitonic_merge_desc([jnp.maximum(a[r], b[n - 1 - r]) for r in range(n)])


def _sorted_top16(vals):
    groups = [_sort_desc(vals[g:g + PEER_TOPK]) for g in range(0, len(vals), PEER_TOPK)]
    while len(groups) > 1:
        groups = [_top_merge(groups[g], groups[g + 1]) for g in range(0, len(groups), 2)]
    return groups[0]


def _dot_nt_3pass(a, b):
    a_hi, b_hi = a.astype(BF16), b.astype(BF16)
    a_lo = (a - a_hi.astype(F32)).astype(BF16)
    b_lo = (b - b_hi.astype(F32)).astype(BF16)
    dot = functools.partial(lax.dot_general, dimension_numbers=NT_DIMS, preferred_element_type=F32)
    return dot(a_hi, b_hi) + (dot(a_hi, b_lo) + dot(a_lo, b_hi))


def _router_kernel(pq_ref, keys_ref, st_ref, sc_ref, scr_ref):
    nch = TT_ROUTER // LANES
    tops = []
    for c in range(2):
        qc = pq_ref[:, c * PEER_HALF:(c + 1) * PEER_HALF]
        s_t = _dot_nt_3pass(keys_ref[0, c], qc)
        st_ref[0, c] = s_t
        for ch in range(nch):
            scr_ref[c, pl.ds(ch, PEER_NKEYS, stride=nch), :] = s_t[:, ch * LANES:(ch + 1) * LANES]
        vals = [scr_ref[c, k * nch:(k + 1) * nch, :] for k in range(PEER_NKEYS)]
        tops.append(_sorted_top16(vals))
    a, b = tops
    cands = [a[r] + b[c] for r in range(PEER_TOPK) for c in range(PEER_TOPK)
             if (r + 1) * (c + 1) <= PEER_TOPK]
    npad = 1 << (len(cands) - 1).bit_length()
    cands = cands + [jnp.full_like(cands[0], NEG)] * (npad - len(cands))
    best = _sort_desc(cands)[:PEER_TOPK]
    z = jnp.ones_like(best[0])
    for r in range(1, PEER_TOPK):
        z = z + jnp.exp(best[r] - best[0])
    for r in range(PEER_TOPK):
        sc_ref[0, SC_TOP1 + r] = a[r]
        sc_ref[0, SC_TOP2 + r] = b[r]
    sc_ref[0, SC_TAU] = best[PEER_TOPK - 1]
    sc_ref[0, SC_ZINV] = 1.0 / z


def _peer_router(pq, sub_keys):
    t = pq.shape[0]
    tt = TT_ROUTER
    nch = tt // LANES
    hp = PEER_HEADS
    return pl.pallas_call(
        _router_kernel,
        grid=(t // tt, hp),
        in_specs=[
            pl.BlockSpec((tt, 2 * PEER_HALF), lambda i, h: (i, h)),
            pl.BlockSpec((1, 2, PEER_NKEYS, PEER_HALF), lambda i, h: (h, 0, 0, 0)),
        ],
        out_specs=[
            pl.BlockSpec((1, 2, PEER_NKEYS, tt), lambda i, h: (h, 0, 0, i)),
            pl.BlockSpec((1, SC_ROWS, nch, LANES), lambda i, h: (h, 0, i, 0)),
        ],
        out_shape=[
            jax.ShapeDtypeStruct((hp, 2, PEER_NKEYS, t), F32),
            jax.ShapeDtypeStruct((hp, SC_ROWS, t // LANES, LANES), F32),
        ],
        scratch_shapes=[pltpu.VMEM((2, PEER_NKEYS * nch, LANES), F32)],
        compiler_params=_cparams(("parallel", "parallel")),
        name="peer_router",
    )(pq, sub_keys)


def _gelu2(x):
    c0 = math.sqrt(2.0 / math.pi)
    z = x * (c0 + (c0 * 0.044715) * (x * x))
    return x + x * jnp.tanh(z)


def _dup_bf16_bits(v):
    bits = pltpu.bitcast(v.astype(BF16).astype(F32), jnp.uint32)
    return bits | (bits >> 16)


def _peer_kernel(h_ref, st_ref, sc_ref, u_ref, vt_ref, x_ref, gfin_ref, o_ref,
                 rank_scr, bw_scr, cnt_scr, aw_scr, act_scr, wa_scr, acc_scr, *, final_norm):
    e = pl.program_id(1)
    tt = h_ref.shape[0]
    nsub = CE_STEP // CE_SUB
    ipc = CE_SUB // PEER_NKEYS
    pack = 2 * SUBLANES

    @pl.when(e == 0)
    def _():
        acc_scr[...] = jnp.zeros_like(acc_scr)
        for hd in range(PEER_HEADS):
            s1 = st_ref[hd, 0]
            s2 = st_ref[hd, 1]
            tau = sc_ref[hd, SC_TAU:SC_TAU + 1, :]
            rank2 = jnp.zeros_like(s2)
            cnt = jnp.zeros_like(s1)
            for c in range(PEER_TOPK):
                b_c = sc_ref[hd, SC_TOP2 + c:SC_TOP2 + c + 1, :]
                rank2 = jnp.where(b_c > s2, c + 1.0, rank2)
                cnt = jnp.where(s1 + b_c >= tau, c + 1.0, cnt)
            a0 = sc_ref[hd, SC_TOP1:SC_TOP1 + 1, :]
            b0 = sc_ref[hd, SC_TOP2:SC_TOP2 + 1, :]
            zinv = sc_ref[hd, SC_ZINV:SC_ZINV + 1, :]
            rank_scr[hd] = rank2.astype(BF16)
            bw_scr[hd] = (jnp.exp(s2 - b0) * (0.5 * zinv)).astype(BF16)
            cnt_w = _dup_bf16_bits(cnt)
            a_w = _dup_bf16_bits(jnp.exp(s1 - a0))
            for lg in range(tt // LANES):
                cnt_scr[hd, lg] = cnt_w[:, lg * LANES:(lg + 1) * LANES]
                aw_scr[hd, lg] = a_w[:, lg * LANES:(lg + 1) * LANES]

    def row_bf16(tile, r):
        words = jnp.broadcast_to(tile[r:r + 1, :], (SUBLANES, LANES))
        return pltpu.bitcast(words, BF16)

    def up_proj(c):
        act_scr[c % 2] = lax.dot_general(u_ref[c * CE_SUB:(c + 1) * CE_SUB, :], h_ref[...], NT_DIMS,
                                         preferred_element_type=F32)

    def down_proj(c):
        acc_scr[...] += jnp.dot(vt_ref[:, c * CE_SUB:(c + 1) * CE_SUB], wa_scr[c % 2],
                                preferred_element_type=F32)

    def gate(c):
        njg = PEER_NKEYS // pack
        zero = jnp.zeros((pack, LANES), BF16)
        for lg in range(tt // LANES):
            lanes = slice(lg * LANES, (lg + 1) * LANES)
            w = [[None] * njg for _ in range(ipc)]
            for hd in range(PEER_HEADS):
                rank = [rank_scr[hd, jg * pack:(jg + 1) * pack, lanes] for jg in range(njg)]
                bw = [bw_scr[hd, jg * pack:(jg + 1) * pack, lanes] for jg in range(njg)]
                first = (e * nsub + c) * ipc
                base = pl.multiple_of((first // SUBLANES) * SUBLANES, SUBLANES)
                off = (c * ipc) % SUBLANES
                cnt_tile = cnt_scr[hd, lg, pl.ds(base, SUBLANES), :]
                a_tile = aw_scr[hd, lg, pl.ds(base, SUBLANES), :]
                for il in range(ipc):
                    cnt_row = row_bf16(cnt_tile, off + il)
                    a_row = row_bf16(a_tile, off + il)
                    for jg in range(njg):
                        term = jnp.where(rank[jg] < cnt_row, a_row * bw[jg], zero)
                        w[il][jg] = term if w[il][jg] is None else w[il][jg] + term
            for il in range(ipc):
                for jg in range(njg):
                    rows = slice(il * PEER_NKEYS + jg * pack, il * PEER_NKEYS + (jg + 1) * pack)
                    wa_scr[c % 2, rows, lanes] = w[il][jg] * _gelu2(act_scr[c % 2, rows, lanes].astype(BF16))

    up_proj(0)
    for c in range(nsub):
        if c + 1 < nsub:
            up_proj(c + 1)
        if c >= 1:
            down_proj(c - 1)
        gate(c)
    down_proj(nsub - 1)

    @pl.when(e == pl.num_programs(1) - 1)
    def _():
        y = x_ref[...] + acc_scr[...].T
        if final_norm:
            y = _rms(y, gfin_ref[...])
        o_ref[...] = y


def _peer_experts(h_bf16, st, sc, u_bf16, vt_bf16, x2d, gfin, final_norm):
    t, d = x2d.shape
    ne = u_bf16.shape[0]
    tt, ce = TT_PEER, CE_STEP
    hp = PEER_HEADS
    assert (ce // PEER_NKEYS) % SUBLANES == 0 and SUBLANES % (CE_SUB // PEER_NKEYS) == 0
    return pl.pallas_call(
        functools.partial(_peer_kernel, final_norm=final_norm),
        grid=(t // tt, ne // ce),
        in_specs=[
            pl.BlockSpec((tt, d), lambda i, e: (i, 0)),
            pl.BlockSpec((hp, 2, PEER_NKEYS, tt), lambda i, e: (0, 0, 0, i)),
            pl.BlockSpec((hp, SC_ROWS, tt), lambda i, e: (0, 0, i)),
            pl.BlockSpec((ce, d), lambda i, e: (e, 0)),
            pl.BlockSpec((None, d, ce), lambda i, e: (e, 0, 0)),
            pl.BlockSpec((tt, d), lambda i, e: (i, 0)),
            pl.BlockSpec((1, d), lambda i, e: (0, 0)),
        ],
        out_specs=pl.BlockSpec((tt, d), lambda i, e: (i, 0)),
        out_shape=jax.ShapeDtypeStruct((t, d), F32),
        scratch_shapes=[
            pltpu.VMEM((hp, PEER_NKEYS, tt), BF16),
            pltpu.VMEM((hp, PEER_NKEYS, tt), BF16),
            pltpu.VMEM((hp, tt // LANES, PEER_NKEYS, LANES), jnp.uint32),
            pltpu.VMEM((hp, tt // LANES, PEER_NKEYS, LANES), jnp.uint32),
            pltpu.VMEM((2, CE_SUB, tt), F32),
            pltpu.VMEM((2, CE_SUB, tt), BF16),
            pltpu.VMEM((d, tt), F32),
        ],
        compiler_params=_cparams(("parallel", "arbitrary")),
        name="peer_experts",
    )(h_bf16, st, sc.reshape(hp, SC_ROWS, t), u_bf16, vt_bf16, x2d, gfin.reshape(1, d))


def _expert_major_t(v):
    ne, d = v.shape
    return v.astype(BF16).reshape(ne // CE_STEP, CE_STEP, d).transpose(0, 2, 1)


def _peer(h_bf16, pq, sub_keys, u_bf16, vt_bf16, x2d, gfin, final_norm):
    st, sc = _peer_router(pq, sub_keys)
    return _peer_experts(h_bf16, st, sc, u_bf16, vt_bf16, x2d, gfin, final_norm)


def _glu_kernel(x_ref, g_ref, w_ref, b_ref, u_ref, *, d):
    h = _rms(x_ref[...], g_ref[...]).astype(BF16)
    a = jnp.dot(h, w_ref[...], preferred_element_type=F32) + b_ref[...]
    u_ref[...] = a[:, :d] * jax.nn.sigmoid(a[:, d:])


def _glu_proj(x2d, g, w_bf16, bias):
    t, d = x2d.shape
    tm = TM_PROJ
    return pl.pallas_call(
        functools.partial(_glu_kernel, d=d),
        grid=(t // tm,),
        in_specs=[
            pl.BlockSpec((tm, d), lambda i: (i, 0)),
            pl.BlockSpec((1, d), lambda i: (0, 0)),
            pl.BlockSpec((d, 2 * d), lambda i: (0, 0)),
            pl.BlockSpec((1, 2 * d), lambda i: (0, 0)),
        ],
        out_specs=pl.BlockSpec((tm, d), lambda i: (i, 0)),
        out_shape=jax.ShapeDtypeStruct((t, d), F32),
        compiler_params=_cparams(("parallel",)),
        name="glu_proj",
    )(x2d, g.reshape(1, d), w_bf16, bias.reshape(1, 2 * d))


def _conv_kernel(ucur_ref, uprev_ref, wdw_ref, bdw_ref, lng_ref, lnb_ref, w2_ref, b2_ref,
                 x_ref, gf_ref, wpq_ref, x_out_ref, h_ref, pq_ref, ext_scr, y_scr):
    tm, d = ucur_ref.shape
    first = pl.program_id(1) == 0
    ext_scr[0:CONV_HALO, :] = jnp.where(first, 0.0, uprev_ref[...])
    ext_scr[CONV_HALO:, :] = ucur_ref[...]
    lead = CONV_HALO - (CONV_WIDTH - 1)

    def chunk(ci, carry):
        r0 = pl.multiple_of(ci * CONV_ROWS, CONV_ROWS)
        rows = CONV_ROWS + CONV_HALO
        window = ext_scr[pl.ds(r0, rows), :]
        shifted = [window] + [pltpu.roll(window, rows - r, 0) for r in range(1, SUBLANES)]
        acc = jnp.broadcast_to(bdw_ref[...], (CONV_ROWS, d))
        for w in range(CONV_WIDTH):
            start = (lead + w) // SUBLANES * SUBLANES
            tap = shifted[(lead + w) % SUBLANES][start:start + CONV_ROWS, :]
            acc = acc + wdw_ref[w:w + 1, :] * tap
        y_scr[pl.ds(r0, CONV_ROWS), :] = acc
        return carry

    lax.fori_loop(0, tm // CONV_ROWS, chunk, 0)
    y = y_scr[...]
    mu = jnp.mean(y, axis=-1, keepdims=True)
    var = jnp.mean(jnp.square(y - mu), axis=-1, keepdims=True)
    y = (y - mu) * lax.rsqrt(var + EPS) * lng_ref[...] + lnb_ref[...]
    y = y * jax.nn.sigmoid(y)
    x_new = x_ref[...] + jnp.dot(y.astype(BF16), w2_ref[...], preferred_element_type=F32) + b2_ref[...]
    _peerq_epilogue(x_new, gf_ref, wpq_ref, x_out_ref, h_ref, pq_ref)


def _conv_tail(u3, wdw, bdw, lng, lnb, w2_bf16, b2, x3, gf, wpq_bf16):
    b, s, d = x3.shape
    nq = wpq_bf16.shape[1]
    tm = TM_CONV
    hpb = tm // CONV_HALO
    row = lambda v: v.reshape(1, d)
    tok = pl.BlockSpec((None, tm, d), lambda bi, i: (bi, i, 0))
    vec = pl.BlockSpec((1, d), lambda bi, i: (0, 0))
    return pl.pallas_call(
        _conv_kernel,
        grid=(b, s // tm),
        in_specs=[
            tok,
            pl.BlockSpec((None, CONV_HALO, d), lambda bi, i: (bi, jnp.maximum(i * hpb - 1, 0), 0)),
            pl.BlockSpec((CONV_WIDTH, d), lambda bi, i: (0, 0)),
            vec, vec, vec,
            pl.BlockSpec((d, d), lambda bi, i: (0, 0)),
            vec,
            tok,
            vec,
            pl.BlockSpec((d, nq), lambda bi, i: (0, 0)),
        ],
        out_specs=[
            tok,
            tok,
            pl.BlockSpec((None, tm, nq), lambda bi, i: (bi, i, 0)),
        ],
        out_shape=[
            jax.ShapeDtypeStruct((b, s, d), F32),
            jax.ShapeDtypeStruct((b, s, d), BF16),
            jax.ShapeDtypeStruct((b, s, nq), F32),
        ],
        scratch_shapes=[
            pltpu.VMEM((tm + CONV_HALO, d), F32),
            pltpu.VMEM((tm, d), F32),
        ],
        compiler_params=_cparams(("parallel", "arbitrary")),
        name="conv_tail",
    )(u3, u3, wdw, row(bdw), row(lng), row(lnb), w2_bf16, row(b2), x3, row(gf), wpq_bf16)


def kernel(x, rel_bias, norm_mix, norm_ffn, attn_w_qkv, attn_w_o, conv_w_pw1, conv_b_pw1,
           conv_w_dw, conv_b_dw, conv_ln_g, conv_ln_b, conv_w_pw2, conv_b_pw2, peer_w_q,
           peer_sub_keys, peer_u, peer_v, norm_final):
    b, s, d = x.shape
    t = b * s
    depth = norm_mix.shape[0]
    assert d == N_HEADS * HEAD_DIM and s % MOBA_BLOCK == 0 and s // MOBA_BLOCK <= LANES
    assert t % TT_ROUTER == 0 and s % TM_CONV == 0 and t % TM_PROJ == 0
    x2d = x.reshape(t, d)
    for i in range(depth):
        j = i // 2
        wpq = peer_w_q[i].astype(BF16)
        if i % 2 == 0:
            wq, wk, wv = (attn_w_qkv[j][:, c * d:(c + 1) * d] for c in range(3))
            qt, k, vt, kmean = _qkv_proj(x2d.reshape(b, s, d), norm_mix[i], wq.T.astype(BF16),
                                         wk.astype(BF16), wv.T.astype(BF16))
            o = _moba_attention(qt, k, vt, kmean, _moba_bias_tables(rel_bias))
            x3, h3, pq3 = _attn_out_proj(o, attn_w_o[j].astype(BF16), x2d.reshape(b, s, d),
                                         norm_ffn[i], wpq)
            x2d, h, pq = x3.reshape(t, d), h3.reshape(t, d), pq3.reshape(t, -1)
        else:
            u = _glu_proj(x2d, norm_mix[i], conv_w_pw1[j].astype(BF16), conv_b_pw1[j])
            x3, h3, pq3 = _conv_tail(u.reshape(b, s, d), conv_w_dw[j], conv_b_dw[j], conv_ln_g[j],
                                     conv_ln_b[j], conv_w_pw2[j].astype(BF16), conv_b_pw2[j],
                                     x2d.reshape(b, s, d), norm_ffn[i], wpq)
            x2d, h, pq = x3.reshape(t, d), h3.reshape(t, d), pq3.reshape(t, -1)
        last = i == depth - 1
        x2d = _peer(h, pq, peer_sub_keys[i], peer_u[i].astype(BF16),
                    _expert_major_t(peer_v[i]), x2d, norm_final, final_norm=last)
    return x2d.reshape(b, s, d)
```
